```python
import math
import jax
import jax.numpy as jnp
from jax import lax
import numpy as np

D_MODEL = 2048
BATCH = 16
SEQ = 2048
DEPTH = 2

CHUNK = 64
N_EVEN = (DEPTH + 1) // 2
N_ODD = DEPTH // 2
EPS = 1e-6

SGU_BLOCK = 128
WA = D_MODEL // 2
A_HEADS = 8
A_HEAD_DIM = WA // A_HEADS
WB = D_MODEL // 2
POOL_WINDOWS = (2, 4, 8, 16)
B_GROUPS = len(POOL_WINDOWS)
GB = WB // B_GROUPS
WC = D_MODEL // 2
SHORT_CONV = 3
WD = D_MODEL // 2
CONF_CONV = 31
N_EXPERTS = 64
TOP_K = 6
N_GROUPS = 8
TOPK_GROUPS = 4
EXPERT_FF = 384
SHARED_FF = D_MODEL // 4
ROUTED_SCALE = 2.5
MOE_BLOCK = 512

kernel_name = "hybrid_chunk_causal_sgu_pool_conv_moe"


def rmsnorm(x, g):
    xf = x.astype(jnp.float32)
    y = xf * lax.rsqrt(jnp.mean(xf * xf, axis=-1, keepdims=True) + EPS)
    return (y * g.astype(jnp.float32)).astype(x.dtype)


def layernorm(x, g, b):
    xf = x.astype(jnp.float32)
    mu = jnp.mean(xf, axis=-1, keepdims=True)
    var = jnp.mean(jnp.square(xf - mu), axis=-1, keepdims=True)
    y = (xf - mu) * lax.rsqrt(var + EPS)
    return (y * g.astype(jnp.float32) + b.astype(jnp.float32)).astype(x.dtype)


def causal_dwconv(x, w, b):
    k = w.shape[0]
    y = lax.conv_general_dilated(x, w[:, None, :].astype(x.dtype), window_strides=(1,),
                                 padding=[(k - 1, 0)], dimension_numbers=("NWC", "WIO", "NWC"),
                                 feature_group_count=x.shape[-1])
    return y + b.astype(x.dtype)


def mixer_ab(h, w_in, sgu_ln_g, sgu_ln_b, sgu_w, sgu_b, pool_w, pool_scale, w_out):
    bsz, seq, _ = h.shape
    p = h @ w_in
    u, v, xb = p[..., :WA], p[..., WA:2 * WA], p[..., 2 * WA:]
    v = layernorm(v, sgu_ln_g, sgu_ln_b)
    vb = v.reshape(bsz, seq // SGU_BLOCK, SGU_BLOCK, A_HEADS, A_HEAD_DIM)
    cidx = jnp.arange(SGU_BLOCK) // CHUNK
    mask = cidx[:, None] >= cidx[None, :]
    ws = jnp.where(mask[None], sgu_w, jnp.zeros_like(sgu_w))
    s = jnp.einsum("hij,bnjhd->bnihd", ws, vb) + jnp.transpose(sgu_b)[None, None, :, :, None]
    ya = u * s.reshape(bsz, seq, WA)
    xf = xb.astype(jnp.float32)
    cs0 = jnp.concatenate([jnp.zeros((bsz, 1, WB), jnp.float32), jnp.cumsum(xf, axis=1)], axis=1)
    pos_count = jnp.arange(1, seq + 1, dtype=jnp.float32)
    groups = []
    for gi, win in enumerate(POOL_WINDOWS):
        sl = cs0[..., gi * GB:(gi + 1) * GB]
        upper = sl[:, 1:]
        lower = jnp.pad(sl[:, :seq - win + 1], ((0, 0), (win - 1, 0), (0, 0)))
        cnt = jnp.minimum(pos_count, float(win))[None, :, None]
        groups.append((upper - lower) / cnt - xf[..., gi * GB:(gi + 1) * GB])
    pooled = jnp.stack(groups, axis=2).astype(xb.dtype)
    yb = jnp.einsum("bsgc,gcd->bsgd", pooled, pool_w).reshape(bsz, seq, WB) * pool_scale
    return jnp.concatenate([ya, yb], axis=-1) @ w_out


def mixer_cd(h, w_in, conv3_w, conv3_b, conv31_w, conv31_b, ln_g, ln_b, w_out):
    p = h @ w_in
    bg = p[..., :WC]
    cg = p[..., WC:2 * WC]
    xc = p[..., 2 * WC:3 * WC]
    da = p[..., 3 * WC:3 * WC + WD]
    dg = p[..., 3 * WC + WD:]
    yc = bg * causal_dwconv(cg * xc, conv3_w, conv3_b)
    d = da * jax.nn.sigmoid(dg)
    d = causal_dwconv(d, conv31_w, conv31_b)
    d = jax.nn.silu(layernorm(d, ln_g, ln_b))
    return jnp.concatenate([yc, d], axis=-1) @ w_out


def moe(h, router_w, router_b, w1, w3, w2, sw1, sw3, sw2):
    bsz, seq, dm = h.shape
    xt = h.reshape(-1, dm)
    n = xt.shape[0]
    nk = n * TOP_K
    scores = jax.nn.sigmoid((xt @ router_w).astype(jnp.float32))
    choice = scores + router_b.astype(jnp.float32)
    grp = choice.reshape(n, N_GROUPS, N_EXPERTS // N_GROUPS)
    grp_score = lax.top_k(grp, 2)[0].sum(-1)
    _, gidx = lax.top_k(grp_score, TOPK_GROUPS)
    gmask = jax.nn.one_hot(gidx, N_GROUPS, dtype=jnp.float32).sum(1) > 0
    emask = jnp.repeat(gmask, N_EXPERTS // N_GROUPS, axis=1)
    masked = jnp.where(emask, choice, -jnp.inf)
    _, eidx = lax.top_k(masked, TOP_K)
    wts = jnp.take_along_axis(scores, eidx, axis=1)
    wts = wts / jnp.sum(wts, axis=-1, keepdims=True) * ROUTED_SCALE
    flat_e = eidx.reshape(-1)
    order = jnp.argsort(flat_e)
    sorted_e = flat_e[order]
    tok = order // TOP_K
    counts = jnp.bincount(flat_e, length=N_EXPERTS).astype(jnp.int32)
    pcounts = (counts + MOE_BLOCK - 1) // MOE_BLOCK * MOE_BLOCK
    offs = jnp.cumsum(counts) - counts
    pends = jnp.cumsum(pcounts)
    poffs = pends - pcounts
    dest = poffs[sorted_e] + (jnp.arange(nk, dtype=jnp.int32) - offs[sorted_e])
    nb = -(-nk // MOE_BLOCK) + N_EXPERTS
    xbuf = jnp.zeros((nb * MOE_BLOCK, dm), xt.dtype).at[dest].set(xt[tok])
    block_e = jnp.minimum(jnp.searchsorted(pends, jnp.arange(nb, dtype=jnp.int32) * MOE_BLOCK,
                                           side="right"), N_EXPERTS - 1)
    xb = xbuf.reshape(nb, MOE_BLOCK, dm)
    a = jax.nn.silu(jnp.einsum("brd,bdf->brf", xb, w1[block_e])) * \
        jnp.einsum("brd,bdf->brf", xb, w3[block_e])
    yb = jnp.einsum("brf,bfd->brd", a, w2[block_e]).reshape(-1, dm)
    y = yb[dest] * wts.reshape(-1)[order][:, None].astype(yb.dtype)
    routed = jax.ops.segment_sum(y, tok, num_segments=n)
    shared = (jax.nn.silu(xt @ sw1) * (xt @ sw3)) @ sw2
    return (routed + shared).reshape(bsz, seq, dm)


def setup_inputs(seed: int = 0) -> dict:
    key = jax.random.key(seed)
    keys = iter(jax.random.split(key, 48))

    def nrm(shape, scale):
        return jax.random.normal(next(keys), shape, jnp.float32) * scale

    def gain(shape):
        return 1.0 + nrm(shape, 0.05)

    d = D_MODEL
    return {
        "x": nrm((BATCH, SEQ, d), 1.0),
        "c": nrm((BATCH, d), 1.0),
        "ada_w": nrm((DEPTH, d, 6 * d), 0.5 * d ** -0.5),
        "ada_b": nrm((DEPTH, 6 * d), 0.02),
        "norm_mix_g": gain((DEPTH, d)),
        "norm_ffn_g": gain((DEPTH, d)),
        "ab_w_in": nrm((N_EVEN, d, 2 * WA + WB), d ** -0.5),
        "sgu_ln_g": gain((N_EVEN, WA)),
        "sgu_ln_b": nrm((N_EVEN, WA), 0.02),
        "sgu_w": nrm((N_EVEN, A_HEADS, SGU_BLOCK, SGU_BLOCK), SGU_BLOCK ** -0.5),
        "sgu_b": gain((N_EVEN, A_HEADS, SGU_BLOCK)),
        "pool_w": nrm((N_EVEN, B_GROUPS, GB, GB), GB ** -0.5),
        "pool_scale": gain((N_EVEN, WB)),
        "ab_w_out": nrm((N_EVEN, WA + WB, d), (WA + WB) ** -0.5),
        "cd_w_in": nrm((N_ODD, d, 3 * WC + 2 * WD), d ** -0.5),
        "conv3_w": nrm((N_ODD, SHORT_CONV, WC), SHORT_CONV ** -0.5),
        "conv3_b": nrm((N_ODD, WC), 0.02),
        "conv31_w": nrm((N_ODD, CONF_CONV, WD), CONF_CONV ** -0.5),
        "conv31_b": nrm((N_ODD, WD), 0.02),
        "cd_ln_g": gain((N_ODD, WD)),
        "cd_ln_b": nrm((N_ODD, WD), 0.02),
        "cd_w_out": nrm((N_ODD, WC + WD, d), (WC + WD) ** -0.5),
        "router_w": nrm((DEPTH, d, N_EXPERTS), d ** -0.5),
        "router_b": nrm((DEPTH, N_EXPERTS), 0.01),
        "exp_w1": nrm((DEPTH, N_EXPERTS, d, EXPERT_FF), d ** -0.5),
        "exp_w3": nrm((DEPTH, N_EXPERTS, d, EXPERT_FF), d ** -0.5),
        "exp_w2": nrm((DEPTH, N_EXPERTS, EXPERT_FF, d), EXPERT_FF ** -0.5),
        "sh_w1": nrm((DEPTH, d, SHARED_FF), d ** -0.5),
        "sh_w3": nrm((DEPTH, d, SHARED_FF), d ** -0.5),
        "sh_w2": nrm((DEPTH, SHARED_FF, d), SHARED_FF ** -0.5),
        "final_g": gain((d,)),
    }


def reference(x, c, ada_w, ada_b, norm_mix_g, norm_ffn_g,
              ab_w_in, sgu_ln_g, sgu_ln_b, sgu_w, sgu_b, pool_w, pool_scale, ab_w_out,
              cd_w_in, conv3_w, conv3_b, conv31_w, conv31_b, cd_ln_g, cd_ln_b, cd_w_out,
              router_w, router_b, exp_w1, exp_w3, exp_w2, sh_w1, sh_w3, sh_w2, final_g):
    c_act = jax.nn.silu(c)
    for l in range(DEPTH):
        mod = c_act @ ada_w[l] + ada_b[l]
        sh1, sc1, g1, sh2, sc2, g2 = [m[:, None, :] for m in jnp.split(mod, 6, axis=-1)]
        h = rmsnorm(x, norm_mix_g[l]) * (1.0 + sc1) + sh1
        if l % 2 == 0:
            e = l // 2
            mix = mixer_ab(h, ab_w_in[e], sgu_ln_g[e], sgu_ln_b[e], sgu_w[e], sgu_b[e],
                           pool_w[e], pool_scale[e], ab_w_out[e])
        else:
            o = l // 2
            mix = mixer_cd(h, cd_w_in[o], conv3_w[o], conv3_b[o], conv31_w[o], conv31_b[o],
                           cd_ln_g[o], cd_ln_b[o], cd_w_out[o])
        x = x + g1 * mix
        h = rmsnorm(x, norm_ffn_g[l]) * (1.0 + sc2) + sh2
        x = x + g2 * moe(h, router_w[l], router_b[l], exp_w1[l], exp_w3[l], exp_w2[l],
                         sh_w1[l], sh_w3[l], sh_w2[l])
    return rmsnorm(x, final_g)
```

```python
import functools

import jax
import jax.numpy as jnp
from jax import lax
from jax.experimental import pallas as pl
from jax.experimental.pallas import tpu as pltpu

f32 = jnp.float32
bf16 = jnp.bfloat16
u32 = jnp.uint32
i32 = jnp.int32

EPS = 1e-6
CHUNK = 64
SGU_BLOCK = 128
A_HEADS = 8
POOL_WINDOWS = (2, 4, 8, 16)
SHORT_CONV = 3
CONF_CONV = 31
N_EXPERTS = 64
TOP_K = 6
N_GROUPS = 8
TOPK_GROUPS = 4
ROUTED_SCALE = 2.5

V7X_LANES = 128
V7X_SUBLANES = 8
V7X_VMEM_LIMIT = 56 * 1024 * 1024

ROW_BLOCK = 512
TILE_MM = 1024
TILE_MIX = 512
TILE_ROUTE = 512
TILE_COMB = 256
CONV_ROWS = 64
HI_MASK = 0xFFFF0000


def _cparams(sem):
    return pltpu.CompilerParams(dimension_semantics=sem, vmem_limit_bytes=V7X_VMEM_LIMIT)


def _pack_halves(v):
    w = v.shape[1] // 2
    bits = pltpu.bitcast(v, u32)
    return (bits[:, :w] >> 16) | bits[:, w:]


def _unpack_halves(words):
    lo = pltpu.bitcast(words << 16, f32)
    hi = pltpu.bitcast(words & jnp.uint32(HI_MASK), f32)
    return lo, hi


def _modulated_rmsnorm(x, g, sc, sh):
    ms = jnp.mean(x * x, axis=-1, keepdims=True)
    y = x * lax.rsqrt(ms + EPS) * g
    return y * (1.0 + sc) + sh


def _ada_kernel(c_ref, w_ref, b_ref, o_ref):
    ca = jax.nn.silu(c_ref[...]).astype(bf16)
    o_ref[0] = jnp.dot(ca, w_ref[0].astype(bf16), preferred_element_type=f32) + b_ref[0]


def _ada(c, ada_w, ada_b):
    depth, d, n6 = ada_w.shape
    bsz = c.shape[0]
    tn = 1024
    return pl.pallas_call(
        _ada_kernel,
        grid=(depth, n6 // tn),
        in_specs=[
            pl.BlockSpec((bsz, d), lambda l, j: (0, 0)),
            pl.BlockSpec((1, d, tn), lambda l, j: (l, 0, j)),
            pl.BlockSpec((1, 1, tn), lambda l, j: (l, 0, j)),
        ],
        out_specs=pl.BlockSpec((1, bsz, tn), lambda l, j: (l, 0, j)),
        out_shape=jax.ShapeDtypeStruct((depth, bsz, n6), f32),
        compiler_params=_cparams(("parallel", "parallel")),
        name="ada_mod",
    )(c, ada_w, ada_b.reshape(depth, 1, n6))


def _norm_mm_kernel(x_ref, g_ref, sc_ref, sh_ref, w_ref, o_ref, h_ref):
    @pl.when(pl.program_id(1) == 0)
    def _():
        h_ref[...] = _modulated_rmsnorm(x_ref[...], g_ref[...], sc_ref[0], sh_ref[0]).astype(bf16)

    o_ref[...] = jnp.dot(h_ref[...], w_ref[...], preferred_element_type=f32).astype(o_ref.dtype)


def _norm_mm(x, g, sc, sh, w, seq):
    n, d = x.shape
    nout = w.shape[1]
    tm = min(TILE_MM, seq)
    tn = 1024
    tiles_per_seq = seq // tm
    return pl.pallas_call(
        _norm_mm_kernel,
        grid=(n // tm, nout // tn),
        in_specs=[
            pl.BlockSpec((tm, d), lambda i, j: (i, 0)),
            pl.BlockSpec((1, d), lambda i, j: (0, 0)),
            pl.BlockSpec((1, 1, d), lambda i, j: (i // tiles_per_seq, 0, 0)),
            pl.BlockSpec((1, 1, d), lambda i, j: (i // tiles_per_seq, 0, 0)),
            pl.BlockSpec((d, tn), lambda i, j: (0, j)),
        ],
        out_specs=pl.BlockSpec((tm, tn), lambda i, j: (i, j)),
        out_shape=jax.ShapeDtypeStruct((n, nout), bf16),
        scratch_shapes=[pltpu.VMEM((tm, d), bf16)],
        compiler_params=_cparams(("parallel", "arbitrary")),
        name="norm_in_proj",
    )(x, g.reshape(1, d), sc, sh, w)


def _ab_mix_kernel(x_ref, p_ref, g1_ref, lng_ref, lnb_ref, sw_ref, sbt_ref, pw_ref, ps_ref, wo_ref,
                   o_ref, ext_ref, cat_ref):
    s_idx = pl.program_id(1)
    ts = x_ref.shape[0]
    wa = lng_ref.shape[1]
    hd = wa // A_HEADS
    gb = pw_ref.shape[1]
    blk = SGU_BLOCK

    @pl.when(s_idx == 0)
    def _():
        ext_ref[0:blk, :] = jnp.zeros((blk, ext_ref.shape[1]), bf16)

    @pl.when(s_idx > 0)
    def _():
        ext_ref[0:blk, :] = ext_ref[ts:ts + blk, :]

    ext_ref[blk:, :] = p_ref[:, 2 * wa:]

    ci = lax.broadcasted_iota(i32, (blk, blk), 0) // CHUNK
    cj = lax.broadcasted_iota(i32, (blk, blk), 1) // CHUNK
    causal = ci >= cj
    trow = lax.broadcasted_iota(i32, (blk, 2 * blk), 0) + blk
    tcol = lax.broadcasted_iota(i32, (blk, 2 * blk), 1)
    tpos = lax.broadcasted_iota(i32, (blk, 1), 0)

    def block_body(bi, carry):
        r0 = pl.multiple_of(bi * blk, blk)
        rows = pl.ds(r0, blk)
        v = p_ref[rows, wa:2 * wa].astype(f32)
        mu = jnp.mean(v, axis=-1, keepdims=True)
        vc = v - mu
        var = jnp.mean(vc * vc, axis=-1, keepdims=True)
        vn = (vc * lax.rsqrt(var + EPS) * lng_ref[...] + lnb_ref[...]).astype(bf16)
        for h in range(A_HEADS):
            cols = slice(h * hd, (h + 1) * hd)
            wm = jnp.where(causal, sw_ref[h], 0.0).astype(bf16)
            s = jnp.dot(wm, vn[:, cols], preferred_element_type=f32) + sbt_ref[:, h:h + 1]
            u = p_ref[rows, cols].astype(f32)
            cat_ref[rows, cols] = (u * s).astype(bf16)
        win_rows = pl.ds(r0, 2 * blk)
        pos = (s_idx * ts + r0 + tpos + 1).astype(f32)
        for gi, win in enumerate(POOL_WINDOWS):
            cols = slice(gi * gb, (gi + 1) * gb)
            band = jnp.where((tcol <= trow) & (tcol > trow - win), 1.0, 0.0).astype(bf16)
            window = ext_ref[win_rows, cols]
            sums = jnp.dot(band, window, preferred_element_type=f32)
            xb = window[blk:, :].astype(f32)
            pooled = sums / jnp.minimum(pos, float(win)) - xb
            yb = jnp.dot(pooled.astype(bf16), pw_ref[gi], preferred_element_type=f32) * ps_ref[:, cols]
            cat_ref[rows, wa + gi * gb:wa + (gi + 1) * gb] = yb.astype(bf16)
        return carry

    lax.fori_loop(0, ts // blk, block_body, 0)
    mix = jnp.dot(cat_ref[...], wo_ref[...], preferred_element_type=f32)
    o_ref[...] = x_ref[...] + g1_ref[0] * mix


def _ab_mix(x, p, g1, ln_g, ln_b, sgu_w, sgu_b, pool_w, pool_scale, w_out, bsz, seq):
    n, d = x.shape
    wa = ln_g.shape[0]
    wb = pool_scale.shape[0]
    ts = min(TILE_MIX, seq)
    nst = seq // ts
    const2 = lambda b, s: (0, 0)
    const3 = lambda b, s: (0, 0, 0)
    return pl.pallas_call(
        _ab_mix_kernel,
        grid=(bsz, nst),
        in_specs=[
            pl.BlockSpec((ts, d), lambda b, s: (b * nst + s, 0)),
            pl.BlockSpec((ts, p.shape[1]), lambda b, s: (b * nst + s, 0)),
            pl.BlockSpec((1, 1, d), lambda b, s: (b, 0, 0)),
            pl.BlockSpec((1, wa), const2),
            pl.BlockSpec((1, wa), const2),
            pl.BlockSpec(sgu_w.shape, const3),
            pl.BlockSpec((SGU_BLOCK, A_HEADS), const2),
            pl.BlockSpec(pool_w.shape, const3),
            pl.BlockSpec((1, wb), const2),
            pl.BlockSpec(w_out.shape, const2),
        ],
        out_specs=pl.BlockSpec((ts, d), lambda b, s: (b * nst + s, 0)),
        out_shape=jax.ShapeDtypeStruct((n, d), f32),
        scratch_shapes=[pltpu.VMEM((ts + SGU_BLOCK, wb), bf16), pltpu.VMEM((ts, wa + wb), bf16)],
        compiler_params=_cparams(("parallel", "arbitrary")),
        name="mixer_ab",
    )(x, p, g1, ln_g.reshape(1, wa), ln_b.reshape(1, wa), sgu_w, sgu_b.T, pool_w.astype(bf16),
      pool_scale.reshape(1, wb), w_out)


def _cd_mix_kernel(x_ref, p_ref, g1_ref, c3w_ref, c3b_ref, c31w_ref, c31b_ref, lng_ref, lnb_ref, wo_ref,
                   o_ref, e3_ref, e31_ref, cat_ref):
    s_idx = pl.program_id(1)
    ts = x_ref.shape[0]
    wc = c3b_ref.shape[1]
    wd = c31b_ref.shape[1]
    h3 = e3_ref.shape[0] - ts
    h31 = e31_ref.shape[0] - ts

    @pl.when(s_idx == 0)
    def _():
        e3_ref[0:h3, :] = jnp.zeros((h3, wc), f32)
        e31_ref[0:h31, :] = jnp.zeros((h31, wd), f32)

    @pl.when(s_idx > 0)
    def _():
        e3_ref[0:h3, :] = e3_ref[ts:ts + h3, :]
        e31_ref[0:h31, :] = e31_ref[ts:ts + h31, :]

    cg = p_ref[:, wc:2 * wc].astype(f32)
    xc = p_ref[:, 2 * wc:3 * wc].astype(f32)
    e3_ref[h3:, :] = cg * xc
    da = p_ref[:, 3 * wc:3 * wc + wd].astype(f32)
    dg = p_ref[:, 3 * wc + wd:].astype(f32)
    e31_ref[h31:, :] = da * jax.nn.sigmoid(dg)

    cw = 256
    rr = CONV_ROWS
    sub = V7X_SUBLANES

    def causal_conv(w_ref, b_ref, e_ref, halo, r0, cols):
        ntaps = w_ref.shape[0]
        acc = jnp.broadcast_to(b_ref[:, cols], (rr, cw))
        for b in range(sub):
            taps = [k for k in range(ntaps) if (halo - (ntaps - 1) + k) % sub == b]
            if not taps:
                continue
            nload = rr + sub if b else rr
            z = None
            for k in taps:
                a8 = (halo - (ntaps - 1) + k) // sub * sub
                t = w_ref[k:k + 1, cols] * e_ref[pl.ds(r0 + a8, nload), cols]
                z = t if z is None else z + t
            if b:
                z = pltpu.roll(z, nload - b, 0)[:rr]
            acc = acc + z
        return acc

    def rows_body(ri, carry):
        r0 = pl.multiple_of(ri * rr, rr)
        rows = pl.ds(r0, rr)
        for c in range(wc // cw):
            cols = slice(c * cw, (c + 1) * cw)
            acc = causal_conv(c3w_ref, c3b_ref, e3_ref, h3, r0, cols)
            cat_ref[rows, cols] = (p_ref[rows, cols].astype(f32) * acc).astype(bf16)
        dparts = []
        for c in range(wd // cw):
            cols = slice(c * cw, (c + 1) * cw)
            dparts.append(causal_conv(c31w_ref, c31b_ref, e31_ref, h31, r0, cols))
        dsum = dparts[0].sum(axis=-1, keepdims=True)
        for dp in dparts[1:]:
            dsum = dsum + dp.sum(axis=-1, keepdims=True)
        mu = dsum / wd
        vsum = None
        for dp in dparts:
            t = ((dp - mu) * (dp - mu)).sum(axis=-1, keepdims=True)
            vsum = t if vsum is None else vsum + t
        inv = lax.rsqrt(vsum / wd + EPS)
        for c, dp in enumerate(dparts):
            cols = slice(c * cw, (c + 1) * cw)
            dn = (dp - mu) * inv * lng_ref[:, cols] + lnb_ref[:, cols]
            cat_ref[rows, wc + c * cw:wc + (c + 1) * cw] = jax.nn.silu(dn).astype(bf16)
        return carry

    lax.fori_loop(0, ts // rr, rows_body, 0)
    mix = jnp.dot(cat_ref[...], wo_ref[...], preferred_element_type=f32)
    o_ref[...] = x_ref[...] + g1_ref[0] * mix


def _cd_mix(x, p, g1, c3w, c3b, c31w, c31b, ln_g, ln_b, w_out, bsz, seq):
    n, d = x.shape
    wc = c3b.shape[0]
    wd = c31b.shape[0]
    ts = min(TILE_MIX, seq)
    nst = seq // ts
    h3 = V7X_SUBLANES
    h31 = 4 * V7X_SUBLANES
    assert h3 >= SHORT_CONV - 1 and h31 >= CONF_CONV - 1
    const2 = lambda b, s: (0, 0)
    return pl.pallas_call(
        _cd_mix_kernel,
        grid=(bsz, nst),
        in_specs=[
            pl.BlockSpec((ts, d), lambda b, s: (b * nst + s, 0)),
            pl.BlockSpec((ts, p.shape[1]), lambda b, s: (b * nst + s, 0)),
            pl.BlockSpec((1, 1, d), lambda b, s: (b, 0, 0)),
            pl.BlockSpec(c3w.shape, const2),
            pl.BlockSpec((1, wc), const2),
            pl.BlockSpec(c31w.shape, const2),
            pl.BlockSpec((1, wd), const2),
            pl.BlockSpec((1, wd), const2),
            pl.BlockSpec((1, wd), const2),
            pl.BlockSpec(w_out.shape, const2),
        ],
        out_specs=pl.BlockSpec((ts, d), lambda b, s: (b * nst + s, 0)),
        out_shape=jax.ShapeDtypeStruct((n, d), f32),
        scratch_shapes=[pltpu.VMEM((ts + h3, wc), f32), pltpu.VMEM((ts + h31, wd), f32),
                        pltpu.VMEM((ts, wc + wd), bf16)],
        compiler_params=_cparams(("parallel", "arbitrary")),
        name="mixer_cd",
    )(x, p, g1, c3w, c3b.reshape(1, wc), c31w, c31b.reshape(1, wd), ln_g.reshape(1, wd), ln_b.reshape(1, wd),
      w_out)


def _route_kernel(x_ref, g_ref, sc_ref, sh_ref, rwh_ref, rwl_ref, rb_ref,
                  hp_ref, eidx_ref, wts_ref, rank_ref, cnt_ref, carry_ref):
    tm = x_ref.shape[0]
    ne = rwh_ref.shape[0]
    gsz = ne // N_GROUPS
    neg = -jnp.inf

    @pl.when(pl.program_id(0) == 0)
    def _():
        carry_ref[...] = jnp.zeros_like(carry_ref)

    h = _modulated_rmsnorm(x_ref[...], g_ref[...], sc_ref[0], sh_ref[0])
    hh = h.astype(bf16)
    hf = hh.astype(f32)
    hl = (h - hf).astype(bf16)
    hp_ref[...] = _pack_halves(hf)

    nt = (((1,), (1,)), ((), ()))
    logits = (lax.dot_general(rwh_ref[...], hh, nt, preferred_element_type=f32)
              + lax.dot_general(rwh_ref[...], hl, nt, preferred_element_type=f32)
              + lax.dot_general(rwl_ref[...], hh, nt, preferred_element_type=f32))
    scores = jax.nn.sigmoid(logits)
    choice = scores + rb_ref[...]

    c3 = choice.reshape(N_GROUPS, gsz, tm)
    sub = lax.broadcasted_iota(i32, c3.shape, 1)
    m1 = jnp.max(c3, axis=1, keepdims=True)
    first = jnp.min(jnp.where(c3 == m1, sub, gsz), axis=1, keepdims=True)
    m2 = jnp.max(jnp.where(sub == first, neg, c3), axis=1, keepdims=True)
    gs = m1 + m2
    keep = []
    for gi in range(N_GROUPS):
        beaten = jnp.zeros((1, tm), i32)
        for gj in range(N_GROUPS):
            if gj == gi:
                continue
            wins = (gs[gj] > gs[gi]) | ((gs[gj] == gs[gi]) & (gj < gi))
            beaten = beaten + wins.astype(i32)
        keep.append(beaten < TOPK_GROUPS)
    masked = jnp.concatenate(
        [jnp.where(keep[gi], c3[gi], neg) for gi in range(N_GROUPS)], axis=0)

    eio = lax.broadcasted_iota(i32, (ne, tm), 0)
    cur = masked
    onehot = jnp.zeros((ne, tm), f32)
    idxs, sels, svals = [], [], []
    for _ in range(TOP_K):
        m = jnp.max(cur, axis=0, keepdims=True)
        idx = jnp.min(jnp.where(cur == m, eio, ne), axis=0, keepdims=True)
        sel = eio == idx
        svals.append(jnp.sum(jnp.where(sel, scores, 0.0), axis=0, keepdims=True))
        cur = jnp.where(sel, neg, cur)
        onehot = onehot + sel.astype(f32)
        idxs.append(idx)
        sels.append(sel)
    ssum = svals[0]
    for sv in svals[1:]:
        ssum = ssum + sv

    tr = lax.broadcasted_iota(i32, (tm, tm), 0)
    tc = lax.broadcasted_iota(i32, (tm, tm), 1)
    upper = jnp.where(tr < tc, 1.0, 0.0).astype(bf16)
    before = jnp.dot(onehot.astype(bf16), upper, preferred_element_type=f32) + carry_ref[...]
    new_carry = carry_ref[...] + jnp.sum(onehot, axis=1, keepdims=True)
    carry_ref[...] = new_carry
    cnt_ref[...] = jnp.broadcast_to(new_carry, cnt_ref.shape).astype(i32)

    zrow_i = jnp.zeros((1, tm), i32)
    zrow_f = jnp.zeros((1, tm), f32)
    ranks = [jnp.sum(jnp.where(sel, before, 0.0), axis=0, keepdims=True).astype(i32) for sel in sels]
    pad = V7X_SUBLANES - TOP_K
    eidx_ref[...] = jnp.concatenate(idxs + [zrow_i] * pad, axis=0)
    wts_ref[...] = jnp.concatenate([sv / ssum * ROUTED_SCALE for sv in svals] + [zrow_f] * pad, axis=0)
    rank_ref[...] = jnp.concatenate(ranks + [zrow_i] * pad, axis=0)


def _route(x, g, sc, sh, router_w, router_b, seq):
    n, d = x.shape
    ne = router_w.shape[1]
    tm = min(TILE_ROUTE, seq)
    tiles_per_seq = seq // tm
    rwt = router_w.T
    rwh = rwt.astype(bf16)
    rwl = (rwt - rwh.astype(f32)).astype(bf16)
    slots = V7X_SUBLANES
    return pl.pallas_call(
        _route_kernel,
        grid=(n // tm,),
        in_specs=[
            pl.BlockSpec((tm, d), lambda i: (i, 0)),
            pl.BlockSpec((1, d), lambda i: (0, 0)),
            pl.BlockSpec((1, 1, d), lambda i: (i // tiles_per_seq, 0, 0)),
            pl.BlockSpec((1, 1, d), lambda i: (i // tiles_per_seq, 0, 0)),
            pl.BlockSpec((ne, d), lambda i: (0, 0)),
            pl.BlockSpec((ne, d), lambda i: (0, 0)),
            pl.BlockSpec((ne, 1), lambda i: (0, 0)),
        ],
        out_specs=[
            pl.BlockSpec((tm, d // 2), lambda i: (i, 0)),
            pl.BlockSpec((slots, tm), lambda i: (0, i)),
            pl.BlockSpec((slots, tm), lambda i: (0, i)),
            pl.BlockSpec((slots, tm), lambda i: (0, i)),
            pl.BlockSpec((ne, V7X_LANES), lambda i: (0, 0)),
        ],
        out_shape=[
            jax.ShapeDtypeStruct((n, d // 2), u32),
            jax.ShapeDtypeStruct((slots, n), i32),
            jax.ShapeDtypeStruct((slots, n), f32),
            jax.ShapeDtypeStruct((slots, n), i32),
            jax.ShapeDtypeStruct((ne, V7X_LANES), i32),
        ],
        scratch_shapes=[pltpu.VMEM((ne, 1), f32)],
        compiler_params=_cparams(("arbitrary",)),
        name="route",
    )(x, g.reshape(1, d), sc, sh, rwh, rwl, router_b.reshape(ne, 1))


def _row_gather_start(tok_ref, src_hbm, dst_ref, sem, nrows):
    def body(r, carry):
        t = tok_ref[0, 0, r]
        pltpu.make_async_copy(src_hbm.at[pl.ds(t, 1)], dst_ref.at[pl.ds(r, 1)], sem).start()
        return carry

    lax.fori_loop(0, nrows, body, 0, unroll=8)


def _expert_kernel(be_ref, nu_ref, tokc_ref, tokn_ref, hp_hbm, w13_ref, w2_ref, y_ref, xg_ref, sem):
    b = pl.program_id(0)
    nused = nu_ref[0]
    slot = b % 2
    rows = xg_ref.shape[1]
    half = xg_ref.shape[2]
    ff = w2_ref.shape[1]

    @pl.when(b == 0)
    def _():
        _row_gather_start(tokc_ref, hp_hbm, xg_ref.at[0], sem.at[0], rows)

    @pl.when(b + 1 < nused)
    def _():
        _row_gather_start(tokn_ref, hp_hbm, xg_ref.at[1 - slot], sem.at[1 - slot], rows)

    @pl.when(b < nused)
    def _():
        pltpu.make_async_copy(hp_hbm.at[pl.ds(0, rows)], xg_ref.at[slot], sem.at[slot]).wait()
        lo, hi = _unpack_halves(xg_ref[slot])
        hcat = (jnp.dot(lo.astype(bf16), w13_ref[0, :half, :], preferred_element_type=f32)
                + jnp.dot(hi.astype(bf16), w13_ref[0, half:, :], preferred_element_type=f32))
        a = jax.nn.silu(hcat[:, :ff]) * hcat[:, ff:]
        y = jnp.dot(a.astype(bf16), w2_ref[0], preferred_element_type=f32)
        y_ref[...] = _pack_halves(y.astype(bf16).astype(f32))

    @pl.when(b >= nused)
    def _():
        y_ref[...] = jnp.zeros_like(y_ref)


def _experts(hp, block_e, nused, row_tok, w13, w2):
    n, half = hp.shape
    nb = block_e.shape[0]
    ne, d, ff2 = w13.shape
    ff = w2.shape[1]
    rows = ROW_BLOCK
    tok3 = row_tok.reshape(nb, 1, rows)
    grid_spec = pltpu.PrefetchScalarGridSpec(
        num_scalar_prefetch=2,
        grid=(nb,),
        in_specs=[
            pl.BlockSpec((1, 1, rows), lambda b, be, nu: (b, 0, 0), memory_space=pltpu.SMEM),
            pl.BlockSpec((1, 1, rows), lambda b, be, nu: (jnp.minimum(b + 1, nb - 1), 0, 0),
                         memory_space=pltpu.SMEM),
            pl.BlockSpec(memory_space=pl.ANY),
            pl.BlockSpec((1, d, ff2), lambda b, be, nu: (be[b], 0, 0)),
            pl.BlockSpec((1, ff, d), lambda b, be, nu: (be[b], 0, 0)),
        ],
        out_specs=pl.BlockSpec((rows, half), lambda b, be, nu: (b, 0)),
        scratch_shapes=[pltpu.VMEM((2, rows, half), u32), pltpu.SemaphoreType.DMA((2,))],
    )
    return pl.pallas_call(
        _expert_kernel,
        grid_spec=grid_spec,
        out_shape=jax.ShapeDtypeStruct((nb * rows, half), u32),
        compiler_params=_cparams(("arbitrary",)),
        name="experts",
    )(block_e, nused, tok3, tok3, hp, w13, w2)


def _combine_kernel(dc_ref, dn_ref, x_ref, hp_ref, w_ref, g2_ref, sw13_ref, sw2_ref, fg_ref, y_hbm,
                    o_ref, buf_ref, sem, *, final):
    i = pl.program_id(0)
    nsteps = pl.num_programs(0)
    slot = i % 2
    tm = x_ref.shape[0]
    half = hp_ref.shape[1]
    sff = sw2_ref.shape[0]

    def start(d_ref, s):
        def body(t, carry):
            for k in range(TOP_K):
                r = d_ref[0, 0, t * TOP_K + k]
                pltpu.make_async_copy(y_hbm.at[pl.ds(r, 1)], buf_ref.at[s, k, pl.ds(t, 1)], sem.at[s]).start()
            return carry

        lax.fori_loop(0, tm, body, 0, unroll=2)

    @pl.when(i == 0)
    def _():
        start(dc_ref, 0)

    @pl.when(i + 1 < nsteps)
    def _():
        start(dn_ref, 1 - slot)

    lo, hi = _unpack_halves(hp_ref[...])
    hcat = (jnp.dot(lo.astype(bf16), sw13_ref[:half, :], preferred_element_type=f32)
            + jnp.dot(hi.astype(bf16), sw13_ref[half:, :], preferred_element_type=f32))
    a = jax.nn.silu(hcat[:, :sff]) * hcat[:, sff:]
    shared = jnp.dot(a.astype(bf16), sw2_ref[...], preferred_element_type=f32)

    for k in range(TOP_K):
        pltpu.make_async_copy(y_hbm.at[pl.ds(0, tm)], buf_ref.at[slot, k], sem.at[slot]).wait()
    w = w_ref[...]
    acc_lo = jnp.zeros((tm, half), f32)
    acc_hi = jnp.zeros((tm, half), f32)
    for k in range(TOP_K):
        ylo, yhi = _unpack_halves(buf_ref[slot, k])
        acc_lo = acc_lo + w[:, k:k + 1] * ylo
        acc_hi = acc_hi + w[:, k:k + 1] * yhi
    g2 = g2_ref[0]
    out_lo = x_ref[:, :half] + g2[:, :half] * (shared[:, :half] + acc_lo)
    out_hi = x_ref[:, half:] + g2[:, half:] * (shared[:, half:] + acc_hi)
    if final:
        ms = (jnp.sum(out_lo * out_lo, axis=-1, keepdims=True)
              + jnp.sum(out_hi * out_hi, axis=-1, keepdims=True)) / (2 * half)
        inv = lax.rsqrt(ms + EPS)
        out_lo = out_lo * inv * fg_ref[:, :half]
        out_hi = out_hi * inv * fg_ref[:, half:]
    o_ref[:, :half] = out_lo
    o_ref[:, half:] = out_hi


def _combine(x, hp, wts, dest, g2, sw13, sw2, final_g, ybuf, seq, final):
    n, d = x.shape
    half = d // 2
    tm = min(TILE_COMB, seq)
    nt = n // tm
    tiles_per_seq = seq // tm
    dest3 = dest.reshape(nt, 1, tm * TOP_K)
    return pl.pallas_call(
        functools.partial(_combine_kernel, final=final),
        grid=(nt,),
        in_specs=[
            pl.BlockSpec((1, 1, tm * TOP_K), lambda i: (i, 0, 0), memory_space=pltpu.SMEM),
            pl.BlockSpec((1, 1, tm * TOP_K), lambda i: (jnp.minimum(i + 1, nt - 1), 0, 0),
                         memory_space=pltpu.SMEM),
            pl.BlockSpec((tm, d), lambda i: (i, 0)),
            pl.BlockSpec((tm, half), lambda i: (i, 0)),
            pl.BlockSpec((tm, V7X_SUBLANES), lambda i: (i, 0)),
            pl.BlockSpec((1, 1, d), lambda i: (i // tiles_per_seq, 0, 0)),
            pl.BlockSpec(sw13.shape, lambda i: (0, 0)),
            pl.BlockSpec(sw2.shape, lambda i: (0, 0)),
            pl.BlockSpec((1, d), lambda i: (0, 0)),
            pl.BlockSpec(memory_space=pl.ANY),
        ],
        out_specs=pl.BlockSpec((tm, d), lambda i: (i, 0)),
        out_shape=jax.ShapeDtypeStruct((n, d), f32),
        scratch_shapes=[pltpu.VMEM((2, TOP_K, tm, half), u32), pltpu.SemaphoreType.DMA((2,))],
        compiler_params=_cparams(("arbitrary",)),
        name="combine",
    )(dest3, dest3, x, hp, wts, g2, sw13, sw2, final_g.reshape(1, d), ybuf)


def _moe(x, g, sc, sh, g2, router_w, router_b, w1, w3, w2, sw1, sw3, sw2, final_g, seq, final):
    n, d = x.shape
    ne = router_w.shape[1]
    hp, eidx_t, wts_t, rank_t, cnt = _route(x, g, sc, sh, router_w, router_b, seq)
    counts = cnt[:, 0]
    pcounts = (counts + ROW_BLOCK - 1) // ROW_BLOCK * ROW_BLOCK
    pends = jnp.cumsum(pcounts)
    poffs = pends - pcounts
    eidx = eidx_t[:TOP_K]
    dest_t = poffs[eidx] + rank_t[:TOP_K]
    nb = -(-(n * TOP_K) // ROW_BLOCK) + ne
    block_e = jnp.minimum(jnp.searchsorted(pends, jnp.arange(nb, dtype=i32) * ROW_BLOCK, side="right"),
                          ne - 1).astype(i32)
    nused = (pends[-1:] // ROW_BLOCK).astype(i32)
    tok_ids = jnp.broadcast_to(jnp.arange(n, dtype=i32)[None, :], (TOP_K, n))
    row_tok = jnp.zeros((nb * ROW_BLOCK,), i32).at[dest_t.reshape(-1)].set(tok_ids.reshape(-1))
    w13 = jnp.concatenate([w1, w3], axis=-1).astype(bf16)
    ybuf = _experts(hp, block_e, nused, row_tok, w13, w2.astype(bf16))
    sw13 = jnp.concatenate([sw1, sw3], axis=-1).astype(bf16)
    return _combine(x, hp, wts_t.T, dest_t.T, g2, sw13, sw2.astype(bf16), final_g, ybuf, seq, final)


def kernel(x, c, ada_w, ada_b, norm_mix_g, norm_ffn_g, ab_w_in, sgu_ln_g, sgu_ln_b, sgu_w, sgu_b, pool_w,
           pool_scale, ab_w_out, cd_w_in, conv3_w, conv3_b, conv31_w, conv31_b, cd_ln_g, cd_ln_b, cd_w_out,
           router_w, router_b, exp_w1, exp_w3, exp_w2, sh_w1, sh_w3, sh_w2, final_g):
    bsz, seq, d = x.shape
    depth = ada_w.shape[0]
    n = bsz * seq
    xf = x.reshape(n, d)
    mod = _ada(c, ada_w, ada_b)
    for l in range(depth):
        sh1, sc1, g1, sh2, sc2, g2 = [mod[l, :, i * d:(i + 1) * d].reshape(bsz, 1, d) for i in range(6)]
        if l % 2 == 0:
            e = l // 2
            p = _norm_mm(xf, norm_mix_g[l], sc1, sh1, ab_w_in[e].astype(bf16), seq)
            xf = _ab_mix(xf, p, g1, sgu_ln_g[e], sgu_ln_b[e], sgu_w[e], sgu_b[e], pool_w[e], pool_scale[e],
                         ab_w_out[e].astype(bf16), bsz, seq)
        else:
            o = l // 2
            p = _norm_mm(xf, norm_mix_g[l], sc1, sh1, cd_w_in[o].astype(bf16), seq)
            xf = _cd_mix(xf, p, g1, conv3_w[o], conv3_b[o], conv31_w[o], conv31_b[o], cd_ln_g[o], cd_ln_b[o],
                         cd_w_out[o].astype(bf16), bsz, seq)
        xf = _moe(xf, norm_ffn_g[l], sc2, sh2, g2, router_w[l], router_b[l], exp_w1[l], exp_w3[l], exp_w2[l],
                  sh_w1[l], sh_w3[l], sh_w2[l], final_g, seq, final=(l == depth - 1))
    return xf.reshape(bsz, seq, d)
```

```python
import functools

import jax
import jax.numpy as jnp
from jax import lax
from jax.experimental import pallas as pl
from jax.experimental.pallas import tpu as pltpu

f32 = jnp.float32
bf16 = jnp.bfloat16
u32 = jnp.uint32
i32 = jnp.int32

EPS = 1e-6
CHUNK = 64
SGU_BLOCK = 128
A_HEADS = 8
POOL_WINDOWS = (2, 4, 8, 16)
SHORT_CONV = 3
CONF_CONV = 31
N_EXPERTS = 64
TOP_K = 6
N_GROUPS = 8
TOPK_GROUPS = 4
ROUTED_SCALE = 2.5

V7X_LANES = 128
V7X_SUBLANES = 8
V7X_VMEM_LIMIT = 56 * 1024 * 1024

ROW_BLOCK = 512
TILE_MM = 1024
TILE_MIX = 512
TILE_ROUTE = 512
TILE_DISPATCH = 1024
TILE_COMB = 256
CONV_ROWS = 64
HI_MASK = 0xFFFF0000


def _cparams(sem):
    return pltpu.CompilerParams(dimension_semantics=sem, vmem_limit_bytes=V7X_VMEM_LIMIT)


def _pack_halves(v):
    w = v.shape[1] // 2
    bits = pltpu.bitcast(v, u32)
    return (bits[:, :w] >> 16) | bits[:, w:]


def _unpack_halves(words):
    lo = pltpu.bitcast(words << 16, f32)
    hi = pltpu.bitcast(words & jnp.uint32(HI_MASK), f32)
    return lo, hi


def _rows_to_tiles(m):
    s = m.shape[1] // V7X_LANES
    st = jnp.stack([m[:, i * V7X_LANES:(i + 1) * V7X_LANES] for i in range(s)], axis=0)
    return pltpu.einshape("srl->rsl", st)


def _tiles_to_rows(t3):
    xt = pltpu.einshape("rsl->srl", t3)
    return jnp.concatenate([xt[i] for i in range(t3.shape[1])], axis=1)


def _modulated_rmsnorm(x, g, sc, sh):
    ms = jnp.mean(x * x, axis=-1, keepdims=True)
    y = x * lax.rsqrt(ms + EPS) * g
    return y * (1.0 + sc) + sh


def _ada_kernel(c_ref, w_ref, b_ref, o_ref):
    ca = jax.nn.silu(c_ref[...]).astype(bf16)
    o_ref[0] = jnp.dot(ca, w_ref[0].astype(bf16), preferred_element_type=f32) + b_ref[0]


def _ada(c, ada_w, ada_b):
    depth, d, n6 = ada_w.shape
    bsz = c.shape[0]
    tn = 1024
    return pl.pallas_call(
        _ada_kernel,
        grid=(depth, n6 // tn),
        in_specs=[
            pl.BlockSpec((bsz, d), lambda l, j: (0, 0)),
            pl.BlockSpec((1, d, tn), lambda l, j: (l, 0, j)),
            pl.BlockSpec((1, 1, tn), lambda l, j: (l, 0, j)),
        ],
        out_specs=pl.BlockSpec((1, bsz, tn), lambda l, j: (l, 0, j)),
        out_shape=jax.ShapeDtypeStruct((depth, bsz, n6), f32),
        compiler_params=_cparams(("parallel", "parallel")),
        name="ada_mod",
    )(c, ada_w, ada_b.reshape(depth, 1, n6))


def _norm_mm_kernel(x_ref, g_ref, sc_ref, sh_ref, w_ref, o_ref, h_ref):
    @pl.when(pl.program_id(1) == 0)
    def _():
        h_ref[...] = _modulated_rmsnorm(x_ref[...], g_ref[...], sc_ref[0], sh_ref[0]).astype(bf16)

    o_ref[...] = jnp.dot(h_ref[...], w_ref[...], preferred_element_type=f32).astype(o_ref.dtype)


def _norm_mm(x, g, sc, sh, w, seq):
    n, d = x.shape
    nout = w.shape[1]
    tm = min(TILE_MM, seq)
    tn = 1024
    tiles_per_seq = seq // tm
    return pl.pallas_call(
        _norm_mm_kernel,
        grid=(n // tm, nout // tn),
        in_specs=[
            pl.BlockSpec((tm, d), lambda i, j: (i, 0)),
            pl.BlockSpec((1, d), lambda i, j: (0, 0)),
            pl.BlockSpec((1, 1, d), lambda i, j: (i // tiles_per_seq, 0, 0)),
            pl.BlockSpec((1, 1, d), lambda i, j: (i // tiles_per_seq, 0, 0)),
            pl.BlockSpec((d, tn), lambda i, j: (0, j)),
        ],
        out_specs=pl.BlockSpec((tm, tn), lambda i, j: (i, j)),
        out_shape=jax.ShapeDtypeStruct((n, nout), bf16),
        scratch_shapes=[pltpu.VMEM((tm, d), bf16)],
        compiler_params=_cparams(("parallel", "arbitrary")),
        name="norm_in_proj",
    )(x, g.reshape(1, d), sc, sh, w)


def _ab_mix_kernel(x_ref, p_ref, g1_ref, lng_ref, lnb_ref, sw_ref, sbt_ref, pw_ref, ps_ref, wo_ref,
                   o_ref, ext_ref, cat_ref):
    s_idx = pl.program_id(1)
    ts = x_ref.shape[0]
    wa = lng_ref.shape[1]
    hd = wa // A_HEADS
    gb = pw_ref.shape[1]
    blk = SGU_BLOCK

    @pl.when(s_idx == 0)
    def _():
        ext_ref[0:blk, :] = jnp.zeros((blk, ext_ref.shape[1]), bf16)

    @pl.when(s_idx > 0)
    def _():
        ext_ref[0:blk, :] = ext_ref[ts:ts + blk, :]

    ext_ref[blk:, :] = p_ref[:, 2 * wa:]

    ci = lax.broadcasted_iota(i32, (blk, blk), 0) // CHUNK
    cj = lax.broadcasted_iota(i32, (blk, blk), 1) // CHUNK
    causal = ci >= cj
    trow = lax.broadcasted_iota(i32, (blk, 2 * blk), 0) + blk
    tcol = lax.broadcasted_iota(i32, (blk, 2 * blk), 1)
    tpos = lax.broadcasted_iota(i32, (blk, 1), 0)

    def block_body(bi, carry):
        r0 = pl.multiple_of(bi * blk, blk)
        rows = pl.ds(r0, blk)
        v = p_ref[rows, wa:2 * wa].astype(f32)
        mu = jnp.mean(v, axis=-1, keepdims=True)
        vc = v - mu
        var = jnp.mean(vc * vc, axis=-1, keepdims=True)
        vn = (vc * lax.rsqrt(var + EPS) * lng_ref[...] + lnb_ref[...]).astype(bf16)
        for h in range(A_HEADS):
            cols = slice(h * hd, (h + 1) * hd)
            wm = jnp.where(causal, sw_ref[h], 0.0).astype(bf16)
            s = jnp.dot(wm, vn[:, cols], preferred_element_type=f32) + sbt_ref[:, h:h + 1]
            u = p_ref[rows, cols].astype(f32)
            cat_ref[rows, cols] = (u * s).astype(bf16)
        win_rows = pl.ds(r0, 2 * blk)
        pos = (s_idx * ts + r0 + tpos + 1).astype(f32)
        for gi, win in enumerate(POOL_WINDOWS):
            cols = slice(gi * gb, (gi + 1) * gb)
            band = jnp.where((tcol <= trow) & (tcol > trow - win), 1.0, 0.0).astype(bf16)
            window = ext_ref[win_rows, cols]
            sums = jnp.dot(band, window, preferred_element_type=f32)
            xb = window[blk:, :].astype(f32)
            pooled = sums / jnp.minimum(pos, float(win)) - xb
            yb = jnp.dot(pooled.astype(bf16), pw_ref[gi], preferred_element_type=f32) * ps_ref[:, cols]
            cat_ref[rows, wa + gi * gb:wa + (gi + 1) * gb] = yb.astype(bf16)
        return carry

    lax.fori_loop(0, ts // blk, block_body, 0)
    mix = jnp.dot(cat_ref[...], wo_ref[...], preferred_element_type=f32)
    o_ref[...] = x_ref[...] + g1_ref[0] * mix


def _ab_mix(x, p, g1, ln_g, ln_b, sgu_w, sgu_b, pool_w, pool_scale, w_out, bsz, seq):
    n, d = x.shape
    wa = ln_g.shape[0]
    wb = pool_scale.shape[0]
    ts = min(TILE_MIX, seq)
    nst = seq // ts
    const2 = lambda b, s: (0, 0)
    const3 = lambda b, s: (0, 0, 0)
    return pl.pallas_call(
        _ab_mix_kernel,
        grid=(bsz, nst),
        in_specs=[
            pl.BlockSpec((ts, d), lambda b, s: (b * nst + s, 0)),
            pl.BlockSpec((ts, p.shape[1]), lambda b, s: (b * nst + s, 0)),
            pl.BlockSpec((1, 1, d), lambda b, s: (b, 0, 0)),
            pl.BlockSpec((1, wa), const2),
            pl.BlockSpec((1, wa), const2),
            pl.BlockSpec(sgu_w.shape, const3),
            pl.BlockSpec((SGU_BLOCK, A_HEADS), const2),
            pl.BlockSpec(pool_w.shape, const3),
            pl.BlockSpec((1, wb), const2),
            pl.BlockSpec(w_out.shape, const2),
        ],
        out_specs=pl.BlockSpec((ts, d), lambda b, s: (b * nst + s, 0)),
        out_shape=jax.ShapeDtypeStruct((n, d), f32),
        scratch_shapes=[pltpu.VMEM((ts + SGU_BLOCK, wb), bf16), pltpu.VMEM((ts, wa + wb), bf16)],
        compiler_params=_cparams(("parallel", "arbitrary")),
        name="mixer_ab",
    )(x, p, g1, ln_g.reshape(1, wa), ln_b.reshape(1, wa), sgu_w, sgu_b.T, pool_w.astype(bf16),
      pool_scale.reshape(1, wb), w_out)


def _cd_mix_kernel(x_ref, p_ref, g1_ref, c3w_ref, c3b_ref, c31w_ref, c31b_ref, lng_ref, lnb_ref, wo_ref,
                   o_ref, e3_ref, e31_ref, cat_ref):
    s_idx = pl.program_id(1)
    ts = x_ref.shape[0]
    wc = c3b_ref.shape[1]
    wd = c31b_ref.shape[1]
    h3 = e3_ref.shape[0] - ts
    h31 = e31_ref.shape[0] - ts

    @pl.when(s_idx == 0)
    def _():
        e3_ref[0:h3, :] = jnp.zeros((h3, wc), f32)
        e31_ref[0:h31, :] = jnp.zeros((h31, wd), f32)

    @pl.when(s_idx > 0)
    def _():
        e3_ref[0:h3, :] = e3_ref[ts:ts + h3, :]
        e31_ref[0:h31, :] = e31_ref[ts:ts + h31, :]

    cg = p_ref[:, wc:2 * wc].astype(f32)
    xc = p_ref[:, 2 * wc:3 * wc].astype(f32)
    e3_ref[h3:, :] = cg * xc
    da = p_ref[:, 3 * wc:3 * wc + wd].astype(f32)
    dg = p_ref[:, 3 * wc + wd:].astype(f32)
    e31_ref[h31:, :] = da * jax.nn.sigmoid(dg)

    cw = 256
    rr = CONV_ROWS
    sub = V7X_SUBLANES

    def causal_conv(w_ref, b_ref, e_ref, halo, r0, cols):
        ntaps = w_ref.shape[0]
        acc = jnp.broadcast_to(b_ref[:, cols], (rr, cw))
        for b in range(sub):
            taps = [k for k in range(ntaps) if (halo - (ntaps - 1) + k) % sub == b]
            if not taps:
                continue
            nload = rr + sub if b else rr
            z = None
            for k in taps:
                a8 = (halo - (ntaps - 1) + k) // sub * sub
                t = w_ref[k:k + 1, cols] * e_ref[pl.ds(r0 + a8, nload), cols]
                z = t if z is None else z + t
            if b:
                z = pltpu.roll(z, nload - b, 0)[:rr]
            acc = acc + z
        return acc

    def rows_body(ri, carry):
        r0 = pl.multiple_of(ri * rr, rr)
        rows = pl.ds(r0, rr)
        for c in range(wc // cw):
            cols = slice(c * cw, (c + 1) * cw)
            acc = causal_conv(c3w_ref, c3b_ref, e3_ref, h3, r0, cols)
            cat_ref[rows, cols] = (p_ref[rows, cols].astype(f32) * acc).astype(bf16)
        dparts = []
        for c in range(wd // cw):
            cols = slice(c * cw, (c + 1) * cw)
            dparts.append(causal_conv(c31w_ref, c31b_ref, e31_ref, h31, r0, cols))
        dsum = dparts[0].sum(axis=-1, keepdims=True)
        for dp in dparts[1:]:
            dsum = dsum + dp.sum(axis=-1, keepdims=True)
        mu = dsum / wd
        vsum = None
        for dp in dparts:
            t = ((dp - mu) * (dp - mu)).sum(axis=-1, keepdims=True)
            vsum = t if vsum is None else vsum + t
        inv = lax.rsqrt(vsum / wd + EPS)
        for c, dp in enumerate(dparts):
            cols = slice(c * cw, (c + 1) * cw)
            dn = (dp - mu) * inv * lng_ref[:, cols] + lnb_ref[:, cols]
            cat_ref[rows, wc + c * cw:wc + (c + 1) * cw] = jax.nn.silu(dn).astype(bf16)
        return carry

    lax.fori_loop(0, ts // rr, rows_body, 0)
    mix = jnp.dot(cat_ref[...], wo_ref[...], preferred_element_type=f32)
    o_ref[...] = x_ref[...] + g1_ref[0] * mix


def _cd_mix(x, p, g1, c3w, c3b, c31w, c31b, ln_g, ln_b, w_out, bsz, seq):
    n, d = x.shape
    wc = c3b.shape[0]
    wd = c31b.shape[0]
    ts = min(TILE_MIX, seq)
    nst = seq // ts
    h3 = V7X_SUBLANES
    h31 = 4 * V7X_SUBLANES
    assert h3 >= SHORT_CONV - 1 and h31 >= CONF_CONV - 1
    const2 = lambda b, s: (0, 0)
    return pl.pallas_call(
        _cd_mix_kernel,
        grid=(bsz, nst),
        in_specs=[
            pl.BlockSpec((ts, d), lambda b, s: (b * nst + s, 0)),
            pl.BlockSpec((ts, p.shape[1]), lambda b, s: (b * nst + s, 0)),
            pl.BlockSpec((1, 1, d), lambda b, s: (b, 0, 0)),
            pl.BlockSpec(c3w.shape, const2),
            pl.BlockSpec((1, wc), const2),
            pl.BlockSpec(c31w.shape, const2),
            pl.BlockSpec((1, wd), const2),
            pl.BlockSpec((1, wd), const2),
            pl.BlockSpec((1, wd), const2),
            pl.BlockSpec(w_out.shape, const2),
        ],
        out_specs=pl.BlockSpec((ts, d), lambda b, s: (b * nst + s, 0)),
        out_shape=jax.ShapeDtypeStruct((n, d), f32),
        scratch_shapes=[pltpu.VMEM((ts + h3, wc), f32), pltpu.VMEM((ts + h31, wd), f32),
                        pltpu.VMEM((ts, wc + wd), bf16)],
        compiler_params=_cparams(("parallel", "arbitrary")),
        name="mixer_cd",
    )(x, p, g1, c3w, c3b.reshape(1, wc), c31w, c31b.reshape(1, wd), ln_g.reshape(1, wd), ln_b.reshape(1, wd),
      w_out)


def _route_kernel(x_ref, g_ref, sc_ref, sh_ref, rwh_ref, rwl_ref, rb_ref,
                  hp_ref, eidx_ref, wts_ref, rank_ref, cnt_ref, carry_ref):
    tm = x_ref.shape[0]
    ne = rwh_ref.shape[0]
    gsz = ne // N_GROUPS
    neg = -jnp.inf

    @pl.when(pl.program_id(0) == 0)
    def _():
        carry_ref[...] = jnp.zeros_like(carry_ref)

    h = _modulated_rmsnorm(x_ref[...], g_ref[...], sc_ref[0], sh_ref[0])
    hh = h.astype(bf16)
    hf = hh.astype(f32)
    hl = (h - hf).astype(bf16)
    hp_ref[...] = _rows_to_tiles(_pack_halves(hf))

    nt = (((1,), (1,)), ((), ()))
    logits = (lax.dot_general(rwh_ref[...], hh, nt, preferred_element_type=f32)
              + lax.dot_general(rwh_ref[...], hl, nt, preferred_element_type=f32)
              + lax.dot_general(rwl_ref[...], hh, nt, preferred_element_type=f32))
    scores = jax.nn.sigmoid(logits)
    choice = scores + rb_ref[...]

    c3 = choice.reshape(N_GROUPS, gsz, tm)
    sub = lax.broadcasted_iota(i32, c3.shape, 1)
    m1 = jnp.max(c3, axis=1, keepdims=True)
    first = jnp.min(jnp.where(c3 == m1, sub, gsz), axis=1, keepdims=True)
    m2 = jnp.max(jnp.where(sub == first, neg, c3), axis=1, keepdims=True)
    gs = m1 + m2
    keep = []
    for gi in range(N_GROUPS):
        beaten = jnp.zeros((1, tm), i32)
        for gj in range(N_GROUPS):
            if gj == gi:
                continue
            wins = (gs[gj] > gs[gi]) | ((gs[gj] == gs[gi]) & (gj < gi))
            beaten = beaten + wins.astype(i32)
        keep.append(beaten < TOPK_GROUPS)
    masked = jnp.concatenate(
        [jnp.where(keep[gi], c3[gi], neg) for gi in range(N_GROUPS)], axis=0)

    eio = lax.broadcasted_iota(i32, (ne, tm), 0)
    cur = masked
    onehot = jnp.zeros((ne, tm), f32)
    idxs, sels, svals = [], [], []
    for _ in range(TOP_K):
        m = jnp.max(cur, axis=0, keepdims=True)
        idx = jnp.min(jnp.where(cur == m, eio, ne), axis=0, keepdims=True)
        sel = eio == idx
        svals.append(jnp.sum(jnp.where(sel, scores, 0.0), axis=0, keepdims=True))
        cur = jnp.where(sel, neg, cur)
        onehot = onehot + sel.astype(f32)
        idxs.append(idx)
        sels.append(sel)
    ssum = svals[0]
    for sv in svals[1:]:
        ssum = ssum + sv

    tr = lax.broadcasted_iota(i32, (tm, tm), 0)
    tc = lax.broadcasted_iota(i32, (tm, tm), 1)
    upper = jnp.where(tr < tc, 1.0, 0.0).astype(bf16)
    before = jnp.dot(onehot.astype(bf16), upper, preferred_element_type=f32) + carry_ref[...]
    new_carry = carry_ref[...] + jnp.sum(onehot, axis=1, keepdims=True)
    carry_ref[...] = new_carry
    cnt_ref[...] = jnp.broadcast_to(new_carry, cnt_ref.shape).astype(i32)

    zrow_i = jnp.zeros((1, tm), i32)
    zrow_f = jnp.zeros((1, tm), f32)
    ranks = [jnp.sum(jnp.where(sel, before, 0.0), axis=0, keepdims=True).astype(i32) for sel in sels]
    pad = V7X_SUBLANES - TOP_K
    eidx_ref[...] = jnp.concatenate(idxs + [zrow_i] * pad, axis=0)
    wts_ref[...] = jnp.concatenate([sv / ssum * ROUTED_SCALE for sv in svals] + [zrow_f] * pad, axis=0)
    rank_ref[...] = jnp.concatenate(ranks + [zrow_i] * pad, axis=0)


def _route(x, g, sc, sh, router_w, router_b, seq):
    n, d = x.shape
    ne = router_w.shape[1]
    tm = min(TILE_ROUTE, seq)
    tiles_per_seq = seq // tm
    rwt = router_w.T
    rwh = rwt.astype(bf16)
    rwl = (rwt - rwh.astype(f32)).astype(bf16)
    slots = V7X_SUBLANES
    return pl.pallas_call(
        _route_kernel,
        grid=(n // tm,),
        in_specs=[
            pl.BlockSpec((tm, d), lambda i: (i, 0)),
            pl.BlockSpec((1, d), lambda i: (0, 0)),
            pl.BlockSpec((1, 1, d), lambda i: (i // tiles_per_seq, 0, 0)),
            pl.BlockSpec((1, 1, d), lambda i: (i // tiles_per_seq, 0, 0)),
            pl.BlockSpec((ne, d), lambda i: (0, 0)),
            pl.BlockSpec((ne, d), lambda i: (0, 0)),
            pl.BlockSpec((ne, 1), lambda i: (0, 0)),
        ],
        out_specs=[
            pl.BlockSpec((tm, d // 2 // V7X_LANES, V7X_LANES), lambda i: (i, 0, 0)),
            pl.BlockSpec((slots, tm), lambda i: (0, i)),
            pl.BlockSpec((slots, tm), lambda i: (0, i)),
            pl.BlockSpec((slots, tm), lambda i: (0, i)),
            pl.BlockSpec((ne, V7X_LANES), lambda i: (0, 0)),
        ],
        out_shape=[
            jax.ShapeDtypeStruct((n, d // 2 // V7X_LANES, V7X_LANES), u32),
            jax.ShapeDtypeStruct((slots, n), i32),
            jax.ShapeDtypeStruct((slots, n), f32),
            jax.ShapeDtypeStruct((slots, n), i32),
            jax.ShapeDtypeStruct((ne, V7X_LANES), i32),
        ],
        scratch_shapes=[pltpu.VMEM((ne, 1), f32)],
        compiler_params=_cparams(("arbitrary",)),
        name="route",
    )(x, g.reshape(1, d), sc, sh, rwh, rwl, router_b.reshape(ne, 1))


def _dest_kernel(poffs_ref, eidx_ref, rank_ref, dest_ref):
    eidx = eidx_ref[...]
    base = jnp.zeros(eidx.shape, i32)
    for e in range(N_EXPERTS):
        base = jnp.where(eidx == e, poffs_ref[e], base)
    dest_ref[...] = base + rank_ref[...]


def _dest(poffs, eidx_t, rank_t):
    slots, n = eidx_t.shape
    td = min(4096, n)
    grid_spec = pltpu.PrefetchScalarGridSpec(
        num_scalar_prefetch=1,
        grid=(n // td,),
        in_specs=[pl.BlockSpec((slots, td), lambda i, po: (0, i)),
                  pl.BlockSpec((slots, td), lambda i, po: (0, i))],
        out_specs=pl.BlockSpec((slots, td), lambda i, po: (0, i)),
    )
    return pl.pallas_call(
        _dest_kernel,
        grid_spec=grid_spec,
        out_shape=jax.ShapeDtypeStruct((slots, n), i32),
        compiler_params=_cparams(("parallel",)),
        name="dest_rows",
    )(poffs, eidx_t, rank_t)


def _dispatch_kernel(nu_ref, zb_ref, dest_ref, hp_ref, xb_hbm, zero_ref, zsem, sem):
    i = pl.program_id(0)
    tm = hp_ref.shape[0]
    rows = zero_ref.shape[0]
    nz = zb_ref.shape[0]

    @pl.when(i == 0)
    def _():
        zero_ref[...] = jnp.zeros_like(zero_ref)

        def zcopy(j):
            return pltpu.make_async_copy(zero_ref, xb_hbm.at[pl.ds(zb_ref[j] * rows, rows)], zsem)

        def zstart(j, carry):
            @pl.when(zb_ref[j] >= 0)
            def _():
                zcopy(j).start()
            return carry

        def zwait(j, carry):
            @pl.when(zb_ref[j] >= 0)
            def _():
                zcopy(j).wait()
            return carry

        lax.fori_loop(0, nz, zstart, 0)
        lax.fori_loop(0, nz, zwait, 0)

    def body(t, carry):
        for k in range(TOP_K):
            pltpu.make_async_copy(hp_ref.at[t], xb_hbm.at[dest_ref[k, t]], sem).start()
        return carry

    lax.fori_loop(0, tm, body, 0, unroll=2)
    for k in range(TOP_K):
        pltpu.make_async_copy(hp_ref, xb_hbm.at[pl.ds(0, tm)], sem).wait()


def _dispatch(hp, dest_t, nused, zero_blocks, nb):
    n, s, lanes = hp.shape
    slots = dest_t.shape[0]
    tm = min(TILE_DISPATCH, n)
    grid_spec = pltpu.PrefetchScalarGridSpec(
        num_scalar_prefetch=2,
        grid=(n // tm,),
        in_specs=[
            pl.BlockSpec((slots, tm), lambda i, nu, zb: (0, i), memory_space=pltpu.SMEM),
            pl.BlockSpec((tm, s, lanes), lambda i, nu, zb: (i, 0, 0)),
        ],
        out_specs=pl.BlockSpec(memory_space=pl.ANY),
        scratch_shapes=[pltpu.VMEM((ROW_BLOCK, s, lanes), u32), pltpu.SemaphoreType.DMA(()),
                        pltpu.SemaphoreType.DMA(())],
    )
    return pl.pallas_call(
        _dispatch_kernel,
        grid_spec=grid_spec,
        out_shape=jax.ShapeDtypeStruct((nb * ROW_BLOCK, s, lanes), u32),
        compiler_params=_cparams(("arbitrary",)),
        name="dispatch",
    )(nused, zero_blocks, dest_t, hp)


def _expert_kernel(be_ref, nu_ref, x_ref, w13_ref, w2_ref, y_ref):
    b = pl.program_id(0)
    nused = nu_ref[0]
    half = x_ref.shape[1] * x_ref.shape[2]
    ff = w2_ref.shape[1]

    @pl.when(b < nused)
    def _():
        lo, hi = _unpack_halves(_tiles_to_rows(x_ref[...]))
        hcat = (jnp.dot(lo.astype(bf16), w13_ref[0, :half, :], preferred_element_type=f32)
                + jnp.dot(hi.astype(bf16), w13_ref[0, half:, :], preferred_element_type=f32))
        a = jax.nn.silu(hcat[:, :ff]) * hcat[:, ff:]
        y = jnp.dot(a.astype(bf16), w2_ref[0], preferred_element_type=f32)
        y_ref[...] = _rows_to_tiles(_pack_halves(y.astype(bf16).astype(f32)))

    @pl.when(b >= nused)
    def _():
        y_ref[...] = jnp.zeros_like(y_ref)


def _experts(xbuf, block_e, nused, w13, w2):
    nrows, s, lanes = xbuf.shape
    nb = block_e.shape[0]
    ne, d, ff2 = w13.shape
    ff = w2.shape[1]
    rows = ROW_BLOCK
    grid_spec = pltpu.PrefetchScalarGridSpec(
        num_scalar_prefetch=2,
        grid=(nb,),
        in_specs=[
            pl.BlockSpec((rows, s, lanes), lambda b, be, nu: (jnp.minimum(b, nu[0] - 1), 0, 0)),
            pl.BlockSpec((1, d, ff2), lambda b, be, nu: (be[b], 0, 0)),
            pl.BlockSpec((1, ff, d), lambda b, be, nu: (be[b], 0, 0)),
        ],
        out_specs=pl.BlockSpec((rows, s, lanes), lambda b, be, nu: (b, 0, 0)),
    )
    return pl.pallas_call(
        _expert_kernel,
        grid_spec=grid_spec,
        out_shape=jax.ShapeDtypeStruct((nrows, s, lanes), u32),
        compiler_params=_cparams(("arbitrary",)),
        name="experts",
    )(block_e, nused, xbuf, w13, w2)


def _combine_kernel(dc_ref, dn_ref, x_ref, hp_ref, w_ref, g2_ref, sw13_ref, sw2_ref, fg_ref, y_hbm,
                    o_ref, buf_ref, sem, *, final):
    i = pl.program_id(0)
    nsteps = pl.num_programs(0)
    slot = i % 2
    tm = x_ref.shape[0]
    half = hp_ref.shape[1] * hp_ref.shape[2]
    sff = sw2_ref.shape[0]

    def start(d_ref, s):
        def body(t, carry):
            for k in range(TOP_K):
                pltpu.make_async_copy(y_hbm.at[d_ref[k, t]], buf_ref.at[s, k, t], sem.at[s]).start()
            return carry

        lax.fori_loop(0, tm, body, 0, unroll=2)

    @pl.when(i == 0)
    def _():
        start(dc_ref, 0)

    @pl.when(i + 1 < nsteps)
    def _():
        start(dn_ref, 1 - slot)

    lo, hi = _unpack_halves(_tiles_to_rows(hp_ref[...]))
    hcat = (jnp.dot(lo.astype(bf16), sw13_ref[:half, :], preferred_element_type=f32)
            + jnp.dot(hi.astype(bf16), sw13_ref[half:, :], preferred_element_type=f32))
    a = jax.nn.silu(hcat[:, :sff]) * hcat[:, sff:]
    shared = jnp.dot(a.astype(bf16), sw2_ref[...], preferred_element_type=f32)

    for k in range(TOP_K):
        pltpu.make_async_copy(y_hbm.at[pl.ds(0, tm)], buf_ref.at[slot, k], sem.at[slot]).wait()
    w = w_ref[...]
    acc_lo = jnp.zeros((tm, half), f32)
    acc_hi = jnp.zeros((tm, half), f32)
    for k in range(TOP_K):
        ylo, yhi = _unpack_halves(_tiles_to_rows(buf_ref[slot, k]))
        acc_lo = acc_lo + w[:, k:k + 1] * ylo
        acc_hi = acc_hi + w[:, k:k + 1] * yhi
    g2 = g2_ref[0]
    out_lo = x_ref[:, :half] + g2[:, :half] * (shared[:, :half] + acc_lo)
    out_hi = x_ref[:, half:] + g2[:, half:] * (shared[:, half:] + acc_hi)
    if final:
        ms = (jnp.sum(out_lo * out_lo, axis=-1, keepdims=True)
              + jnp.sum(out_hi * out_hi, axis=-1, keepdims=True)) / (2 * half)
        inv = lax.rsqrt(ms + EPS)
        out_lo = out_lo * inv * fg_ref[:, :half]
        out_hi = out_hi * inv * fg_ref[:, half:]
    o_ref[:, :half] = out_lo
    o_ref[:, half:] = out_hi


def _combine(x, hp, wts, dest_t, g2, sw13, sw2, final_g, ybuf, seq, final):
    n, d = x.shape
    _, s, lanes = hp.shape
    slots = dest_t.shape[0]
    tm = min(TILE_COMB, seq)
    nt = n // tm
    tiles_per_seq = seq // tm
    return pl.pallas_call(
        functools.partial(_combine_kernel, final=final),
        grid=(nt,),
        in_specs=[
            pl.BlockSpec((slots, tm), lambda i: (0, i), memory_space=pltpu.SMEM),
            pl.BlockSpec((slots, tm), lambda i: (0, jnp.minimum(i + 1, nt - 1)), memory_space=pltpu.SMEM),
            pl.BlockSpec((tm, d), lambda i: (i, 0)),
            pl.BlockSpec((tm, s, lanes), lambda i: (i, 0, 0)),
            pl.BlockSpec((tm, V7X_SUBLANES), lambda i: (i, 0)),
            pl.BlockSpec((1, 1, d), lambda i: (i // tiles_per_seq, 0, 0)),
            pl.BlockSpec(sw13.shape, lambda i: (0, 0)),
            pl.BlockSpec(sw2.shape, lambda i: (0, 0)),
            pl.BlockSpec((1, d), lambda i: (0, 0)),
            pl.BlockSpec(memory_space=pl.ANY),
        ],
        out_specs=pl.BlockSpec((tm, d), lambda i: (i, 0)),
        out_shape=jax.ShapeDtypeStruct((n, d), f32),
        scratch_shapes=[pltpu.VMEM((2, TOP_K, tm, s, lanes), u32), pltpu.SemaphoreType.DMA((2,))],
        compiler_params=_cparams(("arbitrary",)),
        name="combine",
    )(dest_t, dest_t, x, hp, wts, g2, sw13, sw2, final_g.reshape(1, d), ybuf)


def _moe(x, g, sc, sh, g2, router_w, router_b, w1, w3, w2, sw1, sw3, sw2, final_g, seq, final):
    n, d = x.shape
    ne = router_w.shape[1]
    hp, eidx_t, wts_t, rank_t, cnt = _route(x, g, sc, sh, router_w, router_b, seq)
    counts = cnt[:, 0]
    pcounts = (counts + ROW_BLOCK - 1) // ROW_BLOCK * ROW_BLOCK
    pends = jnp.cumsum(pcounts)
    poffs = pends - pcounts
    nb = -(-(n * TOP_K) // ROW_BLOCK) + ne
    starts = jnp.arange(nb, dtype=i32) * ROW_BLOCK
    block_e = jnp.minimum(jnp.sum((pends[None, :] <= starts[:, None]).astype(i32), axis=1), ne - 1)
    nused = (pends[-1:] // ROW_BLOCK).astype(i32)
    last_blocks = jnp.where(pcounts > 0, pends // ROW_BLOCK - 1, -1)
    tail = nused[0] + jnp.arange(nb - (n * TOP_K) // ROW_BLOCK, dtype=i32)
    zero_blocks = jnp.concatenate([last_blocks, jnp.where(tail < nb, tail, -1)]).astype(i32)
    dest_t = _dest(poffs.astype(i32), eidx_t, rank_t)
    xbuf = _dispatch(hp, dest_t, nused, zero_blocks, nb)
    w13 = jnp.concatenate([w1, w3], axis=-1).astype(bf16)
    ybuf = _experts(xbuf, block_e, nused, w13, w2.astype(bf16))
    sw13 = jnp.concatenate([sw1, sw3], axis=-1).astype(bf16)
    return _combine(x, hp, wts_t.T, dest_t, g2, sw13, sw2.astype(bf16), final_g, ybuf, seq, final)


def kernel(x, c, ada_w, ada_b, norm_mix_g, norm_ffn_g, ab_w_in, sgu_ln_g, sgu_ln_b, sgu_w, sgu_b, pool_w,
           pool_scale, ab_w_out, cd_w_in, conv3_w, conv3_b, conv31_w, conv31_b, cd_ln_g, cd_ln_b, cd_w_out,
           router_w, router_b, exp_w1, exp_w3, exp_w2, sh_w1, sh_w3, sh_w2, final_g):
    bsz, seq, d = x.shape
    depth = ada_w.shape[0]
    n = bsz * seq
    xf = x.reshape(n, d)
    mod = _ada(c, ada_w, ada_b)
    for l in range(depth):
        sh1, sc1, g1, sh2, sc2, g2 = [mod[l, :, i * d:(i + 1) * d].reshape(bsz, 1, d) for i in range(6)]
        if l % 2 == 0:
            e = l // 2
            p = _norm_mm(xf, norm_mix_g[l], sc1, sh1, ab_w_in[e].astype(bf16), seq)
            xf = _ab_mix(xf, p, g1, sgu_ln_g[e], sgu_ln_b[e], sgu_w[e], sgu_b[e], pool_w[e], pool_scale[e],
                         ab_w_out[e].astype(bf16), bsz, seq)
        else:
            o = l // 2
            p = _norm_mm(xf, norm_mix_g[l], sc1, sh1, cd_w_in[o].astype(bf16), seq)
            xf = _cd_mix(xf, p, g1, conv3_w[o], conv3_b[o], conv31_w[o], conv31_b[o], cd_ln_g[o], cd_ln_b[o],
                         cd_w_out[o].astype(bf16), bsz, seq)
        xf = _moe(xf, norm_ffn_g[l], sc2, sh2, g2, router_w[l], router_b[l], exp_w1[l], exp_w3[l], exp_w2[l],
                  sh_w1[l], sh_w3[l], sh_w2[l], final_g, seq, final=(l == depth - 1))
    return xf.reshape(bsz, seq, d)
```

```python
import functools

import jax
import jax.numpy as jnp
from jax import lax
from jax.experimental import pallas as pl
from jax.experimental.pallas import tpu as pltpu

f32 = jnp.float32
bf16 = jnp.bfloat16
u32 = jnp.uint32
i32 = jnp.int32

EPS = 1e-6
CHUNK = 64
SGU_BLOCK = 128
A_HEADS = 8
POOL_WINDOWS = (2, 4, 8, 16)
SHORT_CONV = 3
CONF_CONV = 31
N_EXPERTS = 64
TOP_K = 6
N_GROUPS = 8
TOPK_GROUPS = 4
ROUTED_SCALE = 2.5

V7X_LANES = 128
V7X_SUBLANES = 8
V7X_VMEM_LIMIT = 56 * 1024 * 1024

ROW_BLOCK = 512
TILE_MM = 1024
TILE_MIX = 512
TILE_ROUTE = 512
EXPERT_CHUNKS = 4
TILE_DISPATCH = 1024
TILE_COMB = 256
CONV_ROWS = 64
HI_MASK = 0xFFFF0000


def _cparams(sem):
    return pltpu.CompilerParams(dimension_semantics=sem, vmem_limit_bytes=V7X_VMEM_LIMIT)


def _pack_halves(v):
    w = v.shape[1] // 2
    bits = pltpu.bitcast(v, u32)
    return (bits[:, :w] >> 16) | bits[:, w:]


def _unpack_halves(words):
    lo = pltpu.bitcast(words << 16, f32)
    hi = pltpu.bitcast(words & jnp.uint32(HI_MASK), f32)
    return lo, hi


def _rows_to_tiles(m):
    s = m.shape[1] // V7X_LANES
    st = jnp.stack([m[:, i * V7X_LANES:(i + 1) * V7X_LANES] for i in range(s)], axis=0)
    return pltpu.einshape("srl->rsl", st)


def _tiles_to_rows(t3):
    xt = pltpu.einshape("rsl->srl", t3)
    return jnp.concatenate([xt[i] for i in range(t3.shape[1])], axis=1)


def _modulated_rmsnorm(x, g, sc, sh):
    ms = jnp.mean(x * x, axis=-1, keepdims=True)
    y = x * lax.rsqrt(ms + EPS) * g
    return y * (1.0 + sc) + sh


def _ada_kernel(c_ref, w_ref, b_ref, o_ref):
    ca = jax.nn.silu(c_ref[...]).astype(bf16)
    o_ref[0] = jnp.dot(ca, w_ref[0].astype(bf16), preferred_element_type=f32) + b_ref[0]


def _ada(c, ada_w, ada_b):
    depth, d, n6 = ada_w.shape
    bsz = c.shape[0]
    tn = 1024
    return pl.pallas_call(
        _ada_kernel,
        grid=(depth, n6 // tn),
        in_specs=[
            pl.BlockSpec((bsz, d), lambda l, j: (0, 0)),
            pl.BlockSpec((1, d, tn), lambda l, j: (l, 0, j)),
            pl.BlockSpec((1, 1, tn), lambda l, j: (l, 0, j)),
        ],
        out_specs=pl.BlockSpec((1, bsz, tn), lambda l, j: (l, 0, j)),
        out_shape=jax.ShapeDtypeStruct((depth, bsz, n6), f32),
        compiler_params=_cparams(("parallel", "parallel")),
        name="ada_mod",
    )(c, ada_w, ada_b.reshape(depth, 1, n6))


def _norm_mm_kernel(x_ref, g_ref, sc_ref, sh_ref, w_ref, o_ref, h_ref):
    @pl.when(pl.program_id(1) == 0)
    def _():
        h_ref[...] = _modulated_rmsnorm(x_ref[...], g_ref[...], sc_ref[0], sh_ref[0]).astype(bf16)

    o_ref[...] = jnp.dot(h_ref[...], w_ref[...], preferred_element_type=f32).astype(o_ref.dtype)


def _norm_mm(x, g, sc, sh, w, seq):
    n, d = x.shape
    nout = w.shape[1]
    tm = min(TILE_MM, seq)
    tn = 1024
    tiles_per_seq = seq // tm
    return pl.pallas_call(
        _norm_mm_kernel,
        grid=(n // tm, nout // tn),
        in_specs=[
            pl.BlockSpec((tm, d), lambda i, j: (i, 0)),
            pl.BlockSpec((1, d), lambda i, j: (0, 0)),
            pl.BlockSpec((1, 1, d), lambda i, j: (i // tiles_per_seq, 0, 0)),
            pl.BlockSpec((1, 1, d), lambda i, j: (i // tiles_per_seq, 0, 0)),
            pl.BlockSpec((d, tn), lambda i, j: (0, j)),
        ],
        out_specs=pl.BlockSpec((tm, tn), lambda i, j: (i, j)),
        out_shape=jax.ShapeDtypeStruct((n, nout), bf16),
        scratch_shapes=[pltpu.VMEM((tm, d), bf16)],
        compiler_params=_cparams(("parallel", "arbitrary")),
        name="norm_in_proj",
    )(x, g.reshape(1, d), sc, sh, w)


def _ab_mix_kernel(x_ref, p_ref, g1_ref, lng_ref, lnb_ref, sw_ref, sbt_ref, pw_ref, ps_ref, wo_ref,
                   o_ref, ext_ref, cat_ref):
    s_idx = pl.program_id(1)
    ts = x_ref.shape[0]
    wa = lng_ref.shape[1]
    hd = wa // A_HEADS
    gb = pw_ref.shape[1]
    blk = SGU_BLOCK

    @pl.when(s_idx == 0)
    def _():
        ext_ref[0:blk, :] = jnp.zeros((blk, ext_ref.shape[1]), bf16)

    @pl.when(s_idx > 0)
    def _():
        ext_ref[0:blk, :] = ext_ref[ts:ts + blk, :]

    ext_ref[blk:, :] = p_ref[:, 2 * wa:]

    ci = lax.broadcasted_iota(i32, (blk, blk), 0) // CHUNK
    cj = lax.broadcasted_iota(i32, (blk, blk), 1) // CHUNK
    causal = ci >= cj
    trow = lax.broadcasted_iota(i32, (blk, 2 * blk), 0) + blk
    tcol = lax.broadcasted_iota(i32, (blk, 2 * blk), 1)
    tpos = lax.broadcasted_iota(i32, (blk, 1), 0)

    def block_body(bi, carry):
        r0 = pl.multiple_of(bi * blk, blk)
        rows = pl.ds(r0, blk)
        v = p_ref[rows, wa:2 * wa].astype(f32)
        mu = jnp.mean(v, axis=-1, keepdims=True)
        vc = v - mu
        var = jnp.mean(vc * vc, axis=-1, keepdims=True)
        vn = (vc * lax.rsqrt(var + EPS) * lng_ref[...] + lnb_ref[...]).astype(bf16)
        for h in range(A_HEADS):
            cols = slice(h * hd, (h + 1) * hd)
            wm = jnp.where(causal, sw_ref[h], 0.0).astype(bf16)
            s = jnp.dot(wm, vn[:, cols], preferred_element_type=f32) + sbt_ref[:, h:h + 1]
            u = p_ref[rows, cols].astype(f32)
            cat_ref[rows, cols] = (u * s).astype(bf16)
        win_rows = pl.ds(r0, 2 * blk)
        pos = (s_idx * ts + r0 + tpos + 1).astype(f32)
        for gi, win in enumerate(POOL_WINDOWS):
            cols = slice(gi * gb, (gi + 1) * gb)
            band = jnp.where((tcol <= trow) & (tcol > trow - win), 1.0, 0.0).astype(bf16)
            window = ext_ref[win_rows, cols]
            sums = jnp.dot(band, window, preferred_element_type=f32)
            xb = window[blk:, :].astype(f32)
            pooled = sums / jnp.minimum(pos, float(win)) - xb
            yb = jnp.dot(pooled.astype(bf16), pw_ref[gi], preferred_element_type=f32) * ps_ref[:, cols]
            cat_ref[rows, wa + gi * gb:wa + (gi + 1) * gb] = yb.astype(bf16)
        return carry

    lax.fori_loop(0, ts // blk, block_body, 0)
    mix = jnp.dot(cat_ref[...], wo_ref[...], preferred_element_type=f32)
    o_ref[...] = x_ref[...] + g1_ref[0] * mix


def _ab_mix(x, p, g1, ln_g, ln_b, sgu_w, sgu_b, pool_w, pool_scale, w_out, bsz, seq):
    n, d = x.shape
    wa = ln_g.shape[0]
    wb = pool_scale.shape[0]
    ts = min(TILE_MIX, seq)
    nst = seq // ts
    const2 = lambda b, s: (0, 0)
    const3 = lambda b, s: (0, 0, 0)
    return pl.pallas_call(
        _ab_mix_kernel,
        grid=(bsz, nst),
        in_specs=[
            pl.BlockSpec((ts, d), lambda b, s: (b * nst + s, 0)),
            pl.BlockSpec((ts, p.shape[1]), lambda b, s: (b * nst + s, 0)),
            pl.BlockSpec((1, 1, d), lambda b, s: (b, 0, 0)),
            pl.BlockSpec((1, wa), const2),
            pl.BlockSpec((1, wa), const2),
            pl.BlockSpec(sgu_w.shape, const3),
            pl.BlockSpec((SGU_BLOCK, A_HEADS), const2),
            pl.BlockSpec(pool_w.shape, const3),
            pl.BlockSpec((1, wb), const2),
            pl.BlockSpec(w_out.shape, const2),
        ],
        out_specs=pl.BlockSpec((ts, d), lambda b, s: (b * nst + s, 0)),
        out_shape=jax.ShapeDtypeStruct((n, d), f32),
        scratch_shapes=[pltpu.VMEM((ts + SGU_BLOCK, wb), bf16), pltpu.VMEM((ts, wa + wb), bf16)],
        compiler_params=_cparams(("parallel", "arbitrary")),
        name="mixer_ab",
    )(x, p, g1, ln_g.reshape(1, wa), ln_b.reshape(1, wa), sgu_w, sgu_b.T, pool_w.astype(bf16),
      pool_scale.reshape(1, wb), w_out)


def _cd_mix_kernel(x_ref, p_ref, g1_ref, c3w_ref, c3b_ref, c31w_ref, c31b_ref, lng_ref, lnb_ref, wo_ref,
                   o_ref, e3_ref, e31_ref, cat_ref):
    s_idx = pl.program_id(1)
    ts = x_ref.shape[0]
    wc = c3b_ref.shape[1]
    wd = c31b_ref.shape[1]
    h3 = e3_ref.shape[0] - ts
    h31 = e31_ref.shape[0] - ts

    @pl.when(s_idx == 0)
    def _():
        e3_ref[0:h3, :] = jnp.zeros((h3, wc), f32)
        e31_ref[0:h31, :] = jnp.zeros((h31, wd), f32)

    @pl.when(s_idx > 0)
    def _():
        e3_ref[0:h3, :] = e3_ref[ts:ts + h3, :]
        e31_ref[0:h31, :] = e31_ref[ts:ts + h31, :]

    cg = p_ref[:, wc:2 * wc].astype(f32)
    xc = p_ref[:, 2 * wc:3 * wc].astype(f32)
    e3_ref[h3:, :] = cg * xc
    da = p_ref[:, 3 * wc:3 * wc + wd].astype(f32)
    dg = p_ref[:, 3 * wc + wd:].astype(f32)
    e31_ref[h31:, :] = da * jax.nn.sigmoid(dg)

    cw = 256
    rr = CONV_ROWS
    sub = V7X_SUBLANES

    def causal_conv(w_ref, b_ref, e_ref, halo, r0, cols):
        ntaps = w_ref.shape[0]
        acc = jnp.broadcast_to(b_ref[:, cols], (rr, cw))
        for b in range(sub):
            taps = [k for k in range(ntaps) if (halo - (ntaps - 1) + k) % sub == b]
            if not taps:
                continue
            nload = rr + sub if b else rr
            z = None
            for k in taps:
                a8 = (halo - (ntaps - 1) + k) // sub * sub
                t = w_ref[k:k + 1, cols] * e_ref[pl.ds(r0 + a8, nload), cols]
                z = t if z is None else z + t
            if b:
                z = pltpu.roll(z, nload - b, 0)[:rr]
            acc = acc + z
        return acc

    def rows_body(ri, carry):
        r0 = pl.multiple_of(ri * rr, rr)
        rows = pl.ds(r0, rr)
        for c in range(wc // cw):
            cols = slice(c * cw, (c + 1) * cw)
            acc = causal_conv(c3w_ref, c3b_ref, e3_ref, h3, r0, cols)
            cat_ref[rows, cols] = (p_ref[rows, cols].astype(f32) * acc).astype(bf16)
        dparts = []
        for c in range(wd // cw):
            cols = slice(c * cw, (c + 1) * cw)
            dparts.append(causal_conv(c31w_ref, c31b_ref, e31_ref, h31, r0, cols))
        dsum = dparts[0].sum(axis=-1, keepdims=True)
        for dp in dparts[1:]:
            dsum = dsum + dp.sum(axis=-1, keepdims=True)
        mu = dsum / wd
        vsum = None
        for dp in dparts:
            t = ((dp - mu) * (dp - mu)).sum(axis=-1, keepdims=True)
            vsum = t if vsum is None else vsum + t
        inv = lax.rsqrt(vsum / wd + EPS)
        for c, dp in enumerate(dparts):
            cols = slice(c * cw, (c + 1) * cw)
            dn = (dp - mu) * inv * lng_ref[:, cols] + lnb_ref[:, cols]
            cat_ref[rows, wc + c * cw:wc + (c + 1) * cw] = jax.nn.silu(dn).astype(bf16)
        return carry

    lax.fori_loop(0, ts // rr, rows_body, 0)
    mix = jnp.dot(cat_ref[...], wo_ref[...], preferred_element_type=f32)
    o_ref[...] = x_ref[...] + g1_ref[0] * mix


def _cd_mix(x, p, g1, c3w, c3b, c31w, c31b, ln_g, ln_b, w_out, bsz, seq):
    n, d = x.shape
    wc = c3b.shape[0]
    wd = c31b.shape[0]
    ts = min(TILE_MIX, seq)
    nst = seq // ts
    h3 = V7X_SUBLANES
    h31 = 4 * V7X_SUBLANES
    assert h3 >= SHORT_CONV - 1 and h31 >= CONF_CONV - 1
    const2 = lambda b, s: (0, 0)
    return pl.pallas_call(
        _cd_mix_kernel,
        grid=(bsz, nst),
        in_specs=[
            pl.BlockSpec((ts, d), lambda b, s: (b * nst + s, 0)),
            pl.BlockSpec((ts, p.shape[1]), lambda b, s: (b * nst + s, 0)),
            pl.BlockSpec((1, 1, d), lambda b, s: (b, 0, 0)),
            pl.BlockSpec(c3w.shape, const2),
            pl.BlockSpec((1, wc), const2),
            pl.BlockSpec(c31w.shape, const2),
            pl.BlockSpec((1, wd), const2),
            pl.BlockSpec((1, wd), const2),
            pl.BlockSpec((1, wd), const2),
            pl.BlockSpec(w_out.shape, const2),
        ],
        out_specs=pl.BlockSpec((ts, d), lambda b, s: (b * nst + s, 0)),
        out_shape=jax.ShapeDtypeStruct((n, d), f32),
        scratch_shapes=[pltpu.VMEM((ts + h3, wc), f32), pltpu.VMEM((ts + h31, wd), f32),
                        pltpu.VMEM((ts, wc + wd), bf16)],
        compiler_params=_cparams(("parallel", "arbitrary")),
        name="mixer_cd",
    )(x, p, g1, c3w, c3b.reshape(1, wc), c31w, c31b.reshape(1, wd), ln_g.reshape(1, wd), ln_b.reshape(1, wd),
      w_out)


def _route_kernel(x_ref, g_ref, sc_ref, sh_ref, rwh_ref, rwl_ref, rb_ref,
                  hp_ref, eidx_ref, wts_ref, rank_ref, cnt_ref, carry_ref):
    tm = x_ref.shape[0]
    ne = rwh_ref.shape[0]
    gsz = ne // N_GROUPS
    neg = -jnp.inf

    @pl.when(pl.program_id(0) == 0)
    def _():
        carry_ref[...] = jnp.zeros_like(carry_ref)

    h = _modulated_rmsnorm(x_ref[...], g_ref[...], sc_ref[0], sh_ref[0])
    hh = h.astype(bf16)
    hf = hh.astype(f32)
    hl = (h - hf).astype(bf16)
    hp_ref[...] = _rows_to_tiles(_pack_halves(hf))

    nt = (((1,), (1,)), ((), ()))
    logits = (lax.dot_general(rwh_ref[...], hh, nt, preferred_element_type=f32)
              + lax.dot_general(rwh_ref[...], hl, nt, preferred_element_type=f32)
              + lax.dot_general(rwl_ref[...], hh, nt, preferred_element_type=f32))
    scores = jax.nn.sigmoid(logits)
    choice = scores + rb_ref[...]

    c3 = choice.reshape(N_GROUPS, gsz, tm)
    sub = lax.broadcasted_iota(i32, c3.shape, 1)
    m1 = jnp.max(c3, axis=1, keepdims=True)
    first = jnp.min(jnp.where(c3 == m1, sub, gsz), axis=1, keepdims=True)
    m2 = jnp.max(jnp.where(sub == first, neg, c3), axis=1, keepdims=True)
    gs = m1 + m2
    keep = []
    for gi in range(N_GROUPS):
        beaten = jnp.zeros((1, tm), i32)
        for gj in range(N_GROUPS):
            if gj == gi:
                continue
            wins = (gs[gj] > gs[gi]) | ((gs[gj] == gs[gi]) & (gj < gi))
            beaten = beaten + wins.astype(i32)
        keep.append(beaten < TOPK_GROUPS)
    masked = jnp.concatenate(
        [jnp.where(keep[gi], c3[gi], neg) for gi in range(N_GROUPS)], axis=0)

    eio = lax.broadcasted_iota(i32, (ne, tm), 0)
    cur = masked
    onehot = jnp.zeros((ne, tm), f32)
    idxs, sels, svals = [], [], []
    for _ in range(TOP_K):
        m = jnp.max(cur, axis=0, keepdims=True)
        idx = jnp.min(jnp.where(cur == m, eio, ne), axis=0, keepdims=True)
        sel = eio == idx
        svals.append(jnp.sum(jnp.where(sel, scores, 0.0), axis=0, keepdims=True))
        cur = jnp.where(sel, neg, cur)
        onehot = onehot + sel.astype(f32)
        idxs.append(idx)
        sels.append(sel)
    ssum = svals[0]
    for sv in svals[1:]:
        ssum = ssum + sv

    tr = lax.broadcasted_iota(i32, (tm, tm), 0)
    tc = lax.broadcasted_iota(i32, (tm, tm), 1)
    upper = jnp.where(tr < tc, 1.0, 0.0).astype(bf16)
    before = jnp.dot(onehot.astype(bf16), upper, preferred_element_type=f32) + carry_ref[...]
    new_carry = carry_ref[...] + jnp.sum(onehot, axis=1, keepdims=True)
    carry_ref[...] = new_carry
    cnt_ref[...] = jnp.broadcast_to(new_carry, cnt_ref.shape).astype(i32)

    zrow_i = jnp.zeros((1, tm), i32)
    zrow_f = jnp.zeros((1, tm), f32)
    ranks = [jnp.sum(jnp.where(sel, before, 0.0), axis=0, keepdims=True).astype(i32) for sel in sels]
    pad = V7X_SUBLANES - TOP_K
    eidx_ref[...] = jnp.concatenate(idxs + [zrow_i] * pad, axis=0)
    wts_ref[...] = jnp.concatenate([sv / ssum * ROUTED_SCALE for sv in svals] + [zrow_f] * pad, axis=0)
    rank_ref[...] = jnp.concatenate(ranks + [zrow_i] * pad, axis=0)


def _route(x, g, sc, sh, router_w, router_b, seq):
    n, d = x.shape
    ne = router_w.shape[1]
    tm = min(TILE_ROUTE, seq)
    tiles_per_seq = seq // tm
    rwt = router_w.T
    rwh = rwt.astype(bf16)
    rwl = (rwt - rwh.astype(f32)).astype(bf16)
    slots = V7X_SUBLANES
    return pl.pallas_call(
        _route_kernel,
        grid=(n // tm,),
        in_specs=[
            pl.BlockSpec((tm, d), lambda i: (i, 0)),
            pl.BlockSpec((1, d), lambda i: (0, 0)),
            pl.BlockSpec((1, 1, d), lambda i: (i // tiles_per_seq, 0, 0)),
            pl.BlockSpec((1, 1, d), lambda i: (i // tiles_per_seq, 0, 0)),
            pl.BlockSpec((ne, d), lambda i: (0, 0)),
            pl.BlockSpec((ne, d), lambda i: (0, 0)),
            pl.BlockSpec((ne, 1), lambda i: (0, 0)),
        ],
        out_specs=[
            pl.BlockSpec((tm, d // 2 // V7X_LANES, V7X_LANES), lambda i: (i, 0, 0)),
            pl.BlockSpec((slots, tm), lambda i: (0, i)),
            pl.BlockSpec((slots, tm), lambda i: (0, i)),
            pl.BlockSpec((slots, tm), lambda i: (0, i)),
            pl.BlockSpec((ne, V7X_LANES), lambda i: (0, 0)),
        ],
        out_shape=[
            jax.ShapeDtypeStruct((n, d // 2 // V7X_LANES, V7X_LANES), u32),
            jax.ShapeDtypeStruct((slots, n), i32),
            jax.ShapeDtypeStruct((slots, n), f32),
            jax.ShapeDtypeStruct((slots, n), i32),
            jax.ShapeDtypeStruct((ne, V7X_LANES), i32),
        ],
        scratch_shapes=[pltpu.VMEM((ne, 1), f32)],
        compiler_params=_cparams(("arbitrary",)),
        name="route",
    )(x, g.reshape(1, d), sc, sh, rwh, rwl, router_b.reshape(ne, 1))


def _dest_kernel(poffs_ref, eidx_ref, rank_ref, dest_ref):
    eidx = eidx_ref[...]
    base = jnp.zeros(eidx.shape, i32)
    for e in range(N_EXPERTS):
        base = jnp.where(eidx == e, poffs_ref[e], base)
    dest_ref[...] = base + rank_ref[...]


def _dest(poffs, eidx_t, rank_t):
    slots, n = eidx_t.shape
    td = min(4096, n)
    grid_spec = pltpu.PrefetchScalarGridSpec(
        num_scalar_prefetch=1,
        grid=(n // td,),
        in_specs=[pl.BlockSpec((slots, td), lambda i, po: (0, i)),
                  pl.BlockSpec((slots, td), lambda i, po: (0, i))],
        out_specs=pl.BlockSpec((slots, td), lambda i, po: (0, i)),
    )
    return pl.pallas_call(
        _dest_kernel,
        grid_spec=grid_spec,
        out_shape=jax.ShapeDtypeStruct((slots, n), i32),
        compiler_params=_cparams(("parallel",)),
        name="dest_rows",
    )(poffs, eidx_t, rank_t)


def _dispatch_kernel(nu_ref, zb_ref, dest_ref, hp_ref, xb_hbm, zero_ref, zsem, sem):
    i = pl.program_id(0)
    tm = hp_ref.shape[0]
    rows = zero_ref.shape[0]
    nz = zb_ref.shape[0]

    @pl.when(i == 0)
    def _():
        zero_ref[...] = jnp.zeros_like(zero_ref)

        def zcopy(j):
            return pltpu.make_async_copy(zero_ref, xb_hbm.at[pl.ds(zb_ref[j] * rows, rows)], zsem)

        def zstart(j, carry):
            @pl.when(zb_ref[j] >= 0)
            def _():
                zcopy(j).start()
            return carry

        def zwait(j, carry):
            @pl.when(zb_ref[j] >= 0)
            def _():
                zcopy(j).wait()
            return carry

        lax.fori_loop(0, nz, zstart, 0)
        lax.fori_loop(0, nz, zwait, 0)

    def body(t, carry):
        for k in range(TOP_K):
            pltpu.make_async_copy(hp_ref.at[t], xb_hbm.at[dest_ref[0, 0, t * TOP_K + k]],
                                  sem).start(priority=k % 2)
        return carry

    lax.fori_loop(0, tm, body, 0, unroll=2)
    for k in range(TOP_K):
        pltpu.make_async_copy(hp_ref, xb_hbm.at[pl.ds(0, tm)], sem).wait()


def _token_major(a_t, tm):
    n = a_t.shape[1]
    return a_t[:TOP_K].T.reshape(n // tm, 1, tm * TOP_K)


def _dispatch(hp, dest_t, nused, zero_blocks, nb):
    n, s, lanes = hp.shape
    tm = min(TILE_DISPATCH, n)
    grid_spec = pltpu.PrefetchScalarGridSpec(
        num_scalar_prefetch=2,
        grid=(n // tm,),
        in_specs=[
            pl.BlockSpec((1, 1, tm * TOP_K), lambda i, nu, zb: (i, 0, 0), memory_space=pltpu.SMEM),
            pl.BlockSpec((tm, s, lanes), lambda i, nu, zb: (i, 0, 0)),
        ],
        out_specs=pl.BlockSpec(memory_space=pl.ANY),
        scratch_shapes=[pltpu.VMEM((ROW_BLOCK, s, lanes), u32), pltpu.SemaphoreType.DMA(()),
                        pltpu.SemaphoreType.DMA(())],
    )
    return pl.pallas_call(
        _dispatch_kernel,
        grid_spec=grid_spec,
        out_shape=jax.ShapeDtypeStruct((nb * ROW_BLOCK, s, lanes), u32),
        compiler_params=_cparams(("arbitrary",)),
        name="dispatch",
    )(nused, zero_blocks, _token_major(dest_t, tm), hp)


def _expert_kernel(be_ref, nu_ref, x_ref, w1_ref, w3_ref, w2_ref, y_ref, w13_s, w2_s):
    b = pl.program_id(0)
    nused = nu_ref[0]
    ff = w2_s.shape[0]
    rows = x_ref.shape[0]

    @pl.when((b == 0) | (be_ref[b] != be_ref[jnp.maximum(b - 1, 0)]))
    def _():
        w13_s[:, :ff] = w1_ref[0, 0].astype(bf16)
        w13_s[:, ff:] = w3_ref[0, 0].astype(bf16)
        w2_s[...] = w2_ref[0, 0].astype(bf16)

    @pl.when(b < nused)
    def _():
        cr = rows // EXPERT_CHUNKS
        for c in range(EXPERT_CHUNKS):
            rs = slice(c * cr, (c + 1) * cr)
            lo, hi = _unpack_halves(_tiles_to_rows(x_ref[rs]))
            x = jnp.concatenate([lo, hi], axis=1).astype(bf16)
            hcat = jnp.dot(x, w13_s[...], preferred_element_type=f32)
            a = jax.nn.silu(hcat[:, :ff]) * hcat[:, ff:]
            y = jnp.dot(a.astype(bf16), w2_s[...], preferred_element_type=f32)
            y_ref[rs] = _rows_to_tiles(_pack_halves(y.astype(bf16).astype(f32)))

    @pl.when(b >= nused)
    def _():
        y_ref[...] = jnp.zeros_like(y_ref)


def _experts(xbuf, block_e, nused, w1, w3, w2, layer):
    nrows, s, lanes = xbuf.shape
    nb = block_e.shape[0]
    _, ne, d, ff = w1.shape
    rows = ROW_BLOCK
    grid_spec = pltpu.PrefetchScalarGridSpec(
        num_scalar_prefetch=2,
        grid=(nb,),
        in_specs=[
            pl.BlockSpec((rows, s, lanes), lambda b, be, nu: (jnp.minimum(b, nu[0] - 1), 0, 0)),
            pl.BlockSpec((1, 1, d, ff), lambda b, be, nu: (layer, be[b], 0, 0)),
            pl.BlockSpec((1, 1, d, ff), lambda b, be, nu: (layer, be[b], 0, 0)),
            pl.BlockSpec((1, 1, ff, d), lambda b, be, nu: (layer, be[b], 0, 0)),
        ],
        out_specs=pl.BlockSpec((rows, s, lanes), lambda b, be, nu: (b, 0, 0)),
        scratch_shapes=[pltpu.VMEM((d, 2 * ff), bf16), pltpu.VMEM((ff, d), bf16)],
    )
    return pl.pallas_call(
        _expert_kernel,
        grid_spec=grid_spec,
        out_shape=jax.ShapeDtypeStruct((nrows, s, lanes), u32),
        compiler_params=_cparams(("arbitrary",)),
        name="experts",
    )(block_e, nused, xbuf, w1, w3, w2)


def _combine_kernel(dc_ref, dn_ref, x_ref, hp_ref, w_ref, g2_ref, sw13_ref, sw2_ref, fg_ref, y_hbm,
                    o_ref, buf_ref, acc_ref, sem, *, final):
    i = pl.program_id(0)
    nsteps = pl.num_programs(0)
    slot = i % 2
    tm = x_ref.shape[0]
    half = hp_ref.shape[1] * hp_ref.shape[2]
    sff = sw2_ref.shape[0]

    def start_token(d_ref, s, t):
        for k in range(TOP_K):
            pltpu.make_async_copy(y_hbm.at[d_ref[0, 0, t * TOP_K + k]], buf_ref.at[s, k, t],
                                  sem.at[s]).start(priority=k % 2)

    def reduce_token(s, t):
        acc_lo = acc_hi = None
        for k in range(TOP_K):
            ylo, yhi = _unpack_halves(buf_ref[s, k, t])
            wk = w_ref[0, 0, t * TOP_K + k]
            acc_lo = wk * ylo if acc_lo is None else acc_lo + wk * ylo
            acc_hi = wk * yhi if acc_hi is None else acc_hi + wk * yhi
        acc_ref[0, t] = acc_lo
        acc_ref[1, t] = acc_hi

    @pl.when(i == 0)
    def _():
        def body(t, carry):
            start_token(dc_ref, 0, t)
            return carry

        lax.fori_loop(0, tm, body, 0, unroll=2)

    lo, hi = _unpack_halves(_tiles_to_rows(hp_ref[...]))
    hx = jnp.concatenate([lo, hi], axis=1).astype(bf16)
    hcat = jnp.dot(hx, sw13_ref[...], preferred_element_type=f32)
    a = jax.nn.silu(hcat[:, :sff]) * hcat[:, sff:]
    shared = jnp.dot(a.astype(bf16), sw2_ref[...], preferred_element_type=f32)

    for k in range(TOP_K):
        pltpu.make_async_copy(y_hbm.at[pl.ds(0, tm)], buf_ref.at[slot, k], sem.at[slot]).wait()

    for s in range(2):
        @pl.when((i + 1 < nsteps) & (slot == s))
        def _():
            def body(t, carry):
                start_token(dn_ref, 1 - s, t)
                reduce_token(s, t)
                return carry

            lax.fori_loop(0, tm, body, 0, unroll=2)

        @pl.when((i + 1 == nsteps) & (slot == s))
        def _():
            def body(t, carry):
                reduce_token(s, t)
                return carry

            lax.fori_loop(0, tm, body, 0, unroll=2)

    acc_lo = _tiles_to_rows(acc_ref[0])
    acc_hi = _tiles_to_rows(acc_ref[1])
    g2 = g2_ref[0]
    out_lo = x_ref[:, :half] + g2[:, :half] * (shared[:, :half] + acc_lo)
    out_hi = x_ref[:, half:] + g2[:, half:] * (shared[:, half:] + acc_hi)
    if final:
        ms = (jnp.sum(out_lo * out_lo, axis=-1, keepdims=True)
              + jnp.sum(out_hi * out_hi, axis=-1, keepdims=True)) / (2 * half)
        inv = lax.rsqrt(ms + EPS)
        out_lo = out_lo * inv * fg_ref[:, :half]
        out_hi = out_hi * inv * fg_ref[:, half:]
    o_ref[:, :half] = out_lo
    o_ref[:, half:] = out_hi


def _combine(x, hp, wts, dest_t, g2, sw13, sw2, final_g, ybuf, seq, final):
    n, d = x.shape
    _, s, lanes = hp.shape
    tm = min(TILE_COMB, seq)
    nt = n // tm
    tiles_per_seq = seq // tm
    dest = _token_major(dest_t, tm)
    pairs = tm * TOP_K
    return pl.pallas_call(
        functools.partial(_combine_kernel, final=final),
        grid=(nt,),
        in_specs=[
            pl.BlockSpec((1, 1, pairs), lambda i: (i, 0, 0), memory_space=pltpu.SMEM),
            pl.BlockSpec((1, 1, pairs), lambda i: (jnp.minimum(i + 1, nt - 1), 0, 0), memory_space=pltpu.SMEM),
            pl.BlockSpec((tm, d), lambda i: (i, 0)),
            pl.BlockSpec((tm, s, lanes), lambda i: (i, 0, 0)),
            pl.BlockSpec((1, 1, pairs), lambda i: (i, 0, 0), memory_space=pltpu.SMEM),
            pl.BlockSpec((1, 1, d), lambda i: (i // tiles_per_seq, 0, 0)),
            pl.BlockSpec(sw13.shape, lambda i: (0, 0)),
            pl.BlockSpec(sw2.shape, lambda i: (0, 0)),
            pl.BlockSpec((1, d), lambda i: (0, 0)),
            pl.BlockSpec(memory_space=pl.ANY),
        ],
        out_specs=pl.BlockSpec((tm, d), lambda i: (i, 0)),
        out_shape=jax.ShapeDtypeStruct((n, d), f32),
        scratch_shapes=[pltpu.VMEM((2, TOP_K, tm, s, lanes), u32), pltpu.VMEM((2, tm, s, lanes), f32),
                        pltpu.SemaphoreType.DMA((2,))],
        compiler_params=_cparams(("arbitrary",)),
        name="combine",
    )(dest, dest, x, hp, _token_major(wts, tm), g2, sw13, sw2, final_g.reshape(1, d), ybuf)


def _moe(x, g, sc, sh, g2, router_w, router_b, w1, w3, w2, layer, sw1, sw3, sw2, final_g, seq, final):
    n, d = x.shape
    ne = router_w.shape[1]
    hp, eidx_t, wts_t, rank_t, cnt = _route(x, g, sc, sh, router_w, router_b, seq)
    counts = cnt[:, 0]
    pcounts = (counts + ROW_BLOCK - 1) // ROW_BLOCK * ROW_BLOCK
    pends = jnp.cumsum(pcounts)
    poffs = pends - pcounts
    nb = -(-(n * TOP_K) // ROW_BLOCK) + ne
    starts = jnp.arange(nb, dtype=i32) * ROW_BLOCK
    block_e = jnp.minimum(jnp.sum((pends[None, :] <= starts[:, None]).astype(i32), axis=1), ne - 1)
    nused = (pends[-1:] // ROW_BLOCK).astype(i32)
    last_blocks = jnp.where(pcounts > 0, pends // ROW_BLOCK - 1, -1)
    tail = nused[0] + jnp.arange(nb - (n * TOP_K) // ROW_BLOCK, dtype=i32)
    zero_blocks = jnp.concatenate([last_blocks, jnp.where(tail < nb, tail, -1)]).astype(i32)
    dest_t = _dest(poffs.astype(i32), eidx_t, rank_t)
    xbuf = _dispatch(hp, dest_t, nused, zero_blocks, nb)
    ybuf = _experts(xbuf, block_e, nused, w1, w3, w2, layer)
    sw13 = jnp.concatenate([sw1, sw3], axis=-1).astype(bf16)
    return _combine(x, hp, wts_t, dest_t, g2, sw13, sw2.astype(bf16), final_g, ybuf, seq, final)


def kernel(x, c, ada_w, ada_b, norm_mix_g, norm_ffn_g, ab_w_in, sgu_ln_g, sgu_ln_b, sgu_w, sgu_b, pool_w,
           pool_scale, ab_w_out, cd_w_in, conv3_w, conv3_b, conv31_w, conv31_b, cd_ln_g, cd_ln_b, cd_w_out,
           router_w, router_b, exp_w1, exp_w3, exp_w2, sh_w1, sh_w3, sh_w2, final_g):
    bsz, seq, d = x.shape
    depth = ada_w.shape[0]
    n = bsz * seq
    xf = x.reshape(n, d)
    mod = _ada(c, ada_w, ada_b)
    for l in range(depth):
        sh1, sc1, g1, sh2, sc2, g2 = [mod[l, :, i * d:(i + 1) * d].reshape(bsz, 1, d) for i in range(6)]
        if l % 2 == 0:
            e = l // 2
            p = _norm_mm(xf, norm_mix_g[l], sc1, sh1, ab_w_in[e].astype(bf16), seq)
            xf = _ab_mix(xf, p, g1, sgu_ln_g[e], sgu_ln_b[e], sgu_w[e], sgu_b[e], pool_w[e], pool_scale[e],
                         ab_w_out[e].astype(bf16), bsz, seq)
        else:
            o = l // 2
            p = _norm_mm(xf, norm_mix_g[l], sc1, sh1, cd_w_in[o].astype(bf16), seq)
            xf = _cd_mix(xf, p, g1, conv3_w[o], conv3_b[o], conv31_w[o], conv31_b[o], cd_ln_g[o], cd_ln_b[o],
                         cd_w_out[o].astype(bf16), bsz, seq)
        xf = _moe(xf, norm_ffn_g[l], sc2, sh2, g2, router_w[l], router_b[l], exp_w1, exp_w3, exp_w2, l,
                  sh_w1[l], sh_w3[l], sh_w2[l], final_g, seq, final=(l == depth - 1))
    return xf.reshape(bsz, seq, d)
```

```python
import functools

import jax
import jax.numpy as jnp
from jax import lax
from jax.experimental import pallas as pl
from jax.experimental.pallas import tpu as pltpu

f32 = jnp.float32
bf16 = jnp.bfloat16
u32 = jnp.uint32
i32 = jnp.int32

EPS = 1e-6
CHUNK = 64
SGU_BLOCK = 128
A_HEADS = 8
POOL_WINDOWS = (2, 4, 8, 16)
SHORT_CONV = 3
CONF_CONV = 31
N_EXPERTS = 64
TOP_K = 6
N_GROUPS = 8
TOPK_GROUPS = 4
ROUTED_SCALE = 2.5

V7X_LANES = 128
V7X_SUBLANES = 8
V7X_VMEM_LIMIT = 56 * 1024 * 1024

ROW_BLOCK = 512
TILE_MM = 1024
TILE_MIX = 512
TILE_ROUTE = 512
EXPERT_CHUNKS = 4
TILE_DISPATCH = 1024
TILE_COMB = 256
CONV_ROWS = 128
CONV_COLS = 128
LN_ROWS = 32
NORM_ROWS = 16
HI_MASK = 0xFFFF0000


def _cparams(sem):
    return pltpu.CompilerParams(dimension_semantics=sem, vmem_limit_bytes=V7X_VMEM_LIMIT)


def _pack_halves(v):
    w = v.shape[1] // 2
    bits = pltpu.bitcast(v, u32)
    return (bits[:, :w] >> 16) | bits[:, w:]


def _unpack_halves(words):
    lo = pltpu.bitcast(words << 16, f32)
    hi = pltpu.bitcast(words & jnp.uint32(HI_MASK), f32)
    return lo, hi


def _rows_to_tiles(m):
    s = m.shape[1] // V7X_LANES
    st = jnp.stack([m[:, i * V7X_LANES:(i + 1) * V7X_LANES] for i in range(s)], axis=0)
    return pltpu.einshape("srl->rsl", st)


def _tiles_to_rows(t3):
    xt = pltpu.einshape("rsl->srl", t3)
    return jnp.concatenate([xt[i] for i in range(t3.shape[1])], axis=1)


def _modulated_rmsnorm(x, g, sc, sh):
    ms = jnp.mean(x * x, axis=-1, keepdims=True)
    y = x * lax.rsqrt(ms + EPS) * g
    return y * (1.0 + sc) + sh


def _ada_kernel(c_ref, w_ref, b_ref, o_ref):
    ca = jax.nn.silu(c_ref[...]).astype(bf16)
    o_ref[0] = jnp.dot(ca, w_ref[0].astype(bf16), preferred_element_type=f32) + b_ref[0]


def _ada(c, ada_w, ada_b):
    depth, d, n6 = ada_w.shape
    bsz = c.shape[0]
    tn = 1024
    return pl.pallas_call(
        _ada_kernel,
        grid=(depth, n6 // tn),
        in_specs=[
            pl.BlockSpec((bsz, d), lambda l, j: (0, 0)),
            pl.BlockSpec((1, d, tn), lambda l, j: (l, 0, j)),
            pl.BlockSpec((1, 1, tn), lambda l, j: (l, 0, j)),
        ],
        out_specs=pl.BlockSpec((1, bsz, tn), lambda l, j: (l, 0, j)),
        out_shape=jax.ShapeDtypeStruct((depth, bsz, n6), f32),
        compiler_params=_cparams(("parallel", "parallel")),
        name="ada_mod",
    )(c, ada_w, ada_b.reshape(depth, 1, n6))


def _norm_mm_kernel(x_ref, g_ref, sc_ref, sh_ref, w_ref, o_ref, h_ref):
    @pl.when(pl.program_id(1) == 0)
    def _():
        def body(r, carry):
            rows = pl.ds(pl.multiple_of(r * NORM_ROWS, NORM_ROWS), NORM_ROWS)
            h_ref[rows, :] = _modulated_rmsnorm(x_ref[rows, :], g_ref[...], sc_ref[0], sh_ref[0]).astype(bf16)
            return carry

        lax.fori_loop(0, x_ref.shape[0] // NORM_ROWS, body, 0, unroll=8)

    o_ref[...] = jnp.dot(h_ref[...], w_ref[...], preferred_element_type=f32).astype(o_ref.dtype)


def _norm_mm(x, g, sc, sh, w, seq):
    n, d = x.shape
    nout = w.shape[1]
    tm = min(TILE_MM, seq)
    tn = 1024
    tiles_per_seq = seq // tm
    return pl.pallas_call(
        _norm_mm_kernel,
        grid=(n // tm, nout // tn),
        in_specs=[
            pl.BlockSpec((tm, d), lambda i, j: (i, 0)),
            pl.BlockSpec((1, d), lambda i, j: (0, 0)),
            pl.BlockSpec((1, 1, d), lambda i, j: (i // tiles_per_seq, 0, 0)),
            pl.BlockSpec((1, 1, d), lambda i, j: (i // tiles_per_seq, 0, 0)),
            pl.BlockSpec((d, tn), lambda i, j: (0, j)),
        ],
        out_specs=pl.BlockSpec((tm, tn), lambda i, j: (i, j)),
        out_shape=jax.ShapeDtypeStruct((n, nout), bf16),
        scratch_shapes=[pltpu.VMEM((tm, d), bf16)],
        compiler_params=_cparams(("parallel", "arbitrary")),
        name="norm_in_proj",
    )(x, g.reshape(1, d), sc, sh, w)


def _ab_mix_kernel(x_ref, p_ref, g1_ref, lng_ref, lnb_ref, sw_ref, sbt_ref, pw_ref, ps_ref, wo_ref,
                   o_ref, ext_ref, cat_ref):
    s_idx = pl.program_id(1)
    ts = x_ref.shape[0]
    wa = lng_ref.shape[1]
    hd = wa // A_HEADS
    gb = pw_ref.shape[1]
    blk = SGU_BLOCK

    @pl.when(s_idx == 0)
    def _():
        ext_ref[0:blk, :] = jnp.zeros((blk, ext_ref.shape[1]), bf16)

    @pl.when(s_idx > 0)
    def _():
        ext_ref[0:blk, :] = ext_ref[ts:ts + blk, :]

    ext_ref[blk:, :] = p_ref[:, 2 * wa:]

    ci = lax.broadcasted_iota(i32, (blk, blk), 0) // CHUNK
    cj = lax.broadcasted_iota(i32, (blk, blk), 1) // CHUNK
    causal = ci >= cj
    trow = lax.broadcasted_iota(i32, (blk, 2 * blk), 0) + blk
    tcol = lax.broadcasted_iota(i32, (blk, 2 * blk), 1)
    tpos = lax.broadcasted_iota(i32, (blk, 1), 0)

    def block_body(bi, carry):
        r0 = pl.multiple_of(bi * blk, blk)
        rows = pl.ds(r0, blk)
        v = p_ref[rows, wa:2 * wa].astype(f32)
        mu = jnp.mean(v, axis=-1, keepdims=True)
        vc = v - mu
        var = jnp.mean(vc * vc, axis=-1, keepdims=True)
        vn = (vc * lax.rsqrt(var + EPS) * lng_ref[...] + lnb_ref[...]).astype(bf16)
        for h in range(A_HEADS):
            cols = slice(h * hd, (h + 1) * hd)
            wm = jnp.where(causal, sw_ref[h], 0.0).astype(bf16)
            s = jnp.dot(wm, vn[:, cols], preferred_element_type=f32) + sbt_ref[:, h:h + 1]
            u = p_ref[rows, cols].astype(f32)
            cat_ref[rows, cols] = (u * s).astype(bf16)
        win_rows = pl.ds(r0, 2 * blk)
        pos = (s_idx * ts + r0 + tpos + 1).astype(f32)
        for gi, win in enumerate(POOL_WINDOWS):
            cols = slice(gi * gb, (gi + 1) * gb)
            band = jnp.where((tcol <= trow) & (tcol > trow - win), 1.0, 0.0).astype(bf16)
            window = ext_ref[win_rows, cols]
            sums = jnp.dot(band, window, preferred_element_type=f32)
            xb = window[blk:, :].astype(f32)
            pooled = sums / jnp.minimum(pos, float(win)) - xb
            yb = jnp.dot(pooled.astype(bf16), pw_ref[gi], preferred_element_type=f32) * ps_ref[:, cols]
            cat_ref[rows, wa + gi * gb:wa + (gi + 1) * gb] = yb.astype(bf16)
        return carry

    lax.fori_loop(0, ts // blk, block_body, 0)
    mix = jnp.dot(cat_ref[...], wo_ref[...], preferred_element_type=f32)
    o_ref[...] = x_ref[...] + g1_ref[0] * mix


def _ab_mix(x, p, g1, ln_g, ln_b, sgu_w, sgu_b, pool_w, pool_scale, w_out, bsz, seq):
    n, d = x.shape
    wa = ln_g.shape[0]
    wb = pool_scale.shape[0]
    ts = min(TILE_MIX, seq)
    nst = seq // ts
    const2 = lambda b, s: (0, 0)
    const3 = lambda b, s: (0, 0, 0)
    return pl.pallas_call(
        _ab_mix_kernel,
        grid=(bsz, nst),
        in_specs=[
            pl.BlockSpec((ts, d), lambda b, s: (b * nst + s, 0)),
            pl.BlockSpec((ts, p.shape[1]), lambda b, s: (b * nst + s, 0)),
            pl.BlockSpec((1, 1, d), lambda b, s: (b, 0, 0)),
            pl.BlockSpec((1, wa), const2),
            pl.BlockSpec((1, wa), const2),
            pl.BlockSpec(sgu_w.shape, const3),
            pl.BlockSpec((SGU_BLOCK, A_HEADS), const2),
            pl.BlockSpec(pool_w.shape, const3),
            pl.BlockSpec((1, wb), const2),
            pl.BlockSpec(w_out.shape, const2),
        ],
        out_specs=pl.BlockSpec((ts, d), lambda b, s: (b * nst + s, 0)),
        out_shape=jax.ShapeDtypeStruct((n, d), f32),
        scratch_shapes=[pltpu.VMEM((ts + SGU_BLOCK, wb), bf16), pltpu.VMEM((ts, wa + wb), bf16)],
        compiler_params=_cparams(("parallel", "arbitrary")),
        name="mixer_ab",
    )(x, p, g1, ln_g.reshape(1, wa), ln_b.reshape(1, wa), sgu_w, sgu_b.T, pool_w.astype(bf16),
      pool_scale.reshape(1, wb), w_out)


def _cd_mix_kernel(x_ref, p_ref, g1_ref, c3w_ref, c3b_ref, c31w_ref, c31b_ref, lng_ref, lnb_ref, wo_ref,
                   o_ref, e3_ref, e31_ref, d_ref, cat_ref):
    s_idx = pl.program_id(1)
    ts = x_ref.shape[0]
    wc = c3b_ref.shape[1]
    wd = c31b_ref.shape[1]
    h3 = e3_ref.shape[0] - ts
    h31 = e31_ref.shape[0] - ts

    @pl.when(s_idx == 0)
    def _():
        e3_ref[0:h3, :] = jnp.zeros((h3, wc), f32)
        e31_ref[0:h31, :] = jnp.zeros((h31, wd), f32)

    @pl.when(s_idx > 0)
    def _():
        e3_ref[0:h3, :] = e3_ref[ts:ts + h3, :]
        e31_ref[0:h31, :] = e31_ref[ts:ts + h31, :]

    cg = p_ref[:, wc:2 * wc].astype(f32)
    xc = p_ref[:, 2 * wc:3 * wc].astype(f32)
    e3_ref[h3:, :] = cg * xc
    da = p_ref[:, 3 * wc:3 * wc + wd].astype(f32)
    dg = p_ref[:, 3 * wc + wd:].astype(f32)
    e31_ref[h31:, :] = da * jax.nn.sigmoid(dg)

    cw = CONV_COLS
    rr = CONV_ROWS
    sub = V7X_SUBLANES

    def causal_conv(w_ref, b_ref, e_ref, halo, r0, cols):
        ntaps = w_ref.shape[0]
        acc = jnp.broadcast_to(b_ref[:, cols], (rr, cw))
        for b in range(sub):
            taps = [k for k in range(ntaps) if (halo - (ntaps - 1) + k) % sub == b]
            if not taps:
                continue
            nload = rr + sub if b else rr
            z = None
            for k in taps:
                a8 = (halo - (ntaps - 1) + k) // sub * sub
                t = w_ref[k:k + 1, cols] * e_ref[pl.ds(r0 + a8, nload), cols]
                z = t if z is None else z + t
            if b:
                z = pltpu.roll(z, nload - b, 0)[:rr]
            acc = acc + z
        return acc

    def rows_body(ri, carry):
        r0 = pl.multiple_of(ri * rr, rr)
        rows = pl.ds(r0, rr)
        for c in range(wc // cw):
            cols = slice(c * cw, (c + 1) * cw)
            acc = causal_conv(c3w_ref, c3b_ref, e3_ref, h3, r0, cols)
            cat_ref[rows, cols] = (p_ref[rows, cols].astype(f32) * acc).astype(bf16)
        for c in range(wd // cw):
            cols = slice(c * cw, (c + 1) * cw)
            d_ref[rows, cols] = causal_conv(c31w_ref, c31b_ref, e31_ref, h31, r0, cols)
        for q in range(rr // LN_ROWS):
            qrows = pl.ds(r0 + q * LN_ROWS, LN_ROWS)
            dv = d_ref[qrows, :]
            mu = jnp.mean(dv, axis=-1, keepdims=True)
            dc = dv - mu
            var = jnp.mean(dc * dc, axis=-1, keepdims=True)
            dn = dc * lax.rsqrt(var + EPS) * lng_ref[...] + lnb_ref[...]
            cat_ref[qrows, wc:] = jax.nn.silu(dn).astype(bf16)
        return carry

    lax.fori_loop(0, ts // rr, rows_body, 0)
    mix = jnp.dot(cat_ref[...], wo_ref[...], preferred_element_type=f32)
    o_ref[...] = x_ref[...] + g1_ref[0] * mix


def _cd_mix(x, p, g1, c3w, c3b, c31w, c31b, ln_g, ln_b, w_out, bsz, seq):
    n, d = x.shape
    wc = c3b.shape[0]
    wd = c31b.shape[0]
    ts = min(TILE_MIX, seq)
    nst = seq // ts
    h3 = V7X_SUBLANES
    h31 = 4 * V7X_SUBLANES
    assert h3 >= SHORT_CONV - 1 and h31 >= CONF_CONV - 1
    const2 = lambda b, s: (0, 0)
    return pl.pallas_call(
        _cd_mix_kernel,
        grid=(bsz, nst),
        in_specs=[
            pl.BlockSpec((ts, d), lambda b, s: (b * nst + s, 0)),
            pl.BlockSpec((ts, p.shape[1]), lambda b, s: (b * nst + s, 0)),
            pl.BlockSpec((1, 1, d), lambda b, s: (b, 0, 0)),
            pl.BlockSpec(c3w.shape, const2),
            pl.BlockSpec((1, wc), const2),
            pl.BlockSpec(c31w.shape, const2),
            pl.BlockSpec((1, wd), const2),
            pl.BlockSpec((1, wd), const2),
            pl.BlockSpec((1, wd), const2),
            pl.BlockSpec(w_out.shape, const2),
        ],
        out_specs=pl.BlockSpec((ts, d), lambda b, s: (b * nst + s, 0)),
        out_shape=jax.ShapeDtypeStruct((n, d), f32),
        scratch_shapes=[pltpu.VMEM((ts + h3, wc), f32), pltpu.VMEM((ts + h31, wd), f32),
                        pltpu.VMEM((ts, wd), f32), pltpu.VMEM((ts, wc + wd), bf16)],
        compiler_params=_cparams(("parallel", "arbitrary")),
        name="mixer_cd",
    )(x, p, g1, c3w, c3b.reshape(1, wc), c31w, c31b.reshape(1, wd), ln_g.reshape(1, wd), ln_b.reshape(1, wd),
      w_out)


def _route_kernel(x_ref, g_ref, sc_ref, sh_ref, rwh_ref, rwl_ref, rb_ref,
                  hp_ref, eidx_ref, wts_ref, rank_ref, cnt_ref, carry_ref):
    tm = x_ref.shape[0]
    ne = rwh_ref.shape[0]
    gsz = ne // N_GROUPS
    neg = -jnp.inf

    @pl.when(pl.program_id(0) == 0)
    def _():
        carry_ref[...] = jnp.zeros_like(carry_ref)

    h = _modulated_rmsnorm(x_ref[...], g_ref[...], sc_ref[0], sh_ref[0])
    hh = h.astype(bf16)
    hf = hh.astype(f32)
    hl = (h - hf).astype(bf16)
    hp_ref[...] = _rows_to_tiles(_pack_halves(hf))

    nt = (((1,), (1,)), ((), ()))
    logits = (lax.dot_general(rwh_ref[...], hh, nt, preferred_element_type=f32)
              + lax.dot_general(rwh_ref[...], hl, nt, preferred_element_type=f32)
              + lax.dot_general(rwl_ref[...], hh, nt, preferred_element_type=f32))
    scores = jax.nn.sigmoid(logits)
    choice = scores + rb_ref[...]

    c3 = choice.reshape(N_GROUPS, gsz, tm)
    sub = lax.broadcasted_iota(i32, c3.shape, 1)
    m1 = jnp.max(c3, axis=1, keepdims=True)
    first = jnp.min(jnp.where(c3 == m1, sub, gsz), axis=1, keepdims=True)
    m2 = jnp.max(jnp.where(sub == first, neg, c3), axis=1, keepdims=True)
    gs = m1 + m2
    keep = []
    for gi in range(N_GROUPS):
        beaten = jnp.zeros((1, tm), i32)
        for gj in range(N_GROUPS):
            if gj == gi:
                continue
            wins = (gs[gj] > gs[gi]) | ((gs[gj] == gs[gi]) & (gj < gi))
            beaten = beaten + wins.astype(i32)
        keep.append(beaten < TOPK_GROUPS)
    masked = jnp.concatenate(
        [jnp.where(keep[gi], c3[gi], neg) for gi in range(N_GROUPS)], axis=0)

    eio = lax.broadcasted_iota(i32, (ne, tm), 0)
    cur = masked
    onehot = jnp.zeros((ne, tm), f32)
    idxs, sels, svals = [], [], []
    for _ in range(TOP_K):
        m = jnp.max(cur, axis=0, keepdims=True)
        idx = jnp.min(jnp.where(cur == m, eio, ne), axis=0, keepdims=True)
        sel = eio == idx
        svals.append(jnp.sum(jnp.where(sel, scores, 0.0), axis=0, keepdims=True))
        cur = jnp.where(sel, neg, cur)
        onehot = onehot + sel.astype(f32)
        idxs.append(idx)
        sels.append(sel)
    ssum = svals[0]
    for sv in svals[1:]:
        ssum = ssum + sv

    tr = lax.broadcasted_iota(i32, (tm, tm), 0)
    tc = lax.broadcasted_iota(i32, (tm, tm), 1)
    upper = jnp.where(tr < tc, 1.0, 0.0).astype(bf16)
    before = jnp.dot(onehot.astype(bf16), upper, preferred_element_type=f32) + carry_ref[...]
    new_carry = carry_ref[...] + jnp.sum(onehot, axis=1, keepdims=True)
    carry_ref[...] = new_carry
    cnt_ref[...] = jnp.broadcast_to(new_carry, cnt_ref.shape).astype(i32)

    zrow_i = jnp.zeros((1, tm), i32)
    zrow_f = jnp.zeros((1, tm), f32)
    ranks = [jnp.sum(jnp.where(sel, before, 0.0), axis=0, keepdims=True).astype(i32) for sel in sels]
    pad = V7X_SUBLANES - TOP_K
    eidx_ref[...] = jnp.concatenate(idxs + [zrow_i] * pad, axis=0)
    wts_ref[...] = jnp.concatenate([sv / ssum * ROUTED_SCALE for sv in svals] + [zrow_f] * pad, axis=0)
    rank_ref[...] = jnp.concatenate(ranks + [zrow_i] * pad, axis=0)


def _route(x, g, sc, sh, router_w, router_b, seq):
    n, d = x.shape
    ne = router_w.shape[1]
    tm = min(TILE_ROUTE, seq)
    tiles_per_seq = seq // tm
    rwt = router_w.T
    rwh = rwt.astype(bf16)
    rwl = (rwt - rwh.astype(f32)).astype(bf16)
    slots = V7X_SUBLANES
    return pl.pallas_call(
        _route_kernel,
        grid=(n // tm,),
        in_specs=[
            pl.BlockSpec((tm, d), lambda i: (i, 0)),
            pl.BlockSpec((1, d), lambda i: (0, 0)),
            pl.BlockSpec((1, 1, d), lambda i: (i // tiles_per_seq, 0, 0)),
            pl.BlockSpec((1, 1, d), lambda i: (i // tiles_per_seq, 0, 0)),
            pl.BlockSpec((ne, d), lambda i: (0, 0)),
            pl.BlockSpec((ne, d), lambda i: (0, 0)),
            pl.BlockSpec((ne, 1), lambda i: (0, 0)),
        ],
        out_specs=[
            pl.BlockSpec((tm, d // 2 // V7X_LANES, V7X_LANES), lambda i: (i, 0, 0)),
            pl.BlockSpec((slots, tm), lambda i: (0, i)),
            pl.BlockSpec((slots, tm), lambda i: (0, i)),
            pl.BlockSpec((slots, tm), lambda i: (0, i)),
            pl.BlockSpec((ne, V7X_LANES), lambda i: (0, 0)),
        ],
        out_shape=[
            jax.ShapeDtypeStruct((n, d // 2 // V7X_LANES, V7X_LANES), u32),
            jax.ShapeDtypeStruct((slots, n), i32),
            jax.ShapeDtypeStruct((slots, n), f32),
            jax.ShapeDtypeStruct((slots, n), i32),
            jax.ShapeDtypeStruct((ne, V7X_LANES), i32),
        ],
        scratch_shapes=[pltpu.VMEM((ne, 1), f32)],
        compiler_params=_cparams(("arbitrary",)),
        name="route",
    )(x, g.reshape(1, d), sc, sh, rwh, rwl, router_b.reshape(ne, 1))


def _dest_kernel(poffs_ref, eidx_ref, rank_ref, dest_ref):
    eidx = eidx_ref[...]
    base = jnp.zeros(eidx.shape, i32)
    for e in range(N_EXPERTS):
        base = jnp.where(eidx == e, poffs_ref[e], base)
    dest_ref[...] = base + rank_ref[...]


def _dest(poffs, eidx_t, rank_t):
    slots, n = eidx_t.shape
    td = min(4096, n)
    grid_spec = pltpu.PrefetchScalarGridSpec(
        num_scalar_prefetch=1,
        grid=(n // td,),
        in_specs=[pl.BlockSpec((slots, td), lambda i, po: (0, i)),
                  pl.BlockSpec((slots, td), lambda i, po: (0, i))],
        out_specs=pl.BlockSpec((slots, td), lambda i, po: (0, i)),
    )
    return pl.pallas_call(
        _dest_kernel,
        grid_spec=grid_spec,
        out_shape=jax.ShapeDtypeStruct((slots, n), i32),
        compiler_params=_cparams(("parallel",)),
        name="dest_rows",
    )(poffs, eidx_t, rank_t)


def _dispatch_kernel(nu_ref, zb_ref, dest_ref, hp_ref, xb_hbm, zero_ref, zsem, sem):
    i = pl.program_id(0)
    tm = hp_ref.shape[0]
    rows = zero_ref.shape[0]
    nz = zb_ref.shape[0]

    @pl.when(i == 0)
    def _():
        zero_ref[...] = jnp.zeros_like(zero_ref)

        def zcopy(j):
            return pltpu.make_async_copy(zero_ref, xb_hbm.at[pl.ds(zb_ref[j] * rows, rows)], zsem)

        def zstart(j, carry):
            @pl.when(zb_ref[j] >= 0)
            def _():
                zcopy(j).start()
            return carry

        def zwait(j, carry):
            @pl.when(zb_ref[j] >= 0)
            def _():
                zcopy(j).wait()
            return carry

        lax.fori_loop(0, nz, zstart, 0)
        lax.fori_loop(0, nz, zwait, 0)

    def body(t, carry):
        for k in range(TOP_K):
            pltpu.make_async_copy(hp_ref.at[t], xb_hbm.at[dest_ref[0, 0, t * TOP_K + k]],
                                  sem).start(priority=k % 2)
        return carry

    lax.fori_loop(0, tm, body, 0, unroll=2)
    for k in range(TOP_K):
        pltpu.make_async_copy(hp_ref, xb_hbm.at[pl.ds(0, tm)], sem).wait()


def _token_major(a_t, tm):
    n = a_t.shape[1]
    return a_t[:TOP_K].T.reshape(n // tm, 1, tm * TOP_K)


def _dispatch(hp, dest_t, nused, zero_blocks, nb):
    n, s, lanes = hp.shape
    tm = min(TILE_DISPATCH, n)
    grid_spec = pltpu.PrefetchScalarGridSpec(
        num_scalar_prefetch=2,
        grid=(n // tm,),
        in_specs=[
            pl.BlockSpec((1, 1, tm * TOP_K), lambda i, nu, zb: (i, 0, 0), memory_space=pltpu.SMEM),
            pl.BlockSpec((tm, s, lanes), lambda i, nu, zb: (i, 0, 0)),
        ],
        out_specs=pl.BlockSpec(memory_space=pl.ANY),
        scratch_shapes=[pltpu.VMEM((ROW_BLOCK, s, lanes), u32), pltpu.SemaphoreType.DMA(()),
                        pltpu.SemaphoreType.DMA(())],
    )
    return pl.pallas_call(
        _dispatch_kernel,
        grid_spec=grid_spec,
        out_shape=jax.ShapeDtypeStruct((nb * ROW_BLOCK, s, lanes), u32),
        compiler_params=_cparams(("arbitrary",)),
        name="dispatch",
    )(nused, zero_blocks, _token_major(dest_t, tm), hp)


def _expert_kernel(be_ref, nu_ref, eord_ref, elist_ref, nexp_ref, x_ref, w1_hbm, w3_hbm, w2_hbm, y_ref,
                   w1_st, w3_st, w2_st, w13_s, w2_s, wsem, *, layer):
    b = pl.program_id(0)
    nused = nu_ref[0]
    ff = w2_s.shape[0]
    rows = x_ref.shape[0]

    def weight_copies(j, slot):
        e = elist_ref[j]
        return [pltpu.make_async_copy(src.at[layer, e], dst.at[slot], wsem.at[slot])
                for src, dst in ((w1_hbm, w1_st), (w3_hbm, w3_st), (w2_hbm, w2_st))]

    @pl.when(b == 0)
    def _():
        for cp in weight_copies(0, 0):
            cp.start()

    j = eord_ref[b]
    first = (b == 0) | (be_ref[b] != be_ref[jnp.maximum(b - 1, 0)])

    @pl.when(first & (b < nused))
    def _():
        slot = j % 2
        for cp in weight_copies(j, slot):
            cp.wait()

        @pl.when(j + 1 < nexp_ref[0])
        def _():
            for cp in weight_copies(j + 1, 1 - slot):
                cp.start()

        w13_s[:, :ff] = w1_st[slot].astype(bf16)
        w13_s[:, ff:] = w3_st[slot].astype(bf16)
        w2_s[...] = w2_st[slot].astype(bf16)

    @pl.when(b < nused)
    def _():
        cr = rows // EXPERT_CHUNKS
        for c in range(EXPERT_CHUNKS):
            rs = slice(c * cr, (c + 1) * cr)
            lo, hi = _unpack_halves(_tiles_to_rows(x_ref[rs]))
            x = jnp.concatenate([lo, hi], axis=1).astype(bf16)
            hcat = jnp.dot(x, w13_s[...], preferred_element_type=f32)
            a = jax.nn.silu(hcat[:, :ff]) * hcat[:, ff:]
            y = jnp.dot(a.astype(bf16), w2_s[...], preferred_element_type=f32)
            y_ref[rs] = _rows_to_tiles(_pack_halves(y.astype(bf16).astype(f32)))

    @pl.when(b >= nused)
    def _():
        y_ref[...] = jnp.zeros_like(y_ref)


def _experts(xbuf, block_e, nused, expert_ord, expert_list, nexp, w1, w3, w2, layer):
    nrows, s, lanes = xbuf.shape
    nb = block_e.shape[0]
    _, ne, d, ff = w1.shape
    rows = ROW_BLOCK
    grid_spec = pltpu.PrefetchScalarGridSpec(
        num_scalar_prefetch=5,
        grid=(nb,),
        in_specs=[
            pl.BlockSpec((rows, s, lanes), lambda b, be, nu, eo, el, nx: (jnp.minimum(b, nu[0] - 1), 0, 0)),
            pl.BlockSpec(memory_space=pl.ANY),
            pl.BlockSpec(memory_space=pl.ANY),
            pl.BlockSpec(memory_space=pl.ANY),
        ],
        out_specs=pl.BlockSpec((rows, s, lanes), lambda b, be, nu, eo, el, nx: (b, 0, 0)),
        scratch_shapes=[pltpu.VMEM((2, d, ff), f32), pltpu.VMEM((2, d, ff), f32), pltpu.VMEM((2, ff, d), f32),
                        pltpu.VMEM((d, 2 * ff), bf16), pltpu.VMEM((ff, d), bf16),
                        pltpu.SemaphoreType.DMA((2,))],
    )
    return pl.pallas_call(
        functools.partial(_expert_kernel, layer=layer),
        grid_spec=grid_spec,
        out_shape=jax.ShapeDtypeStruct((nrows, s, lanes), u32),
        compiler_params=_cparams(("arbitrary",)),
        name="experts",
    )(block_e, nused, expert_ord, expert_list, nexp, xbuf, w1, w3, w2)


def _combine_kernel(dc_ref, dn_ref, x_ref, hp_ref, w_ref, g2_ref, sw13_ref, sw2_ref, fg_ref, y_hbm,
                    o_ref, buf_ref, acc_ref, sem, *, final):
    i = pl.program_id(0)
    nsteps = pl.num_programs(0)
    slot = i % 2
    tm = x_ref.shape[0]
    half = hp_ref.shape[1] * hp_ref.shape[2]
    sff = sw2_ref.shape[0]

    def start_token(d_ref, s, t):
        for k in range(TOP_K):
            pltpu.make_async_copy(y_hbm.at[d_ref[0, 0, t * TOP_K + k]], buf_ref.at[s, t * TOP_K + k],
                                  sem.at[s]).start(priority=k % 2)

    def reduce_token(s, t):
        acc_lo = acc_hi = None
        for k in range(TOP_K):
            ylo, yhi = _unpack_halves(buf_ref[s, t * TOP_K + k])
            wk = w_ref[0, 0, t * TOP_K + k]
            acc_lo = wk * ylo if acc_lo is None else acc_lo + wk * ylo
            acc_hi = wk * yhi if acc_hi is None else acc_hi + wk * yhi
        acc_ref[0, t] = acc_lo
        acc_ref[1, t] = acc_hi

    @pl.when(i == 0)
    def _():
        def body(t, carry):
            start_token(dc_ref, 0, t)
            return carry

        lax.fori_loop(0, tm, body, 0, unroll=2)

    lo, hi = _unpack_halves(_tiles_to_rows(hp_ref[...]))
    hx = jnp.concatenate([lo, hi], axis=1).astype(bf16)
    hcat = jnp.dot(hx, sw13_ref[...], preferred_element_type=f32)
    a = jax.nn.silu(hcat[:, :sff]) * hcat[:, sff:]
    shared = jnp.dot(a.astype(bf16), sw2_ref[...], preferred_element_type=f32)

    for k in range(TOP_K):
        pltpu.make_async_copy(y_hbm.at[pl.ds(0, tm)], buf_ref.at[slot, pl.ds(k * tm, tm)], sem.at[slot]).wait()

    for s in range(2):
        @pl.when((i + 1 < nsteps) & (slot == s))
        def _():
            def body(t, carry):
                start_token(dn_ref, 1 - s, t)
                reduce_token(s, t)
                return carry

            lax.fori_loop(0, tm, body, 0, unroll=2)

        @pl.when((i + 1 == nsteps) & (slot == s))
        def _():
            def body(t, carry):
                reduce_token(s, t)
                return carry

            lax.fori_loop(0, tm, body, 0, unroll=2)

    acc_lo = _tiles_to_rows(acc_ref[0])
    acc_hi = _tiles_to_rows(acc_ref[1])
    g2 = g2_ref[0]
    out_lo = x_ref[:, :half] + g2[:, :half] * (shared[:, :half] + acc_lo)
    out_hi = x_ref[:, half:] + g2[:, half:] * (shared[:, half:] + acc_hi)
    if final:
        ms = (jnp.sum(out_lo * out_lo, axis=-1, keepdims=True)
              + jnp.sum(out_hi * out_hi, axis=-1, keepdims=True)) / (2 * half)
        inv = lax.rsqrt(ms + EPS)
        out_lo = out_lo * inv * fg_ref[:, :half]
        out_hi = out_hi * inv * fg_ref[:, half:]
    o_ref[:, :half] = out_lo
    o_ref[:, half:] = out_hi


def _combine(x, hp, wts, dest_t, g2, sw13, sw2, final_g, ybuf, seq, final):
    n, d = x.shape
    _, s, lanes = hp.shape
    tm = min(TILE_COMB, seq)
    nt = n // tm
    tiles_per_seq = seq // tm
    dest = _token_major(dest_t, tm)
    pairs = tm * TOP_K
    return pl.pallas_call(
        functools.partial(_combine_kernel, final=final),
        grid=(nt,),
        in_specs=[
            pl.BlockSpec((1, 1, pairs), lambda i: (i, 0, 0), memory_space=pltpu.SMEM),
            pl.BlockSpec((1, 1, pairs), lambda i: (jnp.minimum(i + 1, nt - 1), 0, 0), memory_space=pltpu.SMEM),
            pl.BlockSpec((tm, d), lambda i: (i, 0)),
            pl.BlockSpec((tm, s, lanes), lambda i: (i, 0, 0)),
            pl.BlockSpec((1, 1, pairs), lambda i: (i, 0, 0), memory_space=pltpu.SMEM),
            pl.BlockSpec((1, 1, d), lambda i: (i // tiles_per_seq, 0, 0)),
            pl.BlockSpec(sw13.shape, lambda i: (0, 0)),
            pl.BlockSpec(sw2.shape, lambda i: (0, 0)),
            pl.BlockSpec((1, d), lambda i: (0, 0)),
            pl.BlockSpec(memory_space=pl.ANY),
        ],
        out_specs=pl.BlockSpec((tm, d), lambda i: (i, 0)),
        out_shape=jax.ShapeDtypeStruct((n, d), f32),
        scratch_shapes=[pltpu.VMEM((2, tm * TOP_K, s, lanes), u32), pltpu.VMEM((2, tm, s, lanes), f32),
                        pltpu.SemaphoreType.DMA((2,))],
        compiler_params=_cparams(("arbitrary",)),
        name="combine",
    )(dest, dest, x, hp, _token_major(wts, tm), g2, sw13, sw2, final_g.reshape(1, d), ybuf)


def _moe(x, g, sc, sh, g2, router_w, router_b, w1, w3, w2, layer, sw1, sw3, sw2, final_g, seq, final):
    n, d = x.shape
    ne = router_w.shape[1]
    hp, eidx_t, wts_t, rank_t, cnt = _route(x, g, sc, sh, router_w, router_b, seq)
    counts = cnt[:, 0]
    pcounts = (counts + ROW_BLOCK - 1) // ROW_BLOCK * ROW_BLOCK
    pends = jnp.cumsum(pcounts)
    poffs = pends - pcounts
    nb = -(-(n * TOP_K) // ROW_BLOCK) + ne
    starts = jnp.arange(nb, dtype=i32) * ROW_BLOCK
    block_e = jnp.minimum(jnp.sum((pends[None, :] <= starts[:, None]).astype(i32), axis=1), ne - 1)
    nused = (pends[-1:] // ROW_BLOCK).astype(i32)
    last_blocks = jnp.where(pcounts > 0, pends // ROW_BLOCK - 1, -1)
    tail = nused[0] + jnp.arange(nb - (n * TOP_K) // ROW_BLOCK, dtype=i32)
    zero_blocks = jnp.concatenate([last_blocks, jnp.where(tail < nb, tail, -1)]).astype(i32)
    dest_t = _dest(poffs.astype(i32), eidx_t, rank_t)
    xbuf = _dispatch(hp, dest_t, nused, zero_blocks, nb)
    has_rows = pcounts > 0
    expert_list = jnp.nonzero(has_rows, size=ne, fill_value=ne - 1)[0].astype(i32)
    expert_ord = (jnp.cumsum(has_rows.astype(i32)) - 1)[block_e].astype(i32)
    nexp = jnp.sum(has_rows.astype(i32)).reshape(1)
    ybuf = _experts(xbuf, block_e, nused, expert_ord, expert_list, nexp, w1, w3, w2, layer)
    sw13 = jnp.concatenate([sw1, sw3], axis=-1).astype(bf16)
    return _combine(x, hp, wts_t, dest_t, g2, sw13, sw2.astype(bf16), final_g, ybuf, seq, final)


def kernel(x, c, ada_w, ada_b, norm_mix_g, norm_ffn_g, ab_w_in, sgu_ln_g, sgu_ln_b, sgu_w, sgu_b, pool_w,
           pool_scale, ab_w_out, cd_w_in, conv3_w, conv3_b, conv31_w, conv31_b, cd_ln_g, cd_ln_b, cd_w_out,
           router_w, router_b, exp_w1, exp_w3, exp_w2, sh_w1, sh_w3, sh_w2, final_g):
    bsz, seq, d = x.shape
    depth = ada_w.shape[0]
    n = bsz * seq
    xf = x.reshape(n, d)
    mod = _ada(c, ada_w, ada_b)
    for l in range(depth):
        sh1, sc1, g1, sh2, sc2, g2 = [mod[l, :, i * d:(i + 1) * d].reshape(bsz, 1, d) for i in range(6)]
        if l % 2 == 0:
            e = l // 2
            p = _norm_mm(xf, norm_mix_g[l], sc1, sh1, ab_w_in[e].astype(bf16), seq)
            xf = _ab_mix(xf, p, g1, sgu_ln_g[e], sgu_ln_b[e], sgu_w[e], sgu_b[e], pool_w[e], pool_scale[e],
                         ab_w_out[e].astype(bf16), bsz, seq)
        else:
            o = l // 2
            p = _norm_mm(xf, norm_mix_g[l], sc1, sh1, cd_w_in[o].astype(bf16), seq)
            xf = _cd_mix(xf, p, g1, conv3_w[o], conv3_b[o], conv31_w[o], conv31_b[o], cd_ln_g[o], cd_ln_b[o],
                         cd_w_out[o].astype(bf16), bsz, seq)
        xf = _moe(xf, norm_ffn_g[l], sc2, sh2, g2, router_w[l], router_b[l], exp_w1, exp_w3, exp_w2, l,
                  sh_w1[l], sh_w3[l], sh_w2[l], final_g, seq, final=(l == depth - 1))
    return xf.reshape(bsz, seq, d)
```

```python
import functools

import jax
import jax.numpy as jnp
from jax import lax
from jax.experimental import pallas as pl
from jax.experimental.pallas import tpu as pltpu

f32 = jnp.float32
bf16 = jnp.bfloat16
u32 = jnp.uint32
i32 = jnp.int32

EPS = 1e-6
CHUNK = 64
SGU_BLOCK = 128
A_HEADS = 8
POOL_WINDOWS = (2, 4, 8, 16)
SHORT_CONV = 3
CONF_CONV = 31
N_EXPERTS = 64
TOP_K = 6
N_GROUPS = 8
TOPK_GROUPS = 4
ROUTED_SCALE = 2.5

V7X_LANES = 128
V7X_SUBLANES = 8
V7X_VMEM_LIMIT = 56 * 1024 * 1024

ROW_BLOCK = 512
TILE_MM = 1024
TILE_MIX = 512
TILE_ROUTE = 512
EXPERT_CHUNKS = 4
TILE_DISPATCH = 1024
TILE_COMB = 256
CONV_ROWS = 128
CONV_COLS = 128
LN_ROWS = 32
NORM_ROWS = 16
HI_MASK = 0xFFFF0000


def _cparams(sem):
    return pltpu.CompilerParams(dimension_semantics=sem, vmem_limit_bytes=V7X_VMEM_LIMIT)


def _pack_halves(v):
    w = v.shape[1] // 2
    bits = pltpu.bitcast(v, u32)
    return (bits[:, :w] >> 16) | bits[:, w:]


def _unpack_halves(words):
    lo = pltpu.bitcast(words << 16, f32)
    hi = pltpu.bitcast(words & jnp.uint32(HI_MASK), f32)
    return lo, hi


def _rows_to_tiles(m):
    s = m.shape[1] // V7X_LANES
    st = jnp.stack([m[:, i * V7X_LANES:(i + 1) * V7X_LANES] for i in range(s)], axis=0)
    return pltpu.einshape("srl->rsl", st)


def _tiles_to_rows(t3):
    xt = pltpu.einshape("rsl->srl", t3)
    return jnp.concatenate([xt[i] for i in range(t3.shape[1])], axis=1)


def _modulated_rmsnorm(x, g, sc, sh):
    ms = jnp.mean(x * x, axis=-1, keepdims=True)
    y = x * lax.rsqrt(ms + EPS) * g
    return y * (1.0 + sc) + sh


def _ada_kernel(c_ref, w_ref, b_ref, o_ref):
    ca = jax.nn.silu(c_ref[...]).astype(bf16)
    o_ref[0] = jnp.dot(ca, w_ref[0].astype(bf16), preferred_element_type=f32) + b_ref[0]


def _ada(c, ada_w, ada_b):
    depth, d, n6 = ada_w.shape
    bsz = c.shape[0]
    tn = 1024
    return pl.pallas_call(
        _ada_kernel,
        grid=(depth, n6 // tn),
        in_specs=[
            pl.BlockSpec((bsz, d), lambda l, j: (0, 0)),
            pl.BlockSpec((1, d, tn), lambda l, j: (l, 0, j)),
            pl.BlockSpec((1, 1, tn), lambda l, j: (l, 0, j)),
        ],
        out_specs=pl.BlockSpec((1, bsz, tn), lambda l, j: (l, 0, j)),
        out_shape=jax.ShapeDtypeStruct((depth, bsz, n6), f32),
        compiler_params=_cparams(("parallel", "parallel")),
        name="ada_mod",
    )(c, ada_w, ada_b.reshape(depth, 1, n6))


def _norm_mm_kernel(x_ref, g_ref, sc_ref, sh_ref, w_ref, o_ref, h_ref):
    @pl.when(pl.program_id(1) == 0)
    def _():
        def body(r, carry):
            rows = pl.ds(pl.multiple_of(r * NORM_ROWS, NORM_ROWS), NORM_ROWS)
            h_ref[rows, :] = _modulated_rmsnorm(x_ref[rows, :], g_ref[...], sc_ref[0], sh_ref[0]).astype(bf16)
            return carry

        lax.fori_loop(0, x_ref.shape[0] // NORM_ROWS, body, 0, unroll=8)

    o_ref[...] = jnp.dot(h_ref[...], w_ref[...], preferred_element_type=f32).astype(o_ref.dtype)


def _norm_mm(x, g, sc, sh, w, seq):
    n, d = x.shape
    nout = w.shape[1]
    tm = min(TILE_MM, seq)
    tn = 1024
    tiles_per_seq = seq // tm
    return pl.pallas_call(
        _norm_mm_kernel,
        grid=(n // tm, nout // tn),
        in_specs=[
            pl.BlockSpec((tm, d), lambda i, j: (i, 0)),
            pl.BlockSpec((1, d), lambda i, j: (0, 0)),
            pl.BlockSpec((1, 1, d), lambda i, j: (i // tiles_per_seq, 0, 0)),
            pl.BlockSpec((1, 1, d), lambda i, j: (i // tiles_per_seq, 0, 0)),
            pl.BlockSpec((d, tn), lambda i, j: (0, j)),
        ],
        out_specs=pl.BlockSpec((tm, tn), lambda i, j: (i, j)),
        out_shape=jax.ShapeDtypeStruct((n, nout), bf16),
        scratch_shapes=[pltpu.VMEM((tm, d), bf16)],
        compiler_params=_cparams(("parallel", "arbitrary")),
        name="norm_in_proj",
    )(x, g.reshape(1, d), sc, sh, w)


def _ab_mix_kernel(x_ref, p_ref, g1_ref, lng_ref, lnb_ref, sw_ref, sbt_ref, pw_ref, ps_ref, wo_ref,
                   o_ref, ext_ref, cat_ref):
    s_idx = pl.program_id(1)
    ts = x_ref.shape[0]
    wa = lng_ref.shape[1]
    hd = wa // A_HEADS
    gb = pw_ref.shape[1]
    blk = SGU_BLOCK

    @pl.when(s_idx == 0)
    def _():
        ext_ref[0:blk, :] = jnp.zeros((blk, ext_ref.shape[1]), bf16)

    @pl.when(s_idx > 0)
    def _():
        ext_ref[0:blk, :] = ext_ref[ts:ts + blk, :]

    ext_ref[blk:, :] = p_ref[:, 2 * wa:]

    ci = lax.broadcasted_iota(i32, (blk, blk), 0) // CHUNK
    cj = lax.broadcasted_iota(i32, (blk, blk), 1) // CHUNK
    causal = ci >= cj
    trow = lax.broadcasted_iota(i32, (blk, 2 * blk), 0) + blk
    tcol = lax.broadcasted_iota(i32, (blk, 2 * blk), 1)
    tpos = lax.broadcasted_iota(i32, (blk, 1), 0)

    def block_body(bi, carry):
        r0 = pl.multiple_of(bi * blk, blk)
        rows = pl.ds(r0, blk)
        v = p_ref[rows, wa:2 * wa].astype(f32)
        mu = jnp.mean(v, axis=-1, keepdims=True)
        vc = v - mu
        var = jnp.mean(vc * vc, axis=-1, keepdims=True)
        vn = (vc * lax.rsqrt(var + EPS) * lng_ref[...] + lnb_ref[...]).astype(bf16)
        for h in range(A_HEADS):
            cols = slice(h * hd, (h + 1) * hd)
            wm = jnp.where(causal, sw_ref[h], 0.0).astype(bf16)
            s = jnp.dot(wm, vn[:, cols], preferred_element_type=f32) + sbt_ref[:, h:h + 1]
            u = p_ref[rows, cols].astype(f32)
            cat_ref[rows, cols] = (u * s).astype(bf16)
        win_rows = pl.ds(r0, 2 * blk)
        pos = (s_idx * ts + r0 + tpos + 1).astype(f32)
        for gi, win in enumerate(POOL_WINDOWS):
            cols = slice(gi * gb, (gi + 1) * gb)
            band = jnp.where((tcol <= trow) & (tcol > trow - win), 1.0, 0.0).astype(bf16)
            window = ext_ref[win_rows, cols]
            sums = jnp.dot(band, window, preferred_element_type=f32)
            xb = window[blk:, :].astype(f32)
            pooled = sums / jnp.minimum(pos, float(win)) - xb
            yb = jnp.dot(pooled.astype(bf16), pw_ref[gi], preferred_element_type=f32) * ps_ref[:, cols]
            cat_ref[rows, wa + gi * gb:wa + (gi + 1) * gb] = yb.astype(bf16)
        return carry

    lax.fori_loop(0, ts // blk, block_body, 0)
    mix = jnp.dot(cat_ref[...], wo_ref[...], preferred_element_type=f32)
    o_ref[...] = x_ref[...] + g1_ref[0] * mix


def _ab_mix(x, p, g1, ln_g, ln_b, sgu_w, sgu_b, pool_w, pool_scale, w_out, bsz, seq):
    n, d = x.shape
    wa = ln_g.shape[0]
    wb = pool_scale.shape[0]
    ts = min(TILE_MIX, seq)
    nst = seq // ts
    const2 = lambda b, s: (0, 0)
    const3 = lambda b, s: (0, 0, 0)
    return pl.pallas_call(
        _ab_mix_kernel,
        grid=(bsz, nst),
        in_specs=[
            pl.BlockSpec((ts, d), lambda b, s: (b * nst + s, 0)),
            pl.BlockSpec((ts, p.shape[1]), lambda b, s: (b * nst + s, 0)),
            pl.BlockSpec((1, 1, d), lambda b, s: (b, 0, 0)),
            pl.BlockSpec((1, wa), const2),
            pl.BlockSpec((1, wa), const2),
            pl.BlockSpec(sgu_w.shape, const3),
            pl.BlockSpec((SGU_BLOCK, A_HEADS), const2),
            pl.BlockSpec(pool_w.shape, const3),
            pl.BlockSpec((1, wb), const2),
            pl.BlockSpec(w_out.shape, const2),
        ],
        out_specs=pl.BlockSpec((ts, d), lambda b, s: (b * nst + s, 0)),
        out_shape=jax.ShapeDtypeStruct((n, d), f32),
        scratch_shapes=[pltpu.VMEM((ts + SGU_BLOCK, wb), bf16), pltpu.VMEM((ts, wa + wb), bf16)],
        compiler_params=_cparams(("parallel", "arbitrary")),
        name="mixer_ab",
    )(x, p, g1, ln_g.reshape(1, wa), ln_b.reshape(1, wa), sgu_w, sgu_b.T, pool_w.astype(bf16),
      pool_scale.reshape(1, wb), w_out)


def _cd_mix_kernel(x_ref, p_ref, g1_ref, c3w_ref, c3b_ref, c31w_ref, c31b_ref, lng_ref, lnb_ref, wo_ref,
                   o_ref, e3_ref, e31_ref, d_ref, cat_ref):
    s_idx = pl.program_id(1)
    ts = x_ref.shape[0]
    wc = c3b_ref.shape[1]
    wd = c31b_ref.shape[1]
    h3 = e3_ref.shape[0] - ts
    h31 = e31_ref.shape[0] - ts

    @pl.when(s_idx == 0)
    def _():
        e3_ref[0:h3, :] = jnp.zeros((h3, wc), f32)
        e31_ref[0:h31, :] = jnp.zeros((h31, wd), f32)

    @pl.when(s_idx > 0)
    def _():
        e3_ref[0:h3, :] = e3_ref[ts:ts + h3, :]
        e31_ref[0:h31, :] = e31_ref[ts:ts + h31, :]

    cg = p_ref[:, wc:2 * wc].astype(f32)
    xc = p_ref[:, 2 * wc:3 * wc].astype(f32)
    e3_ref[h3:, :] = cg * xc
    da = p_ref[:, 3 * wc:3 * wc + wd].astype(f32)
    dg = p_ref[:, 3 * wc + wd:].astype(f32)
    e31_ref[h31:, :] = da * jax.nn.sigmoid(dg)

    cw = CONV_COLS
    rr = CONV_ROWS
    sub = V7X_SUBLANES

    def causal_conv(w_ref, b_ref, e_ref, halo, r0, cols):
        ntaps = w_ref.shape[0]
        acc = jnp.broadcast_to(b_ref[:, cols], (rr, cw))
        for b in range(sub):
            taps = [k for k in range(ntaps) if (halo - (ntaps - 1) + k) % sub == b]
            if not taps:
                continue
            nload = rr + sub if b else rr
            z = None
            for k in taps:
                a8 = (halo - (ntaps - 1) + k) // sub * sub
                t = w_ref[k:k + 1, cols] * e_ref[pl.ds(r0 + a8, nload), cols]
                z = t if z is None else z + t
            if b:
                z = pltpu.roll(z, nload - b, 0)[:rr]
            acc = acc + z
        return acc

    def rows_body(ri, carry):
        r0 = pl.multiple_of(ri * rr, rr)
        rows = pl.ds(r0, rr)
        for c in range(wc // cw):
            cols = slice(c * cw, (c + 1) * cw)
            acc = causal_conv(c3w_ref, c3b_ref, e3_ref, h3, r0, cols)
            cat_ref[rows, cols] = (p_ref[rows, cols].astype(f32) * acc).astype(bf16)
        for c in range(wd // cw):
            cols = slice(c * cw, (c + 1) * cw)
            d_ref[rows, cols] = causal_conv(c31w_ref, c31b_ref, e31_ref, h31, r0, cols)
        for q in range(rr // LN_ROWS):
            qrows = pl.ds(r0 + q * LN_ROWS, LN_ROWS)
            dv = d_ref[qrows, :]
            mu = jnp.mean(dv, axis=-1, keepdims=True)
            dc = dv - mu
            var = jnp.mean(dc * dc, axis=-1, keepdims=True)
            dn = dc * lax.rsqrt(var + EPS) * lng_ref[...] + lnb_ref[...]
            cat_ref[qrows, wc:] = jax.nn.silu(dn).astype(bf16)
        return carry

    lax.fori_loop(0, ts // rr, rows_body, 0)
    mix = jnp.dot(cat_ref[...], wo_ref[...], preferred_element_type=f32)
    o_ref[...] = x_ref[...] + g1_ref[0] * mix


def _cd_mix(x, p, g1, c3w, c3b, c31w, c31b, ln_g, ln_b, w_out, bsz, seq):
    n, d = x.shape
    wc = c3b.shape[0]
    wd = c31b.shape[0]
    ts = min(TILE_MIX, seq)
    nst = seq // ts
    h3 = V7X_SUBLANES
    h31 = 4 * V7X_SUBLANES
    assert h3 >= SHORT_CONV - 1 and h31 >= CONF_CONV - 1
    const2 = lambda b, s: (0, 0)
    return pl.pallas_call(
        _cd_mix_kernel,
        grid=(bsz, nst),
        in_specs=[
            pl.BlockSpec((ts, d), lambda b, s: (b * nst + s, 0)),
            pl.BlockSpec((ts, p.shape[1]), lambda b, s: (b * nst + s, 0)),
            pl.BlockSpec((1, 1, d), lambda b, s: (b, 0, 0)),
            pl.BlockSpec(c3w.shape, const2),
            pl.BlockSpec((1, wc), const2),
            pl.BlockSpec(c31w.shape, const2),
            pl.BlockSpec((1, wd), const2),
            pl.BlockSpec((1, wd), const2),
            pl.BlockSpec((1, wd), const2),
            pl.BlockSpec(w_out.shape, const2),
        ],
        out_specs=pl.BlockSpec((ts, d), lambda b, s: (b * nst + s, 0)),
        out_shape=jax.ShapeDtypeStruct((n, d), f32),
        scratch_shapes=[pltpu.VMEM((ts + h3, wc), f32), pltpu.VMEM((ts + h31, wd), f32),
                        pltpu.VMEM((ts, wd), f32), pltpu.VMEM((ts, wc + wd), bf16)],
        compiler_params=_cparams(("parallel", "arbitrary")),
        name="mixer_cd",
    )(x, p, g1, c3w, c3b.reshape(1, wc), c31w, c31b.reshape(1, wd), ln_g.reshape(1, wd), ln_b.reshape(1, wd),
      w_out)


def _route_kernel(x_ref, g_ref, sc_ref, sh_ref, rwh_ref, rwl_ref, rb_ref,
                  hp_ref, eidx_ref, wts_ref, rank_ref, cnt_ref, carry_ref):
    tm = x_ref.shape[0]
    ne = rwh_ref.shape[0]
    gsz = ne // N_GROUPS
    neg = -jnp.inf

    @pl.when(pl.program_id(0) == 0)
    def _():
        carry_ref[...] = jnp.zeros_like(carry_ref)

    h = _modulated_rmsnorm(x_ref[...], g_ref[...], sc_ref[0], sh_ref[0])
    hh = h.astype(bf16)
    hf = hh.astype(f32)
    hl = (h - hf).astype(bf16)
    hp_ref[...] = _rows_to_tiles(_pack_halves(hf))

    nt = (((1,), (1,)), ((), ()))
    logits = (lax.dot_general(rwh_ref[...], hh, nt, preferred_element_type=f32)
              + lax.dot_general(rwh_ref[...], hl, nt, preferred_element_type=f32)
              + lax.dot_general(rwl_ref[...], hh, nt, preferred_element_type=f32))
    scores = jax.nn.sigmoid(logits)
    choice = scores + rb_ref[...]

    c3 = choice.reshape(N_GROUPS, gsz, tm)
    sub = lax.broadcasted_iota(i32, c3.shape, 1)
    m1 = jnp.max(c3, axis=1, keepdims=True)
    first = jnp.min(jnp.where(c3 == m1, sub, gsz), axis=1, keepdims=True)
    m2 = jnp.max(jnp.where(sub == first, neg, c3), axis=1, keepdims=True)
    gs = m1 + m2
    keep = []
    for gi in range(N_GROUPS):
        beaten = jnp.zeros((1, tm), i32)
        for gj in range(N_GROUPS):
            if gj == gi:
                continue
            wins = (gs[gj] > gs[gi]) | ((gs[gj] == gs[gi]) & (gj < gi))
            beaten = beaten + wins.astype(i32)
        keep.append(beaten < TOPK_GROUPS)
    masked = jnp.concatenate(
        [jnp.where(keep[gi], c3[gi], neg) for gi in range(N_GROUPS)], axis=0)

    eio = lax.broadcasted_iota(i32, (ne, tm), 0)
    cur = masked
    onehot = jnp.zeros((ne, tm), f32)
    idxs, sels, svals = [], [], []
    for _ in range(TOP_K):
        m = jnp.max(cur, axis=0, keepdims=True)
        idx = jnp.min(jnp.where(cur == m, eio, ne), axis=0, keepdims=True)
        sel = eio == idx
        svals.append(jnp.sum(jnp.where(sel, scores, 0.0), axis=0, keepdims=True))
        cur = jnp.where(sel, neg, cur)
        onehot = onehot + sel.astype(f32)
        idxs.append(idx)
        sels.append(sel)
    ssum = svals[0]
    for sv in svals[1:]:
        ssum = ssum + sv

    tr = lax.broadcasted_iota(i32, (tm, tm), 0)
    tc = lax.broadcasted_iota(i32, (tm, tm), 1)
    upper = jnp.where(tr < tc, 1.0, 0.0).astype(bf16)
    before = jnp.dot(onehot.astype(bf16), upper, preferred_element_type=f32) + carry_ref[...]
    new_carry = carry_ref[...] + jnp.sum(onehot, axis=1, keepdims=True)
    carry_ref[...] = new_carry
    cnt_ref[...] = jnp.broadcast_to(new_carry, cnt_ref.shape).astype(i32)

    zrow_i = jnp.zeros((1, tm), i32)
    zrow_f = jnp.zeros((1, tm), f32)
    ranks = [jnp.sum(jnp.where(sel, before, 0.0), axis=0, keepdims=True).astype(i32) for sel in sels]
    pad = V7X_SUBLANES - TOP_K
    eidx_ref[...] = jnp.concatenate(idxs + [zrow_i] * pad, axis=0)
    wts_ref[...] = jnp.concatenate([sv / ssum * ROUTED_SCALE for sv in svals] + [zrow_f] * pad, axis=0)
    rank_ref[...] = jnp.concatenate(ranks + [zrow_i] * pad, axis=0)


def _route(x, g, sc, sh, router_w, router_b, seq):
    n, d = x.shape
    ne = router_w.shape[1]
    tm = min(TILE_ROUTE, seq)
    tiles_per_seq = seq // tm
    rwt = router_w.T
    rwh = rwt.astype(bf16)
    rwl = (rwt - rwh.astype(f32)).astype(bf16)
    slots = V7X_SUBLANES
    return pl.pallas_call(
        _route_kernel,
        grid=(n // tm,),
        in_specs=[
            pl.BlockSpec((tm, d), lambda i: (i, 0)),
            pl.BlockSpec((1, d), lambda i: (0, 0)),
            pl.BlockSpec((1, 1, d), lambda i: (i // tiles_per_seq, 0, 0)),
            pl.BlockSpec((1, 1, d), lambda i: (i // tiles_per_seq, 0, 0)),
            pl.BlockSpec((ne, d), lambda i: (0, 0)),
            pl.BlockSpec((ne, d), lambda i: (0, 0)),
            pl.BlockSpec((ne, 1), lambda i: (0, 0)),
        ],
        out_specs=[
            pl.BlockSpec((tm, d // 2 // V7X_LANES, V7X_LANES), lambda i: (i, 0, 0)),
            pl.BlockSpec((slots, tm), lambda i: (0, i)),
            pl.BlockSpec((slots, tm), lambda i: (0, i)),
            pl.BlockSpec((slots, tm), lambda i: (0, i)),
            pl.BlockSpec((ne, V7X_LANES), lambda i: (0, 0)),
        ],
        out_shape=[
            jax.ShapeDtypeStruct((n, d // 2 // V7X_LANES, V7X_LANES), u32),
            jax.ShapeDtypeStruct((slots, n), i32),
            jax.ShapeDtypeStruct((slots, n), f32),
            jax.ShapeDtypeStruct((slots, n), i32),
            jax.ShapeDtypeStruct((ne, V7X_LANES), i32),
        ],
        scratch_shapes=[pltpu.VMEM((ne, 1), f32)],
        compiler_params=_cparams(("arbitrary",)),
        name="route",
    )(x, g.reshape(1, d), sc, sh, rwh, rwl, router_b.reshape(ne, 1))


def _dest_kernel(poffs_ref, eidx_ref, rank_ref, dest_ref):
    eidx = eidx_ref[...]
    base = jnp.zeros(eidx.shape, i32)
    for e in range(N_EXPERTS):
        base = jnp.where(eidx == e, poffs_ref[e], base)
    dest_ref[...] = base + rank_ref[...]


def _dest(poffs, eidx_t, rank_t):
    slots, n = eidx_t.shape
    td = min(4096, n)
    grid_spec = pltpu.PrefetchScalarGridSpec(
        num_scalar_prefetch=1,
        grid=(n // td,),
        in_specs=[pl.BlockSpec((slots, td), lambda i, po: (0, i)),
                  pl.BlockSpec((slots, td), lambda i, po: (0, i))],
        out_specs=pl.BlockSpec((slots, td), lambda i, po: (0, i)),
    )
    return pl.pallas_call(
        _dest_kernel,
        grid_spec=grid_spec,
        out_shape=jax.ShapeDtypeStruct((slots, n), i32),
        compiler_params=_cparams(("parallel",)),
        name="dest_rows",
    )(poffs, eidx_t, rank_t)


def _dispatch_kernel(nu_ref, zb_ref, dest_ref, hp_ref, xb_hbm, zero_ref, zsem, sem):
    i = pl.program_id(0)
    tm = hp_ref.shape[0]
    rows = zero_ref.shape[0]
    nz = zb_ref.shape[0]

    @pl.when(i == 0)
    def _():
        zero_ref[...] = jnp.zeros_like(zero_ref)

        def zcopy(j):
            return pltpu.make_async_copy(zero_ref, xb_hbm.at[pl.ds(zb_ref[j] * rows, rows)], zsem)

        def zstart(j, carry):
            @pl.when(zb_ref[j] >= 0)
            def _():
                zcopy(j).start()
            return carry

        def zwait(j, carry):
            @pl.when(zb_ref[j] >= 0)
            def _():
                zcopy(j).wait()
            return carry

        lax.fori_loop(0, nz, zstart, 0)
        lax.fori_loop(0, nz, zwait, 0)

    def body(t, carry):
        for k in range(TOP_K):
            pltpu.make_async_copy(hp_ref.at[t], xb_hbm.at[dest_ref[0, 0, t * TOP_K + k]],
                                  sem).start(priority=k % 2)
        return carry

    lax.fori_loop(0, tm, body, 0, unroll=2)
    for k in range(TOP_K):
        pltpu.make_async_copy(hp_ref, xb_hbm.at[pl.ds(0, tm)], sem).wait()


def _token_major(a_t, tm):
    n = a_t.shape[1]
    return a_t[:TOP_K].T.reshape(n // tm, 1, tm * TOP_K)


def _dispatch(hp, dest_t, nused, zero_blocks, nb):
    n, s, lanes = hp.shape
    tm = min(TILE_DISPATCH, n)
    grid_spec = pltpu.PrefetchScalarGridSpec(
        num_scalar_prefetch=2,
        grid=(n // tm,),
        in_specs=[
            pl.BlockSpec((1, 1, tm * TOP_K), lambda i, nu, zb: (i, 0, 0), memory_space=pltpu.SMEM),
            pl.BlockSpec((tm, s, lanes), lambda i, nu, zb: (i, 0, 0)),
        ],
        out_specs=pl.BlockSpec(memory_space=pl.ANY),
        scratch_shapes=[pltpu.VMEM((ROW_BLOCK, s, lanes), u32), pltpu.SemaphoreType.DMA(()),
                        pltpu.SemaphoreType.DMA(())],
    )
    return pl.pallas_call(
        _dispatch_kernel,
        grid_spec=grid_spec,
        out_shape=jax.ShapeDtypeStruct((nb * ROW_BLOCK, s, lanes), u32),
        compiler_params=_cparams(("arbitrary",)),
        name="dispatch",
    )(nused, zero_blocks, _token_major(dest_t, tm), hp)


def _expert_kernel(be_ref, nu_ref, eord_ref, elist_ref, nexp_ref, x_ref, w1_hbm, w3_hbm, w2_hbm, y_ref,
                   w1_st, w3_st, w2_st, w13_s, w2_s, wsem, *, layer):
    b = pl.program_id(0)
    nused = nu_ref[0]
    ff = w2_s.shape[0]
    rows = x_ref.shape[0]

    def weight_copies(j, slot):
        e = elist_ref[j]
        return [pltpu.make_async_copy(src.at[layer, e], dst.at[slot], wsem.at[slot])
                for src, dst in ((w1_hbm, w1_st), (w3_hbm, w3_st), (w2_hbm, w2_st))]

    @pl.when(b == 0)
    def _():
        for cp in weight_copies(0, 0):
            cp.start()

    j = eord_ref[b]
    first = (b == 0) | (be_ref[b] != be_ref[jnp.maximum(b - 1, 0)])

    @pl.when(first & (b < nused))
    def _():
        slot = j % 2
        for cp in weight_copies(j, slot):
            cp.wait()

        @pl.when(j + 1 < nexp_ref[0])
        def _():
            for cp in weight_copies(j + 1, 1 - slot):
                cp.start()

        w13_s[:, :ff] = w1_st[slot].astype(bf16)
        w13_s[:, ff:] = w3_st[slot].astype(bf16)
        w2_s[...] = w2_st[slot].astype(bf16)

    @pl.when(b < nused)
    def _():
        cr = rows // EXPERT_CHUNKS
        for c in range(EXPERT_CHUNKS):
            rs = slice(c * cr, (c + 1) * cr)
            lo, hi = _unpack_halves(_tiles_to_rows(x_ref[rs]))
            x = jnp.concatenate([lo, hi], axis=1).astype(bf16)
            hcat = jnp.dot(x, w13_s[...], preferred_element_type=f32)
            a = jax.nn.silu(hcat[:, :ff]) * hcat[:, ff:]
            y = jnp.dot(a.astype(bf16), w2_s[...], preferred_element_type=f32)
            y_ref[rs] = _rows_to_tiles(_pack_halves(y.astype(bf16).astype(f32)))

    @pl.when(b >= nused)
    def _():
        y_ref[...] = jnp.zeros_like(y_ref)


def _experts(xbuf, block_e, nused, expert_ord, expert_list, nexp, w1, w3, w2, layer):
    nrows, s, lanes = xbuf.shape
    nb = block_e.shape[0]
    _, ne, d, ff = w1.shape
    rows = ROW_BLOCK
    grid_spec = pltpu.PrefetchScalarGridSpec(
        num_scalar_prefetch=5,
        grid=(nb,),
        in_specs=[
            pl.BlockSpec((rows, s, lanes), lambda b, be, nu, eo, el, nx: (jnp.minimum(b, nu[0] - 1), 0, 0)),
            pl.BlockSpec(memory_space=pl.ANY),
            pl.BlockSpec(memory_space=pl.ANY),
            pl.BlockSpec(memory_space=pl.ANY),
        ],
        out_specs=pl.BlockSpec((rows, s, lanes), lambda b, be, nu, eo, el, nx: (b, 0, 0)),
        scratch_shapes=[pltpu.VMEM((2, d, ff), f32), pltpu.VMEM((2, d, ff), f32), pltpu.VMEM((2, ff, d), f32),
                        pltpu.VMEM((d, 2 * ff), bf16), pltpu.VMEM((ff, d), bf16),
                        pltpu.SemaphoreType.DMA((2,))],
    )
    return pl.pallas_call(
        functools.partial(_expert_kernel, layer=layer),
        grid_spec=grid_spec,
        out_shape=jax.ShapeDtypeStruct((nrows, s, lanes), u32),
        compiler_params=_cparams(("arbitrary",)),
        name="experts",
    )(block_e, nused, expert_ord, expert_list, nexp, xbuf, w1, w3, w2)


def _combine_kernel(dc_ref, dn_ref, x_ref, hp_ref, w_ref, g2_ref, sw13_ref, sw2_ref, fg_ref, y_hbm,
                    o_ref, buf0_ref, buf1_ref, sem, *, final):
    i = pl.program_id(0)
    nsteps = pl.num_programs(0)
    slot = i % 2
    tm = x_ref.shape[0]
    half = hp_ref.shape[1] * hp_ref.shape[2]
    sff = sw2_ref.shape[0]
    bufs = (buf0_ref, buf1_ref)

    def row_copy(d_ref, s, t, k):
        return pltpu.make_async_copy(y_hbm.at[d_ref[0, 0, t * TOP_K + k]], bufs[s].at[k * tm + t], sem.at[s])

    def drain(s):
        for k in range(TOP_K):
            pltpu.make_async_copy(y_hbm.at[pl.ds(0, tm)], bufs[s].at[pl.ds(k * tm, tm)], sem.at[s]).wait()

    @pl.when(i == 0)
    def _():
        def body(t, carry):
            for k in range(TOP_K):
                row_copy(dc_ref, 0, t, k).start(priority=k % 2)
            return carry

        lax.fori_loop(0, tm, body, 0, unroll=2)

    def step(s):
        drain(s)
        for t in range(tm):
            for k in range(TOP_K):
                row_copy(dn_ref, 1 - s, t, k).start(priority=k % 2)

        lo, hi = _unpack_halves(_tiles_to_rows(hp_ref[...]))
        hx = jnp.concatenate([lo, hi], axis=1).astype(bf16)
        hcat = jnp.dot(hx, sw13_ref[...], preferred_element_type=f32)
        a = jax.nn.silu(hcat[:, :sff]) * hcat[:, sff:]
        shared = jnp.dot(a.astype(bf16), sw2_ref[...], preferred_element_type=f32)

        w = w_ref[...]
        acc_lo = shared[:, :half]
        acc_hi = shared[:, half:]
        for k in range(TOP_K):
            ylo, yhi = _unpack_halves(_tiles_to_rows(bufs[s][k * tm:(k + 1) * tm]))
            acc_lo = acc_lo + w[:, k:k + 1] * ylo
            acc_hi = acc_hi + w[:, k:k + 1] * yhi
        g2 = g2_ref[0]
        out_lo = x_ref[:, :half] + g2[:, :half] * acc_lo
        out_hi = x_ref[:, half:] + g2[:, half:] * acc_hi
        if final:
            ms = (jnp.sum(out_lo * out_lo, axis=-1, keepdims=True)
                  + jnp.sum(out_hi * out_hi, axis=-1, keepdims=True)) / (2 * half)
            inv = lax.rsqrt(ms + EPS)
            out_lo = out_lo * inv * fg_ref[:, :half]
            out_hi = out_hi * inv * fg_ref[:, half:]
        o_ref[:, :half] = out_lo
        o_ref[:, half:] = out_hi

    for s in range(2):
        @pl.when(slot == s)
        def _():
            step(s)

        @pl.when((slot == s) & (i + 1 == nsteps))
        def _():
            drain(1 - s)


def _combine(x, hp, wts, dest_t, g2, sw13, sw2, final_g, ybuf, seq, final):
    n, d = x.shape
    _, s, lanes = hp.shape
    tm = min(TILE_COMB, seq)
    nt = n // tm
    tiles_per_seq = seq // tm
    dest = _token_major(dest_t, tm)
    pairs = tm * TOP_K
    return pl.pallas_call(
        functools.partial(_combine_kernel, final=final),
        grid=(nt,),
        in_specs=[
            pl.BlockSpec((1, 1, pairs), lambda i: (i, 0, 0), memory_space=pltpu.SMEM),
            pl.BlockSpec((1, 1, pairs), lambda i: (jnp.minimum(i + 1, nt - 1), 0, 0), memory_space=pltpu.SMEM),
            pl.BlockSpec((tm, d), lambda i: (i, 0)),
            pl.BlockSpec((tm, s, lanes), lambda i: (i, 0, 0)),
            pl.BlockSpec((tm, V7X_SUBLANES), lambda i: (i, 0)),
            pl.BlockSpec((1, 1, d), lambda i: (i // tiles_per_seq, 0, 0)),
            pl.BlockSpec(sw13.shape, lambda i: (0, 0)),
            pl.BlockSpec(sw2.shape, lambda i: (0, 0)),
            pl.BlockSpec((1, d), lambda i: (0, 0)),
            pl.BlockSpec(memory_space=pl.ANY),
        ],
        out_specs=pl.BlockSpec((tm, d), lambda i: (i, 0)),
        out_shape=jax.ShapeDtypeStruct((n, d), f32),
        scratch_shapes=[pltpu.VMEM((tm * TOP_K, s, lanes), u32), pltpu.VMEM((tm * TOP_K, s, lanes), u32),
                        pltpu.SemaphoreType.DMA((2,))],
        compiler_params=_cparams(("arbitrary",)),
        name="combine",
    )(dest, dest, x, hp, wts.T, g2, sw13, sw2, final_g.reshape(1, d), ybuf)


def _moe(x, g, sc, sh, g2, router_w, router_b, w1, w3, w2, layer, sw1, sw3, sw2, final_g, seq, final):
    n, d = x.shape
    ne = router_w.shape[1]
    hp, eidx_t, wts_t, rank_t, cnt = _route(x, g, sc, sh, router_w, router_b, seq)
    counts = cnt[:, 0]
    pcounts = (counts + ROW_BLOCK - 1) // ROW_BLOCK * ROW_BLOCK
    pends = jnp.cumsum(pcounts)
    poffs = pends - pcounts
    nb = -(-(n * TOP_K) // ROW_BLOCK) + ne
    starts = jnp.arange(nb, dtype=i32) * ROW_BLOCK
    block_e = jnp.minimum(jnp.sum((pends[None, :] <= starts[:, None]).astype(i32), axis=1), ne - 1)
    nused = (pends[-1:] // ROW_BLOCK).astype(i32)
    last_blocks = jnp.where(pcounts > 0, pends // ROW_BLOCK - 1, -1)
    tail = nused[0] + jnp.arange(nb - (n * TOP_K) // ROW_BLOCK, dtype=i32)
    zero_blocks = jnp.concatenate([last_blocks, jnp.where(tail < nb, tail, -1)]).astype(i32)
    dest_t = _dest(poffs.astype(i32), eidx_t, rank_t)
    xbuf = _dispatch(hp, dest_t, nused, zero_blocks, nb)
    has_rows = pcounts > 0
    expert_list = jnp.nonzero(has_rows, size=ne, fill_value=ne - 1)[0].astype(i32)
    expert_ord = (jnp.cumsum(has_rows.astype(i32)) - 1)[block_e].astype(i32)
    nexp = jnp.sum(has_rows.astype(i32)).reshape(1)
    ybuf = _experts(xbuf, block_e, nused, expert_ord, expert_list, nexp, w1, w3, w2, layer)
    sw13 = jnp.concatenate([sw1, sw3], axis=-1).astype(bf16)
    return _combine(x, hp, wts_t, dest_t, g2, sw13, sw2.astype(bf16), final_g, ybuf, seq, final)


def kernel(x, c, ada_w, ada_b, norm_mix_g, norm_ffn_g, ab_w_in, sgu_ln_g, sgu_ln_b, sgu_w, sgu_b, pool_w,
           pool_scale, ab_w_out, cd_w_in, conv3_w, conv3_b, conv31_w, conv31_b, cd_ln_g, cd_ln_b, cd_w_out,
           router_w, router_b, exp_w1, exp_w3, exp_w2, sh_w1, sh_w3, sh_w2, final_g):
    bsz, seq, d = x.shape
    depth = ada_w.shape[0]
    n = bsz * seq
    xf = x.reshape(n, d)
    mod = _ada(c, ada_w, ada_b)
    for l in range(depth):
        sh1, sc1, g1, sh2, sc2, g2 = [mod[l, :, i * d:(i + 1) * d].reshape(bsz, 1, d) for i in range(6)]
        if l % 2 == 0:
            e = l // 2
            p = _norm_mm(xf, norm_mix_g[l], sc1, sh1, ab_w_in[e].astype(bf16), seq)
            xf = _ab_mix(xf, p, g1, sgu_ln_g[e], sgu_ln_b[e], sgu_w[e], sgu_b[e], pool_w[e], pool_scale[e],
                         ab_w_out[e].astype(bf16), bsz, seq)
        else:
            o = l // 2
            p = _norm_mm(xf, norm_mix_g[l], sc1, sh1, cd_w_in[o].astype(bf16), seq)
            xf = _cd_mix(xf, p, g1, conv3_w[o], conv3_b[o], conv31_w[o], conv31_b[o], cd_ln_g[o], cd_ln_b[o],
                         cd_w_out[o].astype(bf16), bsz, seq)
        xf = _moe(xf, norm_ffn_g[l], sc2, sh2, g2, router_w[l], router_b[l], exp_w1, exp_w3, exp_w2, l,
                  sh_w1[l], sh_w3[l], sh_w2[l], final_g, seq, final=(l == depth - 1))
    return xf.reshape(bsz, seq, d)
```

```python
import functools

import jax
import jax.numpy as jnp
from jax import lax
from jax.experimental import pallas as pl
from jax.experimental.pallas import tpu as pltpu
from jax.experimental.pallas import tpu_sc as plsc

f32 = jnp.float32
bf16 = jnp.bfloat16
u32 = jnp.uint32
i32 = jnp.int32

EPS = 1e-6
CHUNK = 64
SGU_BLOCK = 128
A_HEADS = 8
POOL_WINDOWS = (2, 4, 8, 16)
SHORT_CONV = 3
CONF_CONV = 31
N_EXPERTS = 64
TOP_K = 6
N_GROUPS = 8
TOPK_GROUPS = 4
ROUTED_SCALE = 2.5

V7X_LANES = 128
V7X_SUBLANES = 8
V7X_VMEM_LIMIT = 56 * 1024 * 1024

ROW_BLOCK = 512
TILE_MM = 1024
TILE_MIX = 512
TILE_ROUTE = 512
SC_SCATTER_WIDTH = 128
SC_SCATTER_WINDOW = 128
EXPERT_CHUNKS = 4
TILE_COMB = 256
CONV_ROWS = 128
CONV_COLS = 128
LN_ROWS = 32
NORM_ROWS = 16
HI_MASK = 0xFFFF0000


def _cparams(sem):
    return pltpu.CompilerParams(dimension_semantics=sem, vmem_limit_bytes=V7X_VMEM_LIMIT)


def _pack_halves(v):
    w = v.shape[1] // 2
    bits = pltpu.bitcast(v, u32)
    return (bits[:, :w] >> 16) | bits[:, w:]


def _unpack_halves(words):
    lo = pltpu.bitcast(words << 16, f32)
    hi = pltpu.bitcast(words & jnp.uint32(HI_MASK), f32)
    return lo, hi


def _rows_to_tiles(m):
    s = m.shape[1] // V7X_LANES
    st = jnp.stack([m[:, i * V7X_LANES:(i + 1) * V7X_LANES] for i in range(s)], axis=0)
    return pltpu.einshape("srl->rsl", st)


def _tiles_to_rows(t3):
    xt = pltpu.einshape("rsl->srl", t3)
    return jnp.concatenate([xt[i] for i in range(t3.shape[1])], axis=1)


def _modulated_rmsnorm(x, g, sc, sh):
    ms = jnp.mean(x * x, axis=-1, keepdims=True)
    y = x * lax.rsqrt(ms + EPS) * g
    return y * (1.0 + sc) + sh


def _ada_kernel(c_ref, w_ref, b_ref, o_ref):
    ca = jax.nn.silu(c_ref[...]).astype(bf16)
    o_ref[0] = jnp.dot(ca, w_ref[0].astype(bf16), preferred_element_type=f32) + b_ref[0]


def _ada(c, ada_w, ada_b):
    depth, d, n6 = ada_w.shape
    bsz = c.shape[0]
    tn = 1024
    return pl.pallas_call(
        _ada_kernel,
        grid=(depth, n6 // tn),
        in_specs=[
            pl.BlockSpec((bsz, d), lambda l, j: (0, 0)),
            pl.BlockSpec((1, d, tn), lambda l, j: (l, 0, j)),
            pl.BlockSpec((1, 1, tn), lambda l, j: (l, 0, j)),
        ],
        out_specs=pl.BlockSpec((1, bsz, tn), lambda l, j: (l, 0, j)),
        out_shape=jax.ShapeDtypeStruct((depth, bsz, n6), f32),
        compiler_params=_cparams(("parallel", "parallel")),
        name="ada_mod",
    )(c, ada_w, ada_b.reshape(depth, 1, n6))


def _norm_mm_kernel(x_ref, g_ref, sc_ref, sh_ref, w_ref, o_ref, h_ref):
    @pl.when(pl.program_id(1) == 0)
    def _():
        def body(r, carry):
            rows = pl.ds(pl.multiple_of(r * NORM_ROWS, NORM_ROWS), NORM_ROWS)
            h_ref[rows, :] = _modulated_rmsnorm(x_ref[rows, :], g_ref[...], sc_ref[0], sh_ref[0]).astype(bf16)
            return carry

        lax.fori_loop(0, x_ref.shape[0] // NORM_ROWS, body, 0, unroll=8)

    o_ref[...] = jnp.dot(h_ref[...], w_ref[...], preferred_element_type=f32).astype(o_ref.dtype)


def _norm_mm(x, g, sc, sh, w, seq):
    n, d = x.shape
    nout = w.shape[1]
    tm = min(TILE_MM, seq)
    tn = 1024
    tiles_per_seq = seq // tm
    return pl.pallas_call(
        _norm_mm_kernel,
        grid=(n // tm, nout // tn),
        in_specs=[
            pl.BlockSpec((tm, d), lambda i, j: (i, 0)),
            pl.BlockSpec((1, d), lambda i, j: (0, 0)),
            pl.BlockSpec((1, 1, d), lambda i, j: (i // tiles_per_seq, 0, 0)),
            pl.BlockSpec((1, 1, d), lambda i, j: (i // tiles_per_seq, 0, 0)),
            pl.BlockSpec((d, tn), lambda i, j: (0, j)),
        ],
        out_specs=pl.BlockSpec((tm, tn), lambda i, j: (i, j)),
        out_shape=jax.ShapeDtypeStruct((n, nout), bf16),
        scratch_shapes=[pltpu.VMEM((tm, d), bf16)],
        compiler_params=_cparams(("parallel", "arbitrary")),
        name="norm_in_proj",
    )(x, g.reshape(1, d), sc, sh, w)


def _ab_mix_kernel(x_ref, p_ref, g1_ref, lng_ref, lnb_ref, sw_ref, sbt_ref, pw_ref, ps_ref, wo_ref,
                   o_ref, ext_ref, cat_ref):
    s_idx = pl.program_id(1)
    ts = x_ref.shape[0]
    wa = lng_ref.shape[1]
    hd = wa // A_HEADS
    gb = pw_ref.shape[1]
    blk = SGU_BLOCK

    @pl.when(s_idx == 0)
    def _():
        ext_ref[0:blk, :] = jnp.zeros((blk, ext_ref.shape[1]), bf16)

    @pl.when(s_idx > 0)
    def _():
        ext_ref[0:blk, :] = ext_ref[ts:ts + blk, :]

    ext_ref[blk:, :] = p_ref[:, 2 * wa:]

    ci = lax.broadcasted_iota(i32, (blk, blk), 0) // CHUNK
    cj = lax.broadcasted_iota(i32, (blk, blk), 1) // CHUNK
    causal = ci >= cj
    trow = lax.broadcasted_iota(i32, (blk, 2 * blk), 0) + blk
    tcol = lax.broadcasted_iota(i32, (blk, 2 * blk), 1)
    tpos = lax.broadcasted_iota(i32, (blk, 1), 0)

    def block_body(bi, carry):
        r0 = pl.multiple_of(bi * blk, blk)
        rows = pl.ds(r0, blk)
        v = p_ref[rows, wa:2 * wa].astype(f32)
        mu = jnp.mean(v, axis=-1, keepdims=True)
        vc = v - mu
        var = jnp.mean(vc * vc, axis=-1, keepdims=True)
        vn = (vc * lax.rsqrt(var + EPS) * lng_ref[...] + lnb_ref[...]).astype(bf16)
        for h in range(A_HEADS):
            cols = slice(h * hd, (h + 1) * hd)
            wm = jnp.where(causal, sw_ref[h], 0.0).astype(bf16)
            s = jnp.dot(wm, vn[:, cols], preferred_element_type=f32) + sbt_ref[:, h:h + 1]
            u = p_ref[rows, cols].astype(f32)
            cat_ref[rows, cols] = (u * s).astype(bf16)
        win_rows = pl.ds(r0, 2 * blk)
        pos = (s_idx * ts + r0 + tpos + 1).astype(f32)
        for gi, win in enumerate(POOL_WINDOWS):
            cols = slice(gi * gb, (gi + 1) * gb)
            band = jnp.where((tcol <= trow) & (tcol > trow - win), 1.0, 0.0).astype(bf16)
            window = ext_ref[win_rows, cols]
            sums = jnp.dot(band, window, preferred_element_type=f32)
            xb = window[blk:, :].astype(f32)
            pooled = sums / jnp.minimum(pos, float(win)) - xb
            yb = jnp.dot(pooled.astype(bf16), pw_ref[gi], preferred_element_type=f32) * ps_ref[:, cols]
            cat_ref[rows, wa + gi * gb:wa + (gi + 1) * gb] = yb.astype(bf16)
        return carry

    lax.fori_loop(0, ts // blk, block_body, 0)
    mix = jnp.dot(cat_ref[...], wo_ref[...], preferred_element_type=f32)
    o_ref[...] = x_ref[...] + g1_ref[0] * mix


def _ab_mix(x, p, g1, ln_g, ln_b, sgu_w, sgu_b, pool_w, pool_scale, w_out, bsz, seq):
    n, d = x.shape
    wa = ln_g.shape[0]
    wb = pool_scale.shape[0]
    ts = min(TILE_MIX, seq)
    nst = seq // ts
    const2 = lambda b, s: (0, 0)
    const3 = lambda b, s: (0, 0, 0)
    return pl.pallas_call(
        _ab_mix_kernel,
        grid=(bsz, nst),
        in_specs=[
            pl.BlockSpec((ts, d), lambda b, s: (b * nst + s, 0)),
            pl.BlockSpec((ts, p.shape[1]), lambda b, s: (b * nst + s, 0)),
            pl.BlockSpec((1, 1, d), lambda b, s: (b, 0, 0)),
            pl.BlockSpec((1, wa), const2),
            pl.BlockSpec((1, wa), const2),
            pl.BlockSpec(sgu_w.shape, const3),
            pl.BlockSpec((SGU_BLOCK, A_HEADS), const2),
            pl.BlockSpec(pool_w.shape, const3),
            pl.BlockSpec((1, wb), const2),
            pl.BlockSpec(w_out.shape, const2),
        ],
        out_specs=pl.BlockSpec((ts, d), lambda b, s: (b * nst + s, 0)),
        out_shape=jax.ShapeDtypeStruct((n, d), f32),
        scratch_shapes=[pltpu.VMEM((ts + SGU_BLOCK, wb), bf16), pltpu.VMEM((ts, wa + wb), bf16)],
        compiler_params=_cparams(("parallel", "arbitrary")),
        name="mixer_ab",
    )(x, p, g1, ln_g.reshape(1, wa), ln_b.reshape(1, wa), sgu_w, sgu_b.T, pool_w.astype(bf16),
      pool_scale.reshape(1, wb), w_out)


def _cd_mix_kernel(x_ref, p_ref, g1_ref, c3w_ref, c3b_ref, c31w_ref, c31b_ref, lng_ref, lnb_ref, wo_ref,
                   o_ref, e3_ref, e31_ref, d_ref, cat_ref):
    s_idx = pl.program_id(1)
    ts = x_ref.shape[0]
    wc = c3b_ref.shape[1]
    wd = c31b_ref.shape[1]
    h3 = e3_ref.shape[0] - ts
    h31 = e31_ref.shape[0] - ts

    @pl.when(s_idx == 0)
    def _():
        e3_ref[0:h3, :] = jnp.zeros((h3, wc), f32)
        e31_ref[0:h31, :] = jnp.zeros((h31, wd), f32)

    @pl.when(s_idx > 0)
    def _():
        e3_ref[0:h3, :] = e3_ref[ts:ts + h3, :]
        e31_ref[0:h31, :] = e31_ref[ts:ts + h31, :]

    cg = p_ref[:, wc:2 * wc].astype(f32)
    xc = p_ref[:, 2 * wc:3 * wc].astype(f32)
    e3_ref[h3:, :] = cg * xc
    da = p_ref[:, 3 * wc:3 * wc + wd].astype(f32)
    dg = p_ref[:, 3 * wc + wd:].astype(f32)
    e31_ref[h31:, :] = da * jax.nn.sigmoid(dg)

    cw = CONV_COLS
    rr = CONV_ROWS
    sub = V7X_SUBLANES

    def causal_conv(w_ref, b_ref, e_ref, halo, r0, cols):
        ntaps = w_ref.shape[0]
        acc = jnp.broadcast_to(b_ref[:, cols], (rr, cw))
        for b in range(sub):
            taps = [k for k in range(ntaps) if (halo - (ntaps - 1) + k) % sub == b]
            if not taps:
                continue
            nload = rr + sub if b else rr
            z = None
            for k in taps:
                a8 = (halo - (ntaps - 1) + k) // sub * sub
                t = w_ref[k:k + 1, cols] * e_ref[pl.ds(r0 + a8, nload), cols]
                z = t if z is None else z + t
            if b:
                z = pltpu.roll(z, nload - b, 0)[:rr]
            acc = acc + z
        return acc

    def rows_body(ri, carry):
        r0 = pl.multiple_of(ri * rr, rr)
        rows = pl.ds(r0, rr)
        for c in range(wc // cw):
            cols = slice(c * cw, (c + 1) * cw)
            acc = causal_conv(c3w_ref, c3b_ref, e3_ref, h3, r0, cols)
            cat_ref[rows, cols] = (p_ref[rows, cols].astype(f32) * acc).astype(bf16)
        for c in range(wd // cw):
            cols = slice(c * cw, (c + 1) * cw)
            d_ref[rows, cols] = causal_conv(c31w_ref, c31b_ref, e31_ref, h31, r0, cols)
        for q in range(rr // LN_ROWS):
            qrows = pl.ds(r0 + q * LN_ROWS, LN_ROWS)
            dv = d_ref[qrows, :]
            mu = jnp.mean(dv, axis=-1, keepdims=True)
            dc = dv - mu
            var = jnp.mean(dc * dc, axis=-1, keepdims=True)
            dn = dc * lax.rsqrt(var + EPS) * lng_ref[...] + lnb_ref[...]
            cat_ref[qrows, wc:] = jax.nn.silu(dn).astype(bf16)
        return carry

    lax.fori_loop(0, ts // rr, rows_body, 0)
    mix = jnp.dot(cat_ref[...], wo_ref[...], preferred_element_type=f32)
    o_ref[...] = x_ref[...] + g1_ref[0] * mix


def _cd_mix(x, p, g1, c3w, c3b, c31w, c31b, ln_g, ln_b, w_out, bsz, seq):
    n, d = x.shape
    wc = c3b.shape[0]
    wd = c31b.shape[0]
    ts = min(TILE_MIX, seq)
    nst = seq // ts
    h3 = V7X_SUBLANES
    h31 = 4 * V7X_SUBLANES
    assert h3 >= SHORT_CONV - 1 and h31 >= CONF_CONV - 1
    const2 = lambda b, s: (0, 0)
    return pl.pallas_call(
        _cd_mix_kernel,
        grid=(bsz, nst),
        in_specs=[
            pl.BlockSpec((ts, d), lambda b, s: (b * nst + s, 0)),
            pl.BlockSpec((ts, p.shape[1]), lambda b, s: (b * nst + s, 0)),
            pl.BlockSpec((1, 1, d), lambda b, s: (b, 0, 0)),
            pl.BlockSpec(c3w.shape, const2),
            pl.BlockSpec((1, wc), const2),
            pl.BlockSpec(c31w.shape, const2),
            pl.BlockSpec((1, wd), const2),
            pl.BlockSpec((1, wd), const2),
            pl.BlockSpec((1, wd), const2),
            pl.BlockSpec(w_out.shape, const2),
        ],
        out_specs=pl.BlockSpec((ts, d), lambda b, s: (b * nst + s, 0)),
        out_shape=jax.ShapeDtypeStruct((n, d), f32),
        scratch_shapes=[pltpu.VMEM((ts + h3, wc), f32), pltpu.VMEM((ts + h31, wd), f32),
                        pltpu.VMEM((ts, wd), f32), pltpu.VMEM((ts, wc + wd), bf16)],
        compiler_params=_cparams(("parallel", "arbitrary")),
        name="mixer_cd",
    )(x, p, g1, c3w, c3b.reshape(1, wc), c31w, c31b.reshape(1, wd), ln_g.reshape(1, wd), ln_b.reshape(1, wd),
      w_out)


def _route_kernel(x_ref, g_ref, sc_ref, sh_ref, rwh_ref, rwl_ref, rb_ref,
                  hp_ref, eidx_ref, wts_ref, rank_ref, cnt_ref, carry_ref):
    tm = x_ref.shape[0]
    ne = rwh_ref.shape[0]
    gsz = ne // N_GROUPS
    neg = -jnp.inf

    @pl.when(pl.program_id(0) == 0)
    def _():
        carry_ref[...] = jnp.zeros_like(carry_ref)

    h = _modulated_rmsnorm(x_ref[...], g_ref[...], sc_ref[0], sh_ref[0])
    hh = h.astype(bf16)
    hf = hh.astype(f32)
    hl = (h - hf).astype(bf16)
    hp_ref[...] = _rows_to_tiles(_pack_halves(hf))

    nt = (((1,), (1,)), ((), ()))
    logits = (lax.dot_general(rwh_ref[...], hh, nt, preferred_element_type=f32)
              + lax.dot_general(rwh_ref[...], hl, nt, preferred_element_type=f32)
              + lax.dot_general(rwl_ref[...], hh, nt, preferred_element_type=f32))
    scores = jax.nn.sigmoid(logits)
    choice = scores + rb_ref[...]

    c3 = choice.reshape(N_GROUPS, gsz, tm)
    sub = lax.broadcasted_iota(i32, c3.shape, 1)
    m1 = jnp.max(c3, axis=1, keepdims=True)
    first = jnp.min(jnp.where(c3 == m1, sub, gsz), axis=1, keepdims=True)
    m2 = jnp.max(jnp.where(sub == first, neg, c3), axis=1, keepdims=True)
    gs = m1 + m2
    keep = []
    for gi in range(N_GROUPS):
        beaten = jnp.zeros((1, tm), i32)
        for gj in range(N_GROUPS):
            if gj == gi:
                continue
            wins = (gs[gj] > gs[gi]) | ((gs[gj] == gs[gi]) & (gj < gi))
            beaten = beaten + wins.astype(i32)
        keep.append(beaten < TOPK_GROUPS)
    masked = jnp.concatenate(
        [jnp.where(keep[gi], c3[gi], neg) for gi in range(N_GROUPS)], axis=0)

    eio = lax.broadcasted_iota(i32, (ne, tm), 0)
    cur = masked
    onehot = jnp.zeros((ne, tm), f32)
    idxs, sels, svals = [], [], []
    for _ in range(TOP_K):
        m = jnp.max(cur, axis=0, keepdims=True)
        idx = jnp.min(jnp.where(cur == m, eio, ne), axis=0, keepdims=True)
        sel = eio == idx
        svals.append(jnp.sum(jnp.where(sel, scores, 0.0), axis=0, keepdims=True))
        cur = jnp.where(sel, neg, cur)
        onehot = onehot + sel.astype(f32)
        idxs.append(idx)
        sels.append(sel)
    ssum = svals[0]
    for sv in svals[1:]:
        ssum = ssum + sv

    tr = lax.broadcasted_iota(i32, (tm, tm), 0)
    tc = lax.broadcasted_iota(i32, (tm, tm), 1)
    upper = jnp.where(tr < tc, 1.0, 0.0).astype(bf16)
    before = jnp.dot(onehot.astype(bf16), upper, preferred_element_type=f32) + carry_ref[...]
    new_carry = carry_ref[...] + jnp.sum(onehot, axis=1, keepdims=True)
    carry_ref[...] = new_carry
    cnt_ref[...] = jnp.broadcast_to(new_carry, cnt_ref.shape).astype(i32)

    zrow_i = jnp.zeros((1, tm), i32)
    zrow_f = jnp.zeros((1, tm), f32)
    ranks = [jnp.sum(jnp.where(sel, before, 0.0), axis=0, keepdims=True).astype(i32) for sel in sels]
    pad = V7X_SUBLANES - TOP_K
    eidx_ref[...] = jnp.concatenate(idxs + [zrow_i] * pad, axis=0)
    wts_ref[...] = jnp.concatenate([sv / ssum * ROUTED_SCALE for sv in svals] + [zrow_f] * pad, axis=0)
    rank_ref[...] = jnp.concatenate(ranks + [zrow_i] * pad, axis=0)


def _route(x, g, sc, sh, router_w, router_b, seq):
    n, d = x.shape
    ne = router_w.shape[1]
    tm = min(TILE_ROUTE, seq)
    tiles_per_seq = seq // tm
    rwt = router_w.T
    rwh = rwt.astype(bf16)
    rwl = (rwt - rwh.astype(f32)).astype(bf16)
    slots = V7X_SUBLANES
    return pl.pallas_call(
        _route_kernel,
        grid=(n // tm,),
        in_specs=[
            pl.BlockSpec((tm, d), lambda i: (i, 0)),
            pl.BlockSpec((1, d), lambda i: (0, 0)),
            pl.BlockSpec((1, 1, d), lambda i: (i // tiles_per_seq, 0, 0)),
            pl.BlockSpec((1, 1, d), lambda i: (i // tiles_per_seq, 0, 0)),
            pl.BlockSpec((ne, d), lambda i: (0, 0)),
            pl.BlockSpec((ne, d), lambda i: (0, 0)),
            pl.BlockSpec((ne, 1), lambda i: (0, 0)),
        ],
        out_specs=[
            pl.BlockSpec((tm, d // 2 // V7X_LANES, V7X_LANES), lambda i: (i, 0, 0)),
            pl.BlockSpec((slots, tm), lambda i: (0, i)),
            pl.BlockSpec((slots, tm), lambda i: (0, i)),
            pl.BlockSpec((slots, tm), lambda i: (0, i)),
            pl.BlockSpec((ne, V7X_LANES), lambda i: (0, 0)),
        ],
        out_shape=[
            jax.ShapeDtypeStruct((n, d // 2 // V7X_LANES, V7X_LANES), u32),
            jax.ShapeDtypeStruct((slots, n), i32),
            jax.ShapeDtypeStruct((slots, n), f32),
            jax.ShapeDtypeStruct((slots, n), i32),
            jax.ShapeDtypeStruct((ne, V7X_LANES), i32),
        ],
        scratch_shapes=[pltpu.VMEM((ne, 1), f32)],
        compiler_params=_cparams(("arbitrary",)),
        name="route",
    )(x, g.reshape(1, d), sc, sh, rwh, rwl, router_b.reshape(ne, 1))


def _dest_kernel(poffs_ref, eidx_ref, rank_ref, dest_ref):
    eidx = eidx_ref[...]
    base = jnp.zeros(eidx.shape, i32)
    for e in range(N_EXPERTS):
        base = jnp.where(eidx == e, poffs_ref[e], base)
    dest_ref[...] = base + rank_ref[...]


def _dest(poffs, eidx_t, rank_t):
    slots, n = eidx_t.shape
    td = min(4096, n)
    grid_spec = pltpu.PrefetchScalarGridSpec(
        num_scalar_prefetch=1,
        grid=(n // td,),
        in_specs=[pl.BlockSpec((slots, td), lambda i, po: (0, i)),
                  pl.BlockSpec((slots, td), lambda i, po: (0, i))],
        out_specs=pl.BlockSpec((slots, td), lambda i, po: (0, i)),
    )
    return pl.pallas_call(
        _dest_kernel,
        grid_spec=grid_spec,
        out_shape=jax.ShapeDtypeStruct((slots, n), i32),
        compiler_params=_cparams(("parallel",)),
        name="dest_rows",
    )(poffs, eidx_t, rank_t)


def _scatter_pair_ids(dest_flat, nrows):
    npairs = dest_flat.shape[0]
    ids = jnp.broadcast_to(jnp.arange(npairs, dtype=i32)[:, None], (npairs, SC_SCATTER_WIDTH))
    mesh = plsc.VectorSubcoreMesh(core_axis_name="core", subcore_axis_name="subcore")

    @pl.kernel(out_type=jax.ShapeDtypeStruct((nrows, SC_SCATTER_WIDTH), i32), mesh=mesh, scratch_types=[])
    def scatter(ids_hbm, idx_hbm, out_hbm):
        def body(ids_vmem, idx_vmem):
            pltpu.sync_copy(ids_vmem, out_hbm.at[idx_vmem.at[0]])

        pltpu.emit_pipeline(
            body,
            grid=(npairs // SC_SCATTER_WINDOW,),
            in_specs=[pl.BlockSpec((SC_SCATTER_WINDOW, SC_SCATTER_WIDTH), index_map=lambda i: (i, 0)),
                      pl.BlockSpec((1, SC_SCATTER_WINDOW), index_map=lambda i: (0, i))],
            out_specs=[],
            core_axis_name=("core", "subcore"),
            dimension_semantics=(pltpu.PARALLEL,),
        )(ids_hbm, idx_hbm)

    return scatter(ids, dest_flat.reshape(1, npairs))[:, 0]


def _token_major(a_t, tm):
    n = a_t.shape[1]
    return a_t[:TOP_K].T.reshape(n // tm, 1, tm * TOP_K)


def _expert_kernel(be_ref, nu_ref, eord_ref, elist_ref, nexp_ref, tokc_ref, tokn_ref, hp_hbm, w1_hbm, w3_hbm,
                   w2_hbm, y_ref, xg0_ref, xg1_ref, w1_st, w3_st, w2_st, w13_s, w2_s, gsem, wsem, *, layer):
    b = pl.program_id(0)
    nused = nu_ref[0]
    ff = w2_s.shape[0]
    rows = y_ref.shape[0]
    xgs = (xg0_ref, xg1_ref)

    def row_copy(tok_ref, s, r):
        return pltpu.make_async_copy(hp_hbm.at[tok_ref[0, 0, r]], xgs[s].at[r], gsem.at[s])

    def drain_rows(s):
        pltpu.make_async_copy(hp_hbm.at[pl.ds(0, rows)], xgs[s], gsem.at[s]).wait()

    @pl.when(b == 0)
    def _():
        def body(r, carry):
            row_copy(tokc_ref, 0, r).start(priority=0)
            return carry

        lax.fori_loop(0, rows, body, 0, unroll=8)

    def weight_copies(j, slot):
        e = elist_ref[j]
        return [pltpu.make_async_copy(src.at[layer, e], dst.at[slot], wsem.at[slot])
                for src, dst in ((w1_hbm, w1_st), (w3_hbm, w3_st), (w2_hbm, w2_st))]

    @pl.when(b == 0)
    def _():
        for cp in weight_copies(0, 0):
            cp.start()

    j = eord_ref[b]
    first = (b == 0) | (be_ref[b] != be_ref[jnp.maximum(b - 1, 0)])

    @pl.when(first & (b < nused))
    def _():
        slot = j % 2
        for cp in weight_copies(j, slot):
            cp.wait()

        @pl.when(j + 1 < nexp_ref[0])
        def _():
            for cp in weight_copies(j + 1, 1 - slot):
                cp.start()

        w13_s[:, :ff] = w1_st[slot].astype(bf16)
        w13_s[:, ff:] = w3_st[slot].astype(bf16)
        w2_s[...] = w2_st[slot].astype(bf16)

    def step(s):
        drain_rows(s)
        for r in range(rows):
            row_copy(tokn_ref, 1 - s, r).start(priority=r % 2)
        cr = rows // EXPERT_CHUNKS
        for c in range(EXPERT_CHUNKS):
            rs = slice(c * cr, (c + 1) * cr)
            lo, hi = _unpack_halves(_tiles_to_rows(xgs[s][rs]))
            x = jnp.concatenate([lo, hi], axis=1).astype(bf16)
            hcat = jnp.dot(x, w13_s[...], preferred_element_type=f32)
            a = jax.nn.silu(hcat[:, :ff]) * hcat[:, ff:]
            y = jnp.dot(a.astype(bf16), w2_s[...], preferred_element_type=f32)
            y_ref[rs] = _rows_to_tiles(_pack_halves(y.astype(bf16).astype(f32)))

    for s in range(2):
        @pl.when((b < nused) & (b % 2 == s))
        def _():
            step(s)

        @pl.when((b == nused) & (b % 2 == s))
        def _():
            drain_rows(s)

    @pl.when(b >= nused)
    def _():
        y_ref[...] = jnp.zeros_like(y_ref)


def _experts(hp, row_tok, block_e, nused, expert_ord, expert_list, nexp, w1, w3, w2, layer):
    n, s, lanes = hp.shape
    nb = block_e.shape[0]
    _, ne, d, ff = w1.shape
    rows = ROW_BLOCK
    tok3 = row_tok.reshape(nb, 1, rows)
    grid_spec = pltpu.PrefetchScalarGridSpec(
        num_scalar_prefetch=5,
        grid=(nb,),
        in_specs=[
            pl.BlockSpec((1, 1, rows), lambda b, be, nu, eo, el, nx: (b, 0, 0), memory_space=pltpu.SMEM),
            pl.BlockSpec((1, 1, rows), lambda b, be, nu, eo, el, nx: (jnp.minimum(b + 1, nb - 1), 0, 0),
                         memory_space=pltpu.SMEM),
            pl.BlockSpec(memory_space=pl.ANY),
            pl.BlockSpec(memory_space=pl.ANY),
            pl.BlockSpec(memory_space=pl.ANY),
            pl.BlockSpec(memory_space=pl.ANY),
        ],
        out_specs=pl.BlockSpec((rows, s, lanes), lambda b, be, nu, eo, el, nx: (b, 0, 0)),
        scratch_shapes=[pltpu.VMEM((rows, s, lanes), u32), pltpu.VMEM((rows, s, lanes), u32),
                        pltpu.VMEM((2, d, ff), f32), pltpu.VMEM((2, d, ff), f32), pltpu.VMEM((2, ff, d), f32),
                        pltpu.VMEM((d, 2 * ff), bf16), pltpu.VMEM((ff, d), bf16),
                        pltpu.SemaphoreType.DMA((2,)), pltpu.SemaphoreType.DMA((2,))],
    )
    return pl.pallas_call(
        functools.partial(_expert_kernel, layer=layer),
        grid_spec=grid_spec,
        out_shape=jax.ShapeDtypeStruct((nb * rows, s, lanes), u32),
        compiler_params=_cparams(("arbitrary",)),
        name="experts",
    )(block_e, nused, expert_ord, expert_list, nexp, tok3, tok3, hp, w1, w3, w2)


def _combine_kernel(dc_ref, dn_ref, x_ref, hp_ref, w_ref, g2_ref, sw13_ref, sw2_ref, fg_ref, y_hbm,
                    o_ref, buf0_ref, buf1_ref, sem, *, final):
    i = pl.program_id(0)
    nsteps = pl.num_programs(0)
    slot = i % 2
    tm = x_ref.shape[0]
    half = hp_ref.shape[1] * hp_ref.shape[2]
    sff = sw2_ref.shape[0]
    bufs = (buf0_ref, buf1_ref)

    def row_copy(d_ref, s, t, k):
        return pltpu.make_async_copy(y_hbm.at[d_ref[0, 0, t * TOP_K + k]], bufs[s].at[k * tm + t], sem.at[s])

    def drain(s):
        for k in range(TOP_K):
            pltpu.make_async_copy(y_hbm.at[pl.ds(0, tm)], bufs[s].at[pl.ds(k * tm, tm)], sem.at[s]).wait()

    @pl.when(i == 0)
    def _():
        def body(t, carry):
            for k in range(TOP_K):
                row_copy(dc_ref, 0, t, k).start(priority=k % 2)
            return carry

        lax.fori_loop(0, tm, body, 0, unroll=2)

    def step(s):
        drain(s)
        for t in range(tm):
            for k in range(TOP_K):
                row_copy(dn_ref, 1 - s, t, k).start(priority=k % 2)

        lo, hi = _unpack_halves(_tiles_to_rows(hp_ref[...]))
        hx = jnp.concatenate([lo, hi], axis=1).astype(bf16)
        hcat = jnp.dot(hx, sw13_ref[...], preferred_element_type=f32)
        a = jax.nn.silu(hcat[:, :sff]) * hcat[:, sff:]
        shared = jnp.dot(a.astype(bf16), sw2_ref[...], preferred_element_type=f32)

        w = w_ref[...]
        acc_lo = shared[:, :half]
        acc_hi = shared[:, half:]
        for k in range(TOP_K):
            ylo, yhi = _unpack_halves(_tiles_to_rows(bufs[s][k * tm:(k + 1) * tm]))
            acc_lo = acc_lo + w[:, k:k + 1] * ylo
            acc_hi = acc_hi + w[:, k:k + 1] * yhi
        g2 = g2_ref[0]
        out_lo = x_ref[:, :half] + g2[:, :half] * acc_lo
        out_hi = x_ref[:, half:] + g2[:, half:] * acc_hi
        if final:
            ms = (jnp.sum(out_lo * out_lo, axis=-1, keepdims=True)
                  + jnp.sum(out_hi * out_hi, axis=-1, keepdims=True)) / (2 * half)
            inv = lax.rsqrt(ms + EPS)
            out_lo = out_lo * inv * fg_ref[:, :half]
            out_hi = out_hi * inv * fg_ref[:, half:]
        o_ref[:, :half] = out_lo
        o_ref[:, half:] = out_hi

    for s in range(2):
        @pl.when(slot == s)
        def _():
            step(s)

        @pl.when((slot == s) & (i + 1 == nsteps))
        def _():
            drain(1 - s)


def _combine(x, hp, wts, dest_t, g2, sw13, sw2, final_g, ybuf, seq, final):
    n, d = x.shape
    _, s, lanes = hp.shape
    tm = min(TILE_COMB, seq)
    nt = n // tm
    tiles_per_seq = seq // tm
    dest = _token_major(dest_t, tm)
    pairs = tm * TOP_K
    return pl.pallas_call(
        functools.partial(_combine_kernel, final=final),
        grid=(nt,),
        in_specs=[
            pl.BlockSpec((1, 1, pairs), lambda i: (i, 0, 0), memory_space=pltpu.SMEM),
            pl.BlockSpec((1, 1, pairs), lambda i: (jnp.minimum(i + 1, nt - 1), 0, 0), memory_space=pltpu.SMEM),
            pl.BlockSpec((tm, d), lambda i: (i, 0)),
            pl.BlockSpec((tm, s, lanes), lambda i: (i, 0, 0)),
            pl.BlockSpec((tm, V7X_SUBLANES), lambda i: (i, 0)),
            pl.BlockSpec((1, 1, d), lambda i: (i // tiles_per_seq, 0, 0)),
            pl.BlockSpec(sw13.shape, lambda i: (0, 0)),
            pl.BlockSpec(sw2.shape, lambda i: (0, 0)),
            pl.BlockSpec((1, d), lambda i: (0, 0)),
            pl.BlockSpec(memory_space=pl.ANY),
        ],
        out_specs=pl.BlockSpec((tm, d), lambda i: (i, 0)),
        out_shape=jax.ShapeDtypeStruct((n, d), f32),
        scratch_shapes=[pltpu.VMEM((tm * TOP_K, s, lanes), u32), pltpu.VMEM((tm * TOP_K, s, lanes), u32),
                        pltpu.SemaphoreType.DMA((2,))],
        compiler_params=_cparams(("arbitrary",)),
        name="combine",
    )(dest, dest, x, hp, wts.T, g2, sw13, sw2, final_g.reshape(1, d), ybuf)


def _moe(x, g, sc, sh, g2, router_w, router_b, w1, w3, w2, layer, sw1, sw3, sw2, final_g, seq, final):
    n, d = x.shape
    ne = router_w.shape[1]
    hp, eidx_t, wts_t, rank_t, cnt = _route(x, g, sc, sh, router_w, router_b, seq)
    counts = cnt[:, 0]
    pcounts = (counts + ROW_BLOCK - 1) // ROW_BLOCK * ROW_BLOCK
    pends = jnp.cumsum(pcounts)
    poffs = pends - pcounts
    nb = -(-(n * TOP_K) // ROW_BLOCK) + ne + 1
    starts = jnp.arange(nb, dtype=i32) * ROW_BLOCK
    block_e = jnp.minimum(jnp.sum((pends[None, :] <= starts[:, None]).astype(i32), axis=1), ne - 1)
    nused = (pends[-1:] // ROW_BLOCK).astype(i32)
    dest_t = _dest(poffs.astype(i32), eidx_t, rank_t)
    pair_of_row = _scatter_pair_ids(dest_t[:TOP_K].reshape(-1), nb * ROW_BLOCK).reshape(nb, ROW_BLOCK)
    in_expert = starts[:, None] + jnp.arange(ROW_BLOCK, dtype=i32)[None, :] - poffs[block_e][:, None]
    valid = (in_expert < counts[block_e][:, None]) & (starts < pends[-1])[:, None]
    row_tok = jnp.where(valid, pair_of_row % n, 0).astype(i32)
    has_rows = pcounts > 0
    expert_list = jnp.nonzero(has_rows, size=ne, fill_value=ne - 1)[0].astype(i32)
    expert_ord = (jnp.cumsum(has_rows.astype(i32)) - 1)[block_e].astype(i32)
    nexp = jnp.sum(has_rows.astype(i32)).reshape(1)
    ybuf = _experts(hp, row_tok, block_e, nused, expert_ord, expert_list, nexp, w1, w3, w2, layer)
    sw13 = jnp.concatenate([sw1, sw3], axis=-1).astype(bf16)
    return _combine(x, hp, wts_t, dest_t, g2, sw13, sw2.astype(bf16), final_g, ybuf, seq, final)


def kernel(x, c, ada_w, ada_b, norm_mix_g, norm_ffn_g, ab_w_in, sgu_ln_g, sgu_ln_b, sgu_w, sgu_b, pool_w,
           pool_scale, ab_w_out, cd_w_in, conv3_w, conv3_b, conv31_w, conv31_b, cd_ln_g, cd_ln_b, cd_w_out,
           router_w, router_b, exp_w1, exp_w3, exp_w2, sh_w1, sh_w3, sh_w2, final_g):
    bsz, seq, d = x.shape
    depth = ada_w.shape[0]
    n = bsz * seq
    xf = x.reshape(n, d)
    mod = _ada(c, ada_w, ada_b)
    for l in range(depth):
        sh1, sc1, g1, sh2, sc2, g2 = [mod[l, :, i * d:(i + 1) * d].reshape(bsz, 1, d) for i in range(6)]
        if l % 2 == 0:
            e = l // 2
            p = _norm_mm(xf, norm_mix_g[l], sc1, sh1, ab_w_in[e].astype(bf16), seq)
            xf = _ab_mix(xf, p, g1, sgu_ln_g[e], sgu_ln_b[e], sgu_w[e], sgu_b[e], pool_w[e], pool_scale[e],
                         ab_w_out[e].astype(bf16), bsz, seq)
        else:
            o = l // 2
            p = _norm_mm(xf, norm_mix_g[l], sc1, sh1, cd_w_in[o].astype(bf16), seq)
            xf = _cd_mix(xf, p, g1, conv3_w[o], conv3_b[o], conv31_w[o], conv31_b[o], cd_ln_g[o], cd_ln_b[o],
                         cd_w_out[o].astype(bf16), bsz, seq)
        xf = _moe(xf, norm_ffn_g[l], sc2, sh2, g2, router_w[l], router_b[l], exp_w1, exp_w3, exp_w2, l,
                  sh_w1[l], sh_w3[l], sh_w2[l], final_g, seq, final=(l == depth - 1))
    return xf.reshape(bsz, seq, d)
```

```python
import functools

import jax
import jax.numpy as jnp
from jax import lax
from jax.experimental import pallas as pl
from jax.experimental.pallas import tpu as pltpu
from jax.experimental.pallas import tpu_sc as plsc

f32 = jnp.float32
bf16 = jnp.bfloat16
u32 = jnp.uint32
i32 = jnp.int32

EPS = 1e-6
CHUNK = 64
SGU_BLOCK = 128
A_HEADS = 8
POOL_WINDOWS = (2, 4, 8, 16)
SHORT_CONV = 3
CONF_CONV = 31
N_EXPERTS = 64
TOP_K = 6
N_GROUPS = 8
TOPK_GROUPS = 4
ROUTED_SCALE = 2.5

V7X_LANES = 128
V7X_SUBLANES = 8
V7X_VMEM_LIMIT = 56 * 1024 * 1024

ROW_BLOCK = 512
TILE_MM = 1024
TILE_MIX = 512
TILE_ROUTE = 512
SC_SCATTER_WIDTH = 128
SC_SCATTER_WINDOW = 128
EXPERT_ISSUE_GROUPS = 2
EXPERT_CHUNKS = 4
TILE_COMB = 256
CONV_ROWS = 128
CONV_COLS = 128
LN_ROWS = 32
NORM_ROWS = 16
HI_MASK = 0xFFFF0000


def _cparams(sem):
    return pltpu.CompilerParams(dimension_semantics=sem, vmem_limit_bytes=V7X_VMEM_LIMIT)


def _pack_halves(v):
    w = v.shape[1] // 2
    bits = pltpu.bitcast(v, u32)
    return (bits[:, :w] >> 16) | bits[:, w:]


def _unpack_halves(words):
    lo = pltpu.bitcast(words << 16, f32)
    hi = pltpu.bitcast(words & jnp.uint32(HI_MASK), f32)
    return lo, hi


def _rows_to_tiles(m):
    s = m.shape[1] // V7X_LANES
    st = jnp.stack([m[:, i * V7X_LANES:(i + 1) * V7X_LANES] for i in range(s)], axis=0)
    return pltpu.einshape("srl->rsl", st)


def _tiles_to_rows(t3):
    xt = pltpu.einshape("rsl->srl", t3)
    return jnp.concatenate([xt[i] for i in range(t3.shape[1])], axis=1)


def _modulated_rmsnorm(x, g, sc, sh):
    ms = jnp.mean(x * x, axis=-1, keepdims=True)
    y = x * lax.rsqrt(ms + EPS) * g
    return y * (1.0 + sc) + sh


def _ada_kernel(c_ref, w_ref, b_ref, o_ref):
    ca = jax.nn.silu(c_ref[...]).astype(bf16)
    o_ref[0] = jnp.dot(ca, w_ref[0].astype(bf16), preferred_element_type=f32) + b_ref[0]


def _ada(c, ada_w, ada_b):
    depth, d, n6 = ada_w.shape
    bsz = c.shape[0]
    tn = 1024
    return pl.pallas_call(
        _ada_kernel,
        grid=(depth, n6 // tn),
        in_specs=[
            pl.BlockSpec((bsz, d), lambda l, j: (0, 0)),
            pl.BlockSpec((1, d, tn), lambda l, j: (l, 0, j)),
            pl.BlockSpec((1, 1, tn), lambda l, j: (l, 0, j)),
        ],
        out_specs=pl.BlockSpec((1, bsz, tn), lambda l, j: (l, 0, j)),
        out_shape=jax.ShapeDtypeStruct((depth, bsz, n6), f32),
        compiler_params=_cparams(("parallel", "parallel")),
        name="ada_mod",
    )(c, ada_w, ada_b.reshape(depth, 1, n6))


def _norm_mm_kernel(x_ref, g_ref, sc_ref, sh_ref, w_ref, o_ref, h_ref):
    @pl.when(pl.program_id(1) == 0)
    def _():
        def body(r, carry):
            rows = pl.ds(pl.multiple_of(r * NORM_ROWS, NORM_ROWS), NORM_ROWS)
            h_ref[rows, :] = _modulated_rmsnorm(x_ref[rows, :], g_ref[...], sc_ref[0], sh_ref[0]).astype(bf16)
            return carry

        lax.fori_loop(0, x_ref.shape[0] // NORM_ROWS, body, 0, unroll=8)

    o_ref[...] = jnp.dot(h_ref[...], w_ref[...], preferred_element_type=f32).astype(o_ref.dtype)


def _norm_mm(x, g, sc, sh, w, seq):
    n, d = x.shape
    nout = w.shape[1]
    tm = min(TILE_MM, seq)
    tn = 1024
    tiles_per_seq = seq // tm
    return pl.pallas_call(
        _norm_mm_kernel,
        grid=(n // tm, nout // tn),
        in_specs=[
            pl.BlockSpec((tm, d), lambda i, j: (i, 0)),
            pl.BlockSpec((1, d), lambda i, j: (0, 0)),
            pl.BlockSpec((1, 1, d), lambda i, j: (i // tiles_per_seq, 0, 0)),
            pl.BlockSpec((1, 1, d), lambda i, j: (i // tiles_per_seq, 0, 0)),
            pl.BlockSpec((d, tn), lambda i, j: (0, j)),
        ],
        out_specs=pl.BlockSpec((tm, tn), lambda i, j: (i, j)),
        out_shape=jax.ShapeDtypeStruct((n, nout), bf16),
        scratch_shapes=[pltpu.VMEM((tm, d), bf16)],
        compiler_params=_cparams(("parallel", "arbitrary")),
        name="norm_in_proj",
    )(x, g.reshape(1, d), sc, sh, w)


def _ab_mix_kernel(x_ref, p_ref, g1_ref, lng_ref, lnb_ref, sw_ref, sbt_ref, pw_ref, ps_ref, wo_ref,
                   o_ref, ext_ref, cat_ref):
    s_idx = pl.program_id(1)
    ts = x_ref.shape[0]
    wa = lng_ref.shape[1]
    hd = wa // A_HEADS
    gb = pw_ref.shape[1]
    blk = SGU_BLOCK

    @pl.when(s_idx == 0)
    def _():
        ext_ref[0:blk, :] = jnp.zeros((blk, ext_ref.shape[1]), bf16)

    @pl.when(s_idx > 0)
    def _():
        ext_ref[0:blk, :] = ext_ref[ts:ts + blk, :]

    ext_ref[blk:, :] = p_ref[:, 2 * wa:]

    ci = lax.broadcasted_iota(i32, (blk, blk), 0) // CHUNK
    cj = lax.broadcasted_iota(i32, (blk, blk), 1) // CHUNK
    causal = ci >= cj
    trow = lax.broadcasted_iota(i32, (blk, 2 * blk), 0) + blk
    tcol = lax.broadcasted_iota(i32, (blk, 2 * blk), 1)
    tpos = lax.broadcasted_iota(i32, (blk, 1), 0)

    def block_body(bi, carry):
        r0 = pl.multiple_of(bi * blk, blk)
        rows = pl.ds(r0, blk)
        v = p_ref[rows, wa:2 * wa].astype(f32)
        mu = jnp.mean(v, axis=-1, keepdims=True)
        vc = v - mu
        var = jnp.mean(vc * vc, axis=-1, keepdims=True)
        vn = (vc * lax.rsqrt(var + EPS) * lng_ref[...] + lnb_ref[...]).astype(bf16)
        for h in range(A_HEADS):
            cols = slice(h * hd, (h + 1) * hd)
            wm = jnp.where(causal, sw_ref[h], 0.0).astype(bf16)
            s = jnp.dot(wm, vn[:, cols], preferred_element_type=f32) + sbt_ref[:, h:h + 1]
            u = p_ref[rows, cols].astype(f32)
            cat_ref[rows, cols] = (u * s).astype(bf16)
        win_rows = pl.ds(r0, 2 * blk)
        pos = (s_idx * ts + r0 + tpos + 1).astype(f32)
        for gi, win in enumerate(POOL_WINDOWS):
            cols = slice(gi * gb, (gi + 1) * gb)
            band = jnp.where((tcol <= trow) & (tcol > trow - win), 1.0, 0.0).astype(bf16)
            window = ext_ref[win_rows, cols]
            sums = jnp.dot(band, window, preferred_element_type=f32)
            xb = window[blk:, :].astype(f32)
            pooled = sums / jnp.minimum(pos, float(win)) - xb
            yb = jnp.dot(pooled.astype(bf16), pw_ref[gi], preferred_element_type=f32) * ps_ref[:, cols]
            cat_ref[rows, wa + gi * gb:wa + (gi + 1) * gb] = yb.astype(bf16)
        return carry

    lax.fori_loop(0, ts // blk, block_body, 0)
    mix = jnp.dot(cat_ref[...], wo_ref[...], preferred_element_type=f32)
    o_ref[...] = x_ref[...] + g1_ref[0] * mix


def _ab_mix(x, p, g1, ln_g, ln_b, sgu_w, sgu_b, pool_w, pool_scale, w_out, bsz, seq):
    n, d = x.shape
    wa = ln_g.shape[0]
    wb = pool_scale.shape[0]
    ts = min(TILE_MIX, seq)
    nst = seq // ts
    const2 = lambda b, s: (0, 0)
    const3 = lambda b, s: (0, 0, 0)
    return pl.pallas_call(
        _ab_mix_kernel,
        grid=(bsz, nst),
        in_specs=[
            pl.BlockSpec((ts, d), lambda b, s: (b * nst + s, 0)),
            pl.BlockSpec((ts, p.shape[1]), lambda b, s: (b * nst + s, 0)),
            pl.BlockSpec((1, 1, d), lambda b, s: (b, 0, 0)),
            pl.BlockSpec((1, wa), const2),
            pl.BlockSpec((1, wa), const2),
            pl.BlockSpec(sgu_w.shape, const3),
            pl.BlockSpec((SGU_BLOCK, A_HEADS), const2),
            pl.BlockSpec(pool_w.shape, const3),
            pl.BlockSpec((1, wb), const2),
            pl.BlockSpec(w_out.shape, const2),
        ],
        out_specs=pl.BlockSpec((ts, d), lambda b, s: (b * nst + s, 0)),
        out_shape=jax.ShapeDtypeStruct((n, d), f32),
        scratch_shapes=[pltpu.VMEM((ts + SGU_BLOCK, wb), bf16), pltpu.VMEM((ts, wa + wb), bf16)],
        compiler_params=_cparams(("parallel", "arbitrary")),
        name="mixer_ab",
    )(x, p, g1, ln_g.reshape(1, wa), ln_b.reshape(1, wa), sgu_w, sgu_b.T, pool_w.astype(bf16),
      pool_scale.reshape(1, wb), w_out)


def _cd_mix_kernel(x_ref, p_ref, g1_ref, c3w_ref, c3b_ref, c31w_ref, c31b_ref, lng_ref, lnb_ref, wo_ref,
                   o_ref, e3_ref, e31_ref, d_ref, cat_ref):
    s_idx = pl.program_id(1)
    ts = x_ref.shape[0]
    wc = c3b_ref.shape[1]
    wd = c31b_ref.shape[1]
    h3 = e3_ref.shape[0] - ts
    h31 = e31_ref.shape[0] - ts

    @pl.when(s_idx == 0)
    def _():
        e3_ref[0:h3, :] = jnp.zeros((h3, wc), f32)
        e31_ref[0:h31, :] = jnp.zeros((h31, wd), f32)

    @pl.when(s_idx > 0)
    def _():
        e3_ref[0:h3, :] = e3_ref[ts:ts + h3, :]
        e31_ref[0:h31, :] = e31_ref[ts:ts + h31, :]

    cg = p_ref[:, wc:2 * wc].astype(f32)
    xc = p_ref[:, 2 * wc:3 * wc].astype(f32)
    e3_ref[h3:, :] = cg * xc
    da = p_ref[:, 3 * wc:3 * wc + wd].astype(f32)
    dg = p_ref[:, 3 * wc + wd:].astype(f32)
    e31_ref[h31:, :] = da * jax.nn.sigmoid(dg)

    cw = CONV_COLS
    rr = CONV_ROWS
    sub = V7X_SUBLANES

    def causal_conv(w_ref, b_ref, e_ref, halo, r0, cols):
        ntaps = w_ref.shape[0]
        acc = jnp.broadcast_to(b_ref[:, cols], (rr, cw))
        for b in range(sub):
            taps = [k for k in range(ntaps) if (halo - (ntaps - 1) + k) % sub == b]
            if not taps:
                continue
            nload = rr + sub if b else rr
            z = None
            for k in taps:
                a8 = (halo - (ntaps - 1) + k) // sub * sub
                t = w_ref[k:k + 1, cols] * e_ref[pl.ds(r0 + a8, nload), cols]
                z = t if z is None else z + t
            if b:
                z = pltpu.roll(z, nload - b, 0)[:rr]
            acc = acc + z
        return acc

    def rows_body(ri, carry):
        r0 = pl.multiple_of(ri * rr, rr)
        rows = pl.ds(r0, rr)
        for c in range(wc // cw):
            cols = slice(c * cw, (c + 1) * cw)
            acc = causal_conv(c3w_ref, c3b_ref, e3_ref, h3, r0, cols)
            cat_ref[rows, cols] = (p_ref[rows, cols].astype(f32) * acc).astype(bf16)
        for c in range(wd // cw):
            cols = slice(c * cw, (c + 1) * cw)
            d_ref[rows, cols] = causal_conv(c31w_ref, c31b_ref, e31_ref, h31, r0, cols)
        for q in range(rr // LN_ROWS):
            qrows = pl.ds(r0 + q * LN_ROWS, LN_ROWS)
            dv = d_ref[qrows, :]
            mu = jnp.mean(dv, axis=-1, keepdims=True)
            dc = dv - mu
            var = jnp.mean(dc * dc, axis=-1, keepdims=True)
            dn = dc * lax.rsqrt(var + EPS) * lng_ref[...] + lnb_ref[...]
            cat_ref[qrows, wc:] = jax.nn.silu(dn).astype(bf16)
        return carry

    lax.fori_loop(0, ts // rr, rows_body, 0)
    mix = jnp.dot(cat_ref[...], wo_ref[...], preferred_element_type=f32)
    o_ref[...] = x_ref[...] + g1_ref[0] * mix


def _cd_mix(x, p, g1, c3w, c3b, c31w, c31b, ln_g, ln_b, w_out, bsz, seq):
    n, d = x.shape
    wc = c3b.shape[0]
    wd = c31b.shape[0]
    ts = min(TILE_MIX, seq)
    nst = seq // ts
    h3 = V7X_SUBLANES
    h31 = 4 * V7X_SUBLANES
    assert h3 >= SHORT_CONV - 1 and h31 >= CONF_CONV - 1
    const2 = lambda b, s: (0, 0)
    return pl.pallas_call(
        _cd_mix_kernel,
        grid=(bsz, nst),
        in_specs=[
            pl.BlockSpec((ts, d), lambda b, s: (b * nst + s, 0)),
            pl.BlockSpec((ts, p.shape[1]), lambda b, s: (b * nst + s, 0)),
            pl.BlockSpec((1, 1, d), lambda b, s: (b, 0, 0)),
            pl.BlockSpec(c3w.shape, const2),
            pl.BlockSpec((1, wc), const2),
            pl.BlockSpec(c31w.shape, const2),
            pl.BlockSpec((1, wd), const2),
            pl.BlockSpec((1, wd), const2),
            pl.BlockSpec((1, wd), const2),
            pl.BlockSpec(w_out.shape, const2),
        ],
        out_specs=pl.BlockSpec((ts, d), lambda b, s: (b * nst + s, 0)),
        out_shape=jax.ShapeDtypeStruct((n, d), f32),
        scratch_shapes=[pltpu.VMEM((ts + h3, wc), f32), pltpu.VMEM((ts + h31, wd), f32),
                        pltpu.VMEM((ts, wd), f32), pltpu.VMEM((ts, wc + wd), bf16)],
        compiler_params=_cparams(("parallel", "arbitrary")),
        name="mixer_cd",
    )(x, p, g1, c3w, c3b.reshape(1, wc), c31w, c31b.reshape(1, wd), ln_g.reshape(1, wd), ln_b.reshape(1, wd),
      w_out)


def _route_kernel(x_ref, g_ref, sc_ref, sh_ref, rwh_ref, rwl_ref, rb_ref,
                  hp_ref, eidx_ref, wts_ref, rank_ref, cnt_ref, carry_ref):
    tm = x_ref.shape[0]
    ne = rwh_ref.shape[0]
    gsz = ne // N_GROUPS
    neg = -jnp.inf

    @pl.when(pl.program_id(0) == 0)
    def _():
        carry_ref[...] = jnp.zeros_like(carry_ref)

    h = _modulated_rmsnorm(x_ref[...], g_ref[...], sc_ref[0], sh_ref[0])
    hh = h.astype(bf16)
    hf = hh.astype(f32)
    hl = (h - hf).astype(bf16)
    hp_ref[...] = _rows_to_tiles(_pack_halves(hf))

    nt = (((1,), (1,)), ((), ()))
    logits = (lax.dot_general(rwh_ref[...], hh, nt, preferred_element_type=f32)
              + lax.dot_general(rwh_ref[...], hl, nt, preferred_element_type=f32)
              + lax.dot_general(rwl_ref[...], hh, nt, preferred_element_type=f32))
    scores = jax.nn.sigmoid(logits)
    choice = scores + rb_ref[...]

    c3 = choice.reshape(N_GROUPS, gsz, tm)
    sub = lax.broadcasted_iota(i32, c3.shape, 1)
    m1 = jnp.max(c3, axis=1, keepdims=True)
    first = jnp.min(jnp.where(c3 == m1, sub, gsz), axis=1, keepdims=True)
    m2 = jnp.max(jnp.where(sub == first, neg, c3), axis=1, keepdims=True)
    gs = m1 + m2
    keep = []
    for gi in range(N_GROUPS):
        beaten = jnp.zeros((1, tm), i32)
        for gj in range(N_GROUPS):
            if gj == gi:
                continue
            wins = (gs[gj] > gs[gi]) | ((gs[gj] == gs[gi]) & (gj < gi))
            beaten = beaten + wins.astype(i32)
        keep.append(beaten < TOPK_GROUPS)
    masked = jnp.concatenate(
        [jnp.where(keep[gi], c3[gi], neg) for gi in range(N_GROUPS)], axis=0)

    eio = lax.broadcasted_iota(i32, (ne, tm), 0)
    cur = masked
    onehot = jnp.zeros((ne, tm), f32)
    idxs, sels, svals = [], [], []
    for _ in range(TOP_K):
        m = jnp.max(cur, axis=0, keepdims=True)
        idx = jnp.min(jnp.where(cur == m, eio, ne), axis=0, keepdims=True)
        sel = eio == idx
        svals.append(jnp.sum(jnp.where(sel, scores, 0.0), axis=0, keepdims=True))
        cur = jnp.where(sel, neg, cur)
        onehot = onehot + sel.astype(f32)
        idxs.append(idx)
        sels.append(sel)
    ssum = svals[0]
    for sv in svals[1:]:
        ssum = ssum + sv

    tr = lax.broadcasted_iota(i32, (tm, tm), 0)
    tc = lax.broadcasted_iota(i32, (tm, tm), 1)
    upper = jnp.where(tr < tc, 1.0, 0.0).astype(bf16)
    before = jnp.dot(onehot.astype(bf16), upper, preferred_element_type=f32) + carry_ref[...]
    new_carry = carry_ref[...] + jnp.sum(onehot, axis=1, keepdims=True)
    carry_ref[...] = new_carry
    cnt_ref[...] = jnp.broadcast_to(new_carry, cnt_ref.shape).astype(i32)

    zrow_i = jnp.zeros((1, tm), i32)
    zrow_f = jnp.zeros((1, tm), f32)
    ranks = [jnp.sum(jnp.where(sel, before, 0.0), axis=0, keepdims=True).astype(i32) for sel in sels]
    pad = V7X_SUBLANES - TOP_K
    eidx_ref[...] = jnp.concatenate(idxs + [zrow_i] * pad, axis=0)
    wts_ref[...] = jnp.concatenate([sv / ssum * ROUTED_SCALE for sv in svals] + [zrow_f] * pad, axis=0)
    rank_ref[...] = jnp.concatenate(ranks + [zrow_i] * pad, axis=0)


def _route(x, g, sc, sh, router_w, router_b, seq):
    n, d = x.shape
    ne = router_w.shape[1]
    tm = min(TILE_ROUTE, seq)
    tiles_per_seq = seq // tm
    rwt = router_w.T
    rwh = rwt.astype(bf16)
    rwl = (rwt - rwh.astype(f32)).astype(bf16)
    slots = V7X_SUBLANES
    return pl.pallas_call(
        _route_kernel,
        grid=(n // tm,),
        in_specs=[
            pl.BlockSpec((tm, d), lambda i: (i, 0)),
            pl.BlockSpec((1, d), lambda i: (0, 0)),
            pl.BlockSpec((1, 1, d), lambda i: (i // tiles_per_seq, 0, 0)),
            pl.BlockSpec((1, 1, d), lambda i: (i // tiles_per_seq, 0, 0)),
            pl.BlockSpec((ne, d), lambda i: (0, 0)),
            pl.BlockSpec((ne, d), lambda i: (0, 0)),
            pl.BlockSpec((ne, 1), lambda i: (0, 0)),
        ],
        out_specs=[
            pl.BlockSpec((tm, d // 2 // V7X_LANES, V7X_LANES), lambda i: (i, 0, 0)),
            pl.BlockSpec((slots, tm), lambda i: (0, i)),
            pl.BlockSpec((slots, tm), lambda i: (0, i)),
            pl.BlockSpec((slots, tm), lambda i: (0, i)),
            pl.BlockSpec((ne, V7X_LANES), lambda i: (0, 0)),
        ],
        out_shape=[
            jax.ShapeDtypeStruct((n, d // 2 // V7X_LANES, V7X_LANES), u32),
            jax.ShapeDtypeStruct((slots, n), i32),
            jax.ShapeDtypeStruct((slots, n), f32),
            jax.ShapeDtypeStruct((slots, n), i32),
            jax.ShapeDtypeStruct((ne, V7X_LANES), i32),
        ],
        scratch_shapes=[pltpu.VMEM((ne, 1), f32)],
        compiler_params=_cparams(("arbitrary",)),
        name="route",
    )(x, g.reshape(1, d), sc, sh, rwh, rwl, router_b.reshape(ne, 1))


def _dest_kernel(poffs_ref, eidx_ref, rank_ref, dest_ref):
    eidx = eidx_ref[...]
    base = jnp.zeros(eidx.shape, i32)
    for e in range(N_EXPERTS):
        base = jnp.where(eidx == e, poffs_ref[e], base)
    dest_ref[...] = base + rank_ref[...]


def _dest(poffs, eidx_t, rank_t):
    slots, n = eidx_t.shape
    td = min(4096, n)
    grid_spec = pltpu.PrefetchScalarGridSpec(
        num_scalar_prefetch=1,
        grid=(n // td,),
        in_specs=[pl.BlockSpec((slots, td), lambda i, po: (0, i)),
                  pl.BlockSpec((slots, td), lambda i, po: (0, i))],
        out_specs=pl.BlockSpec((slots, td), lambda i, po: (0, i)),
    )
    return pl.pallas_call(
        _dest_kernel,
        grid_spec=grid_spec,
        out_shape=jax.ShapeDtypeStruct((slots, n), i32),
        compiler_params=_cparams(("parallel",)),
        name="dest_rows",
    )(poffs, eidx_t, rank_t)


def _scatter_pair_ids(dest_flat, nrows):
    npairs = dest_flat.shape[0]
    ids = jnp.broadcast_to(jnp.arange(npairs, dtype=i32)[:, None], (npairs, SC_SCATTER_WIDTH))
    mesh = plsc.VectorSubcoreMesh(core_axis_name="core", subcore_axis_name="subcore")

    @pl.kernel(out_type=jax.ShapeDtypeStruct((nrows, SC_SCATTER_WIDTH), i32), mesh=mesh, scratch_types=[])
    def scatter(ids_hbm, idx_hbm, out_hbm):
        def body(ids_vmem, idx_vmem):
            pltpu.sync_copy(ids_vmem, out_hbm.at[idx_vmem.at[0]])

        pltpu.emit_pipeline(
            body,
            grid=(npairs // SC_SCATTER_WINDOW,),
            in_specs=[pl.BlockSpec((SC_SCATTER_WINDOW, SC_SCATTER_WIDTH), index_map=lambda i: (i, 0)),
                      pl.BlockSpec((1, SC_SCATTER_WINDOW), index_map=lambda i: (0, i))],
            out_specs=[],
            core_axis_name=("core", "subcore"),
            dimension_semantics=(pltpu.PARALLEL,),
        )(ids_hbm, idx_hbm)

    return scatter(ids, dest_flat.reshape(1, npairs))[:, 0]


def _expert_kernel(be_ref, nu_ref, eord_ref, elist_ref, nexp_ref, tokc_ref, tokn_ref, hp_hbm, w1_hbm, w3_hbm,
                   w2_hbm, y_ref, xg0_ref, xg1_ref, w1_st, w3_st, w2_st, w13_s, w2_s, gsem, wsem, *, layer):
    b = pl.program_id(0)
    nused = nu_ref[0]
    ff = w2_s.shape[0]
    rows = y_ref.shape[0]
    xgs = (xg0_ref, xg1_ref)

    def row_copy(tok_ref, s, r):
        return pltpu.make_async_copy(hp_hbm.at[tok_ref[0, 0, r]], xgs[s].at[r], gsem.at[s])

    def drain_rows(s):
        pltpu.make_async_copy(hp_hbm.at[pl.ds(0, rows)], xgs[s], gsem.at[s]).wait()

    @pl.when(b == 0)
    def _():
        def body(r, carry):
            row_copy(tokc_ref, 0, r).start(priority=1)
            return carry

        lax.fori_loop(0, rows, body, 0, unroll=8)

    def weight_copies(j, slot):
        e = elist_ref[j]
        return [pltpu.make_async_copy(src.at[layer, e], dst.at[slot], wsem.at[slot])
                for src, dst in ((w1_hbm, w1_st), (w3_hbm, w3_st), (w2_hbm, w2_st))]

    @pl.when(b == 0)
    def _():
        for cp in weight_copies(0, 0):
            cp.start()

    j = eord_ref[b]
    first = (b == 0) | (be_ref[b] != be_ref[jnp.maximum(b - 1, 0)])

    @pl.when(first & (b < nused))
    def _():
        slot = j % 2
        for cp in weight_copies(j, slot):
            cp.wait()

        @pl.when(j + 1 < nexp_ref[0])
        def _():
            for cp in weight_copies(j + 1, 1 - slot):
                cp.start()

        w13_s[:, :ff] = w1_st[slot].astype(bf16)
        w13_s[:, ff:] = w3_st[slot].astype(bf16)
        w2_s[...] = w2_st[slot].astype(bf16)

    def step(s):
        drain_rows(s)
        cr = rows // EXPERT_CHUNKS
        gr = rows // EXPERT_ISSUE_GROUPS
        for g in range(EXPERT_ISSUE_GROUPS):
            @pl.when(b < nused)
            def _():
                for r in range(g * gr, (g + 1) * gr):
                    row_copy(tokn_ref, 1 - s, r).start(priority=1)
                for c in range(g * gr // cr, (g + 1) * gr // cr):
                    rs = slice(c * cr, (c + 1) * cr)
                    lo, hi = _unpack_halves(_tiles_to_rows(xgs[s][rs]))
                    x = jnp.concatenate([lo, hi], axis=1).astype(bf16)
                    hcat = jnp.dot(x, w13_s[...], preferred_element_type=f32)
                    a = jax.nn.silu(hcat[:, :ff]) * hcat[:, ff:]
                    y = jnp.dot(a.astype(bf16), w2_s[...], preferred_element_type=f32)
                    y_ref[rs] = _rows_to_tiles(_pack_halves(y.astype(bf16).astype(f32)))

    for s in range(2):
        @pl.when((b < nused) & (b % 2 == s))
        def _():
            step(s)

        @pl.when((b == nused) & (b % 2 == s))
        def _():
            drain_rows(s)

    @pl.when(b >= nused)
    def _():
        y_ref[...] = jnp.zeros_like(y_ref)


def _experts(hp, row_tok, block_e, nused, expert_ord, expert_list, nexp, w1, w3, w2, layer):
    n, s, lanes = hp.shape
    nb = block_e.shape[0]
    _, ne, d, ff = w1.shape
    rows = ROW_BLOCK
    tok3 = row_tok.reshape(nb, 1, rows)
    grid_spec = pltpu.PrefetchScalarGridSpec(
        num_scalar_prefetch=5,
        grid=(nb,),
        in_specs=[
            pl.BlockSpec((1, 1, rows), lambda b, be, nu, eo, el, nx: (b, 0, 0), memory_space=pltpu.SMEM),
            pl.BlockSpec((1, 1, rows), lambda b, be, nu, eo, el, nx: (jnp.minimum(b + 1, nb - 1), 0, 0),
                         memory_space=pltpu.SMEM),
            pl.BlockSpec(memory_space=pl.ANY),
            pl.BlockSpec(memory_space=pl.ANY),
            pl.BlockSpec(memory_space=pl.ANY),
            pl.BlockSpec(memory_space=pl.ANY),
        ],
        out_specs=pl.BlockSpec((rows, s, lanes), lambda b, be, nu, eo, el, nx: (b, 0, 0)),
        scratch_shapes=[pltpu.VMEM((rows, s, lanes), u32), pltpu.VMEM((rows, s, lanes), u32),
                        pltpu.VMEM((2, d, ff), f32), pltpu.VMEM((2, d, ff), f32), pltpu.VMEM((2, ff, d), f32),
                        pltpu.VMEM((d, 2 * ff), bf16), pltpu.VMEM((ff, d), bf16),
                        pltpu.SemaphoreType.DMA((2,)), pltpu.SemaphoreType.DMA((2,))],
    )
    return pl.pallas_call(
        functools.partial(_expert_kernel, layer=layer),
        grid_spec=grid_spec,
        out_shape=jax.ShapeDtypeStruct((nb * rows, s, lanes), u32),
        compiler_params=_cparams(("arbitrary",)),
        name="experts",
    )(block_e, nused, expert_ord, expert_list, nexp, tok3, tok3, hp, w1, w3, w2)


def _combine_kernel(dc_ref, dn_ref, x_ref, hp_ref, w_ref, g2_ref, sw13_ref, sw2_ref, fg_ref, y_hbm,
                    o_ref, buf0_ref, buf1_ref, sem, *, final):
    i = pl.program_id(0)
    nsteps = pl.num_programs(0)
    slot = i % 2
    tm = x_ref.shape[0]
    half = hp_ref.shape[1] * hp_ref.shape[2]
    sff = sw2_ref.shape[0]
    bufs = (buf0_ref, buf1_ref)

    def row_copy(d_ref, s, t, k):
        return pltpu.make_async_copy(y_hbm.at[d_ref[k, t]], bufs[s].at[k * tm + t], sem.at[s])

    def drain(s):
        for k in range(TOP_K):
            pltpu.make_async_copy(y_hbm.at[pl.ds(0, tm)], bufs[s].at[pl.ds(k * tm, tm)], sem.at[s]).wait()

    @pl.when(i == 0)
    def _():
        def body(t, carry):
            for k in range(TOP_K):
                row_copy(dc_ref, 0, t, k).start(priority=k % 2)
            return carry

        lax.fori_loop(0, tm, body, 0, unroll=2)

    def step(s):
        drain(s)
        for t in range(tm):
            for k in range(TOP_K):
                row_copy(dn_ref, 1 - s, t, k).start(priority=k % 2)

        lo, hi = _unpack_halves(_tiles_to_rows(hp_ref[...]))
        hx = jnp.concatenate([lo, hi], axis=1).astype(bf16)
        hcat = jnp.dot(hx, sw13_ref[...], preferred_element_type=f32)
        a = jax.nn.silu(hcat[:, :sff]) * hcat[:, sff:]
        shared = jnp.dot(a.astype(bf16), sw2_ref[...], preferred_element_type=f32)

        w = w_ref[...]
        acc_lo = shared[:, :half]
        acc_hi = shared[:, half:]
        for k in range(TOP_K):
            ylo, yhi = _unpack_halves(_tiles_to_rows(bufs[s][k * tm:(k + 1) * tm]))
            acc_lo = acc_lo + w[:, k:k + 1] * ylo
            acc_hi = acc_hi + w[:, k:k + 1] * yhi
        g2 = g2_ref[0]
        out_lo = x_ref[:, :half] + g2[:, :half] * acc_lo
        out_hi = x_ref[:, half:] + g2[:, half:] * acc_hi
        if final:
            ms = (jnp.sum(out_lo * out_lo, axis=-1, keepdims=True)
                  + jnp.sum(out_hi * out_hi, axis=-1, keepdims=True)) / (2 * half)
            inv = lax.rsqrt(ms + EPS)
            out_lo = out_lo * inv * fg_ref[:, :half]
            out_hi = out_hi * inv * fg_ref[:, half:]
        o_ref[:, :half] = out_lo
        o_ref[:, half:] = out_hi

    for s in range(2):
        @pl.when(slot == s)
        def _():
            step(s)

        @pl.when((slot == s) & (i + 1 == nsteps))
        def _():
            drain(1 - s)


def _combine(x, hp, wts, dest_t, g2, sw13, sw2, final_g, ybuf, seq, final):
    n, d = x.shape
    _, s, lanes = hp.shape
    tm = min(TILE_COMB, seq)
    nt = n // tm
    tiles_per_seq = seq // tm
    slots = dest_t.shape[0]
    return pl.pallas_call(
        functools.partial(_combine_kernel, final=final),
        grid=(nt,),
        in_specs=[
            pl.BlockSpec((slots, tm), lambda i: (0, i), memory_space=pltpu.SMEM),
            pl.BlockSpec((slots, tm), lambda i: (0, jnp.minimum(i + 1, nt - 1)), memory_space=pltpu.SMEM),
            pl.BlockSpec((tm, d), lambda i: (i, 0)),
            pl.BlockSpec((tm, s, lanes), lambda i: (i, 0, 0)),
            pl.BlockSpec((tm, V7X_SUBLANES), lambda i: (i, 0)),
            pl.BlockSpec((1, 1, d), lambda i: (i // tiles_per_seq, 0, 0)),
            pl.BlockSpec(sw13.shape, lambda i: (0, 0)),
            pl.BlockSpec(sw2.shape, lambda i: (0, 0)),
            pl.BlockSpec((1, d), lambda i: (0, 0)),
            pl.BlockSpec(memory_space=pl.ANY),
        ],
        out_specs=pl.BlockSpec((tm, d), lambda i: (i, 0)),
        out_shape=jax.ShapeDtypeStruct((n, d), f32),
        scratch_shapes=[pltpu.VMEM((tm * TOP_K, s, lanes), u32), pltpu.VMEM((tm * TOP_K, s, lanes), u32),
                        pltpu.SemaphoreType.DMA((2,))],
        compiler_params=_cparams(("arbitrary",)),
        name="combine",
    )(dest_t, dest_t, x, hp, wts.T, g2, sw13, sw2, final_g.reshape(1, d), ybuf)


def _moe(x, g, sc, sh, g2, router_w, router_b, w1, w3, w2, layer, sw1, sw3, sw2, final_g, seq, final):
    n, d = x.shape
    ne = router_w.shape[1]
    hp, eidx_t, wts_t, rank_t, cnt = _route(x, g, sc, sh, router_w, router_b, seq)
    counts = cnt[:, 0]
    pcounts = (counts + ROW_BLOCK - 1) // ROW_BLOCK * ROW_BLOCK
    pends = jnp.cumsum(pcounts)
    poffs = pends - pcounts
    nb = -(-(n * TOP_K) // ROW_BLOCK) + ne + 1
    starts = jnp.arange(nb, dtype=i32) * ROW_BLOCK
    before = pends[None, :] <= starts[:, None]
    block_e = jnp.minimum(jnp.sum(before.astype(i32), axis=1), ne - 1)
    nused = (pends[-1:] // ROW_BLOCK).astype(i32)
    dest_t = _dest(poffs.astype(i32), eidx_t, rank_t)
    pair_of_row = _scatter_pair_ids(dest_t[:TOP_K].reshape(-1), nb * ROW_BLOCK).reshape(nb, ROW_BLOCK)
    real_end = jnp.min(jnp.where(before, jnp.iinfo(jnp.int32).max, (poffs + counts)[None, :]), axis=1)
    nvalid = jnp.where(starts < pends[-1], jnp.clip(real_end - starts, 0, ROW_BLOCK), 0)
    valid = jnp.arange(ROW_BLOCK, dtype=i32)[None, :] < nvalid[:, None]
    row_tok = jnp.where(valid, pair_of_row % n, 0).astype(i32)
    has_rows = pcounts > 0
    expert_list = jnp.nonzero(has_rows, size=ne, fill_value=ne - 1)[0].astype(i32)
    nexp = jnp.sum(has_rows.astype(i32)).reshape(1)
    expert_ord = jnp.minimum(jnp.sum((before & has_rows[None, :]).astype(i32), axis=1), nexp - 1)
    ybuf = _experts(hp, row_tok, block_e, nused, expert_ord, expert_list, nexp, w1, w3, w2, layer)
    sw13 = jnp.concatenate([sw1, sw3], axis=-1).astype(bf16)
    return _combine(x, hp, wts_t, dest_t, g2, sw13, sw2.astype(bf16), final_g, ybuf, seq, final)


def kernel(x, c, ada_w, ada_b, norm_mix_g, norm_ffn_g, ab_w_in, sgu_ln_g, sgu_ln_b, sgu_w, sgu_b, pool_w,
           pool_scale, ab_w_out, cd_w_in, conv3_w, conv3_b, conv31_w, conv31_b, cd_ln_g, cd_ln_b, cd_w_out,
           router_w, router_b, exp_w1, exp_w3, exp_w2, sh_w1, sh_w3, sh_w2, final_g):
    bsz, seq, d = x.shape
    depth = ada_w.shape[0]
    n = bsz * seq
    xf = x.reshape(n, d)
    mod = _ada(c, ada_w, ada_b)
    for l in range(depth):
        sh1, sc1, g1, sh2, sc2, g2 = [mod[l, :, i * d:(i + 1) * d].reshape(bsz, 1, d) for i in range(6)]
        if l % 2 == 0:
            e = l // 2
            p = _norm_mm(xf, norm_mix_g[l], sc1, sh1, ab_w_in[e].astype(bf16), seq)
            xf = _ab_mix(xf, p, g1, sgu_ln_g[e], sgu_ln_b[e], sgu_w[e], sgu_b[e], pool_w[e], pool_scale[e],
                         ab_w_out[e].astype(bf16), bsz, seq)
        else:
            o = l // 2
            p = _norm_mm(xf, norm_mix_g[l], sc1, sh1, cd_w_in[o].astype(bf16), seq)
            xf = _cd_mix(xf, p, g1, conv3_w[o], conv3_b[o], conv31_w[o], conv31_b[o], cd_ln_g[o], cd_ln_b[o],
                         cd_w_out[o].astype(bf16), bsz, seq)
        xf = _moe(xf, norm_ffn_g[l], sc2, sh2, g2, router_w[l], router_b[l], exp_w1, exp_w3, exp_w2, l,
                  sh_w1[l], sh_w3[l], sh_w2[l], final_g, seq, final=(l == depth - 1))
    return xf.reshape(bsz, seq, d)
```

```python
import functools

import jax
import jax.numpy as jnp
from jax import lax
from jax.experimental import pallas as pl
from jax.experimental.pallas import tpu as pltpu

f32 = jnp.float32
bf16 = jnp.bfloat16
u32 = jnp.uint32
i32 = jnp.int32

EPS = 1e-6
CHUNK = 64
SGU_BLOCK = 128
A_HEADS = 8
POOL_WINDOWS = (2, 4, 8, 16)
SHORT_CONV = 3
CONF_CONV = 31
N_EXPERTS = 64
TOP_K = 6
N_GROUPS = 8
TOPK_GROUPS = 4
ROUTED_SCALE = 2.5

V7X_LANES = 128
V7X_SUBLANES = 8
V7X_VMEM_LIMIT = 56 * 1024 * 1024

ROW_BLOCK = 512
TILE_MM = 1024
TILE_MIX = 512
TILE_ROUTE = 512
EXPERT_CHUNKS = 4
TILE_DISPATCH = 1024
TILE_COMB = 256
CONV_ROWS = 128
CONV_COLS = 128
LN_ROWS = 32
NORM_ROWS = 16
HI_MASK = 0xFFFF0000


def _cparams(sem):
    return pltpu.CompilerParams(dimension_semantics=sem, vmem_limit_bytes=V7X_VMEM_LIMIT)


def _pack_halves(v):
    w = v.shape[1] // 2
    bits = pltpu.bitcast(v, u32)
    return (bits[:, :w] >> 16) | bits[:, w:]


def _unpack_halves(words):
    lo = pltpu.bitcast(words << 16, f32)
    hi = pltpu.bitcast(words & jnp.uint32(HI_MASK), f32)
    return lo, hi


def _rows_to_tiles(m):
    s = m.shape[1] // V7X_LANES
    st = jnp.stack([m[:, i * V7X_LANES:(i + 1) * V7X_LANES] for i in range(s)], axis=0)
    return pltpu.einshape("srl->rsl", st)


def _tiles_to_rows(t3):
    xt = pltpu.einshape("rsl->srl", t3)
    return jnp.concatenate([xt[i] for i in range(t3.shape[1])], axis=1)


def _modulated_rmsnorm(x, g, sc, sh):
    ms = jnp.mean(x * x, axis=-1, keepdims=True)
    y = x * lax.rsqrt(ms + EPS) * g
    return y * (1.0 + sc) + sh


def _ada_kernel(c_ref, w_ref, b_ref, o_ref):
    ca = jax.nn.silu(c_ref[...]).astype(bf16)
    o_ref[0] = jnp.dot(ca, w_ref[0].astype(bf16), preferred_element_type=f32) + b_ref[0]


def _ada(c, ada_w, ada_b):
    depth, d, n6 = ada_w.shape
    bsz = c.shape[0]
    tn = 1024
    return pl.pallas_call(
        _ada_kernel,
        grid=(depth, n6 // tn),
        in_specs=[
            pl.BlockSpec((bsz, d), lambda l, j: (0, 0)),
            pl.BlockSpec((1, d, tn), lambda l, j: (l, 0, j)),
            pl.BlockSpec((1, 1, tn), lambda l, j: (l, 0, j)),
        ],
        out_specs=pl.BlockSpec((1, bsz, tn), lambda l, j: (l, 0, j)),
        out_shape=jax.ShapeDtypeStruct((depth, bsz, n6), f32),
        compiler_params=_cparams(("parallel", "parallel")),
        name="ada_mod",
    )(c, ada_w, ada_b.reshape(depth, 1, n6))


def _norm_mm_kernel(x_ref, g_ref, sc_ref, sh_ref, w_ref, o_ref, h_ref):
    @pl.when(pl.program_id(1) == 0)
    def _():
        def body(r, carry):
            rows = pl.ds(pl.multiple_of(r * NORM_ROWS, NORM_ROWS), NORM_ROWS)
            h_ref[rows, :] = _modulated_rmsnorm(x_ref[rows, :], g_ref[...], sc_ref[0], sh_ref[0]).astype(bf16)
            return carry

        lax.fori_loop(0, x_ref.shape[0] // NORM_ROWS, body, 0, unroll=8)

    o_ref[...] = jnp.dot(h_ref[...], w_ref[...], preferred_element_type=f32).astype(o_ref.dtype)


def _norm_mm(x, g, sc, sh, w, seq):
    n, d = x.shape
    nout = w.shape[1]
    tm = min(TILE_MM, seq)
    tn = 1024
    tiles_per_seq = seq // tm
    return pl.pallas_call(
        _norm_mm_kernel,
        grid=(n // tm, nout // tn),
        in_specs=[
            pl.BlockSpec((tm, d), lambda i, j: (i, 0)),
            pl.BlockSpec((1, d), lambda i, j: (0, 0)),
            pl.BlockSpec((1, 1, d), lambda i, j: (i // tiles_per_seq, 0, 0)),
            pl.BlockSpec((1, 1, d), lambda i, j: (i // tiles_per_seq, 0, 0)),
            pl.BlockSpec((d, tn), lambda i, j: (0, j)),
        ],
        out_specs=pl.BlockSpec((tm, tn), lambda i, j: (i, j)),
        out_shape=jax.ShapeDtypeStruct((n, nout), bf16),
        scratch_shapes=[pltpu.VMEM((tm, d), bf16)],
        compiler_params=_cparams(("parallel", "arbitrary")),
        name="norm_in_proj",
    )(x, g.reshape(1, d), sc, sh, w)


def _ab_mix_kernel(x_ref, p_ref, g1_ref, lng_ref, lnb_ref, sw_ref, sbt_ref, pw_ref, ps_ref, wo_ref,
                   o_ref, ext_ref, cat_ref):
    s_idx = pl.program_id(1)
    ts = x_ref.shape[0]
    wa = lng_ref.shape[1]
    hd = wa // A_HEADS
    gb = pw_ref.shape[1]
    blk = SGU_BLOCK

    @pl.when(s_idx == 0)
    def _():
        ext_ref[0:blk, :] = jnp.zeros((blk, ext_ref.shape[1]), bf16)

    @pl.when(s_idx > 0)
    def _():
        ext_ref[0:blk, :] = ext_ref[ts:ts + blk, :]

    ext_ref[blk:, :] = p_ref[:, 2 * wa:]

    ci = lax.broadcasted_iota(i32, (blk, blk), 0) // CHUNK
    cj = lax.broadcasted_iota(i32, (blk, blk), 1) // CHUNK
    causal = ci >= cj
    trow = lax.broadcasted_iota(i32, (blk, 2 * blk), 0) + blk
    tcol = lax.broadcasted_iota(i32, (blk, 2 * blk), 1)
    tpos = lax.broadcasted_iota(i32, (blk, 1), 0)

    def block_body(bi, carry):
        r0 = pl.multiple_of(bi * blk, blk)
        rows = pl.ds(r0, blk)
        v = p_ref[rows, wa:2 * wa].astype(f32)
        mu = jnp.mean(v, axis=-1, keepdims=True)
        vc = v - mu
        var = jnp.mean(vc * vc, axis=-1, keepdims=True)
        vn = (vc * lax.rsqrt(var + EPS) * lng_ref[...] + lnb_ref[...]).astype(bf16)
        for h in range(A_HEADS):
            cols = slice(h * hd, (h + 1) * hd)
            wm = jnp.where(causal, sw_ref[h], 0.0).astype(bf16)
            s = jnp.dot(wm, vn[:, cols], preferred_element_type=f32) + sbt_ref[:, h:h + 1]
            u = p_ref[rows, cols].astype(f32)
            cat_ref[rows, cols] = (u * s).astype(bf16)
        win_rows = pl.ds(r0, 2 * blk)
        pos = (s_idx * ts + r0 + tpos + 1).astype(f32)
        for gi, win in enumerate(POOL_WINDOWS):
            cols = slice(gi * gb, (gi + 1) * gb)
            band = jnp.where((tcol <= trow) & (tcol > trow - win), 1.0, 0.0).astype(bf16)
            window = ext_ref[win_rows, cols]
            sums = jnp.dot(band, window, preferred_element_type=f32)
            xb = window[blk:, :].astype(f32)
            pooled = sums / jnp.minimum(pos, float(win)) - xb
            yb = jnp.dot(pooled.astype(bf16), pw_ref[gi], preferred_element_type=f32) * ps_ref[:, cols]
            cat_ref[rows, wa + gi * gb:wa + (gi + 1) * gb] = yb.astype(bf16)
        return carry

    lax.fori_loop(0, ts // blk, block_body, 0)
    mix = jnp.dot(cat_ref[...], wo_ref[...], preferred_element_type=f32)
    o_ref[...] = x_ref[...] + g1_ref[0] * mix


def _ab_mix(x, p, g1, ln_g, ln_b, sgu_w, sgu_b, pool_w, pool_scale, w_out, bsz, seq):
    n, d = x.shape
    wa = ln_g.shape[0]
    wb = pool_scale.shape[0]
    ts = min(TILE_MIX, seq)
    nst = seq // ts
    const2 = lambda b, s: (0, 0)
    const3 = lambda b, s: (0, 0, 0)
    return pl.pallas_call(
        _ab_mix_kernel,
        grid=(bsz, nst),
        in_specs=[
            pl.BlockSpec((ts, d), lambda b, s: (b * nst + s, 0)),
            pl.BlockSpec((ts, p.shape[1]), lambda b, s: (b * nst + s, 0)),
            pl.BlockSpec((1, 1, d), lambda b, s: (b, 0, 0)),
            pl.BlockSpec((1, wa), const2),
            pl.BlockSpec((1, wa), const2),
            pl.BlockSpec(sgu_w.shape, const3),
            pl.BlockSpec((SGU_BLOCK, A_HEADS), const2),
            pl.BlockSpec(pool_w.shape, const3),
            pl.BlockSpec((1, wb), const2),
            pl.BlockSpec(w_out.shape, const2),
        ],
        out_specs=pl.BlockSpec((ts, d), lambda b, s: (b * nst + s, 0)),
        out_shape=jax.ShapeDtypeStruct((n, d), f32),
        scratch_shapes=[pltpu.VMEM((ts + SGU_BLOCK, wb), bf16), pltpu.VMEM((ts, wa + wb), bf16)],
        compiler_params=_cparams(("parallel", "arbitrary")),
        name="mixer_ab",
    )(x, p, g1, ln_g.reshape(1, wa), ln_b.reshape(1, wa), sgu_w, sgu_b.T, pool_w.astype(bf16),
      pool_scale.reshape(1, wb), w_out)


def _cd_mix_kernel(x_ref, p_ref, g1_ref, c3w_ref, c3b_ref, c31w_ref, c31b_ref, lng_ref, lnb_ref, wo_ref,
                   o_ref, e3_ref, e31_ref, d_ref, cat_ref):
    s_idx = pl.program_id(1)
    ts = x_ref.shape[0]
    wc = c3b_ref.shape[1]
    wd = c31b_ref.shape[1]
    h3 = e3_ref.shape[0] - ts
    h31 = e31_ref.shape[0] - ts

    @pl.when(s_idx == 0)
    def _():
        e3_ref[0:h3, :] = jnp.zeros((h3, wc), f32)
        e31_ref[0:h31, :] = jnp.zeros((h31, wd), f32)

    @pl.when(s_idx > 0)
    def _():
        e3_ref[0:h3, :] = e3_ref[ts:ts + h3, :]
        e31_ref[0:h31, :] = e31_ref[ts:ts + h31, :]

    cg = p_ref[:, wc:2 * wc].astype(f32)
    xc = p_ref[:, 2 * wc:3 * wc].astype(f32)
    e3_ref[h3:, :] = cg * xc
    da = p_ref[:, 3 * wc:3 * wc + wd].astype(f32)
    dg = p_ref[:, 3 * wc + wd:].astype(f32)
    e31_ref[h31:, :] = da * jax.nn.sigmoid(dg)

    cw = CONV_COLS
    rr = CONV_ROWS
    sub = V7X_SUBLANES

    def causal_conv(w_ref, b_ref, e_ref, halo, r0, cols):
        ntaps = w_ref.shape[0]
        acc = jnp.broadcast_to(b_ref[:, cols], (rr, cw))
        for b in range(sub):
            taps = [k for k in range(ntaps) if (halo - (ntaps - 1) + k) % sub == b]
            if not taps:
                continue
            nload = rr + sub if b else rr
            z = None
            for k in taps:
                a8 = (halo - (ntaps - 1) + k) // sub * sub
                t = w_ref[k:k + 1, cols] * e_ref[pl.ds(r0 + a8, nload), cols]
                z = t if z is None else z + t
            if b:
                z = pltpu.roll(z, nload - b, 0)[:rr]
            acc = acc + z
        return acc

    def rows_body(ri, carry):
        r0 = pl.multiple_of(ri * rr, rr)
        rows = pl.ds(r0, rr)
        for c in range(wc // cw):
            cols = slice(c * cw, (c + 1) * cw)
            acc = causal_conv(c3w_ref, c3b_ref, e3_ref, h3, r0, cols)
            cat_ref[rows, cols] = (p_ref[rows, cols].astype(f32) * acc).astype(bf16)
        for c in range(wd // cw):
            cols = slice(c * cw, (c + 1) * cw)
            d_ref[rows, cols] = causal_conv(c31w_ref, c31b_ref, e31_ref, h31, r0, cols)
        for q in range(rr // LN_ROWS):
            qrows = pl.ds(r0 + q * LN_ROWS, LN_ROWS)
            dv = d_ref[qrows, :]
            mu = jnp.mean(dv, axis=-1, keepdims=True)
            dc = dv - mu
            var = jnp.mean(dc * dc, axis=-1, keepdims=True)
            dn = dc * lax.rsqrt(var + EPS) * lng_ref[...] + lnb_ref[...]
            cat_ref[qrows, wc:] = jax.nn.silu(dn).astype(bf16)
        return carry

    lax.fori_loop(0, ts // rr, rows_body, 0)
    mix = jnp.dot(cat_ref[...], wo_ref[...], preferred_element_type=f32)
    o_ref[...] = x_ref[...] + g1_ref[0] * mix


def _cd_mix(x, p, g1, c3w, c3b, c31w, c31b, ln_g, ln_b, w_out, bsz, seq):
    n, d = x.shape
    wc = c3b.shape[0]
    wd = c31b.shape[0]
    ts = min(TILE_MIX, seq)
    nst = seq // ts
    h3 = V7X_SUBLANES
    h31 = 4 * V7X_SUBLANES
    assert h3 >= SHORT_CONV - 1 and h31 >= CONF_CONV - 1
    const2 = lambda b, s: (0, 0)
    return pl.pallas_call(
        _cd_mix_kernel,
        grid=(bsz, nst),
        in_specs=[
            pl.BlockSpec((ts, d), lambda b, s: (b * nst + s, 0)),
            pl.BlockSpec((ts, p.shape[1]), lambda b, s: (b * nst + s, 0)),
            pl.BlockSpec((1, 1, d), lambda b, s: (b, 0, 0)),
            pl.BlockSpec(c3w.shape, const2),
            pl.BlockSpec((1, wc), const2),
            pl.BlockSpec(c31w.shape, const2),
            pl.BlockSpec((1, wd), const2),
            pl.BlockSpec((1, wd), const2),
            pl.BlockSpec((1, wd), const2),
            pl.BlockSpec(w_out.shape, const2),
        ],
        out_specs=pl.BlockSpec((ts, d), lambda b, s: (b * nst + s, 0)),
        out_shape=jax.ShapeDtypeStruct((n, d), f32),
        scratch_shapes=[pltpu.VMEM((ts + h3, wc), f32), pltpu.VMEM((ts + h31, wd), f32),
                        pltpu.VMEM((ts, wd), f32), pltpu.VMEM((ts, wc + wd), bf16)],
        compiler_params=_cparams(("parallel", "arbitrary")),
        name="mixer_cd",
    )(x, p, g1, c3w, c3b.reshape(1, wc), c31w, c31b.reshape(1, wd), ln_g.reshape(1, wd), ln_b.reshape(1, wd),
      w_out)


def _route_kernel(x_ref, g_ref, sc_ref, sh_ref, rwh_ref, rwl_ref, rb_ref,
                  hp_ref, eidx_ref, wts_ref, rank_ref, cnt_ref, carry_ref):
    tm = x_ref.shape[0]
    ne = rwh_ref.shape[0]
    gsz = ne // N_GROUPS
    neg = -jnp.inf

    @pl.when(pl.program_id(0) == 0)
    def _():
        carry_ref[...] = jnp.zeros_like(carry_ref)

    h = _modulated_rmsnorm(x_ref[...], g_ref[...], sc_ref[0], sh_ref[0])
    hh = h.astype(bf16)
    hf = hh.astype(f32)
    hl = (h - hf).astype(bf16)
    hp_ref[...] = _rows_to_tiles(_pack_halves(hf))

    nt = (((1,), (1,)), ((), ()))
    logits = (lax.dot_general(rwh_ref[...], hh, nt, preferred_element_type=f32)
              + lax.dot_general(rwh_ref[...], hl, nt, preferred_element_type=f32)
              + lax.dot_general(rwl_ref[...], hh, nt, preferred_element_type=f32))
    scores = jax.nn.sigmoid(logits)
    choice = scores + rb_ref[...]

    c3 = choice.reshape(N_GROUPS, gsz, tm)
    sub = lax.broadcasted_iota(i32, c3.shape, 1)
    m1 = jnp.max(c3, axis=1, keepdims=True)
    first = jnp.min(jnp.where(c3 == m1, sub, gsz), axis=1, keepdims=True)
    m2 = jnp.max(jnp.where(sub == first, neg, c3), axis=1, keepdims=True)
    gs = m1 + m2
    keep = []
    for gi in range(N_GROUPS):
        beaten = jnp.zeros((1, tm), i32)
        for gj in range(N_GROUPS):
            if gj == gi:
                continue
            wins = (gs[gj] > gs[gi]) | ((gs[gj] == gs[gi]) & (gj < gi))
            beaten = beaten + wins.astype(i32)
        keep.append(beaten < TOPK_GROUPS)
    masked = jnp.concatenate(
        [jnp.where(keep[gi], c3[gi], neg) for gi in range(N_GROUPS)], axis=0)

    eio = lax.broadcasted_iota(i32, (ne, tm), 0)
    cur = masked
    onehot = jnp.zeros((ne, tm), f32)
    idxs, sels, svals = [], [], []
    for _ in range(TOP_K):
        m = jnp.max(cur, axis=0, keepdims=True)
        idx = jnp.min(jnp.where(cur == m, eio, ne), axis=0, keepdims=True)
        sel = eio == idx
        svals.append(jnp.sum(jnp.where(sel, scores, 0.0), axis=0, keepdims=True))
        cur = jnp.where(sel, neg, cur)
        onehot = onehot + sel.astype(f32)
        idxs.append(idx)
        sels.append(sel)
    ssum = svals[0]
    for sv in svals[1:]:
        ssum = ssum + sv

    tr = lax.broadcasted_iota(i32, (tm, tm), 0)
    tc = lax.broadcasted_iota(i32, (tm, tm), 1)
    upper = jnp.where(tr < tc, 1.0, 0.0).astype(bf16)
    before = jnp.dot(onehot.astype(bf16), upper, preferred_element_type=f32) + carry_ref[...]
    new_carry = carry_ref[...] + jnp.sum(onehot, axis=1, keepdims=True)
    carry_ref[...] = new_carry
    cnt_ref[...] = jnp.broadcast_to(new_carry, cnt_ref.shape).astype(i32)

    zrow_i = jnp.zeros((1, tm), i32)
    zrow_f = jnp.zeros((1, tm), f32)
    ranks = [jnp.sum(jnp.where(sel, before, 0.0), axis=0, keepdims=True).astype(i32) for sel in sels]
    pad = V7X_SUBLANES - TOP_K
    eidx_ref[...] = jnp.concatenate(idxs + [zrow_i] * pad, axis=0)
    wts_ref[...] = jnp.concatenate([sv / ssum * ROUTED_SCALE for sv in svals] + [zrow_f] * pad, axis=0)
    rank_ref[...] = jnp.concatenate(ranks + [zrow_i] * pad, axis=0)


def _route(x, g, sc, sh, router_w, router_b, seq):
    n, d = x.shape
    ne = router_w.shape[1]
    tm = min(TILE_ROUTE, seq)
    tiles_per_seq = seq // tm
    rwt = router_w.T
    rwh = rwt.astype(bf16)
    rwl = (rwt - rwh.astype(f32)).astype(bf16)
    slots = V7X_SUBLANES
    return pl.pallas_call(
        _route_kernel,
        grid=(n // tm,),
        in_specs=[
            pl.BlockSpec((tm, d), lambda i: (i, 0)),
            pl.BlockSpec((1, d), lambda i: (0, 0)),
            pl.BlockSpec((1, 1, d), lambda i: (i // tiles_per_seq, 0, 0)),
            pl.BlockSpec((1, 1, d), lambda i: (i // tiles_per_seq, 0, 0)),
            pl.BlockSpec((ne, d), lambda i: (0, 0)),
            pl.BlockSpec((ne, d), lambda i: (0, 0)),
            pl.BlockSpec((ne, 1), lambda i: (0, 0)),
        ],
        out_specs=[
            pl.BlockSpec((tm, d // 2 // V7X_LANES, V7X_LANES), lambda i: (i, 0, 0)),
            pl.BlockSpec((slots, tm), lambda i: (0, i)),
            pl.BlockSpec((slots, tm), lambda i: (0, i)),
            pl.BlockSpec((slots, tm), lambda i: (0, i)),
            pl.BlockSpec((ne, V7X_LANES), lambda i: (0, 0)),
        ],
        out_shape=[
            jax.ShapeDtypeStruct((n, d // 2 // V7X_LANES, V7X_LANES), u32),
            jax.ShapeDtypeStruct((slots, n), i32),
            jax.ShapeDtypeStruct((slots, n), f32),
            jax.ShapeDtypeStruct((slots, n), i32),
            jax.ShapeDtypeStruct((ne, V7X_LANES), i32),
        ],
        scratch_shapes=[pltpu.VMEM((ne, 1), f32)],
        compiler_params=_cparams(("arbitrary",)),
        name="route",
    )(x, g.reshape(1, d), sc, sh, rwh, rwl, router_b.reshape(ne, 1))


def _dest_kernel(poffs_ref, eidx_ref, rank_ref, dest_ref):
    eidx = eidx_ref[...]
    base = jnp.zeros(eidx.shape, i32)
    for e in range(N_EXPERTS):
        base = jnp.where(eidx == e, poffs_ref[e], base)
    dest_ref[...] = base + rank_ref[...]


def _dest(poffs, eidx_t, rank_t):
    slots, n = eidx_t.shape
    td = min(4096, n)
    grid_spec = pltpu.PrefetchScalarGridSpec(
        num_scalar_prefetch=1,
        grid=(n // td,),
        in_specs=[pl.BlockSpec((slots, td), lambda i, po: (0, i)),
                  pl.BlockSpec((slots, td), lambda i, po: (0, i))],
        out_specs=pl.BlockSpec((slots, td), lambda i, po: (0, i)),
    )
    return pl.pallas_call(
        _dest_kernel,
        grid_spec=grid_spec,
        out_shape=jax.ShapeDtypeStruct((slots, n), i32),
        compiler_params=_cparams(("parallel",)),
        name="dest_rows",
    )(poffs, eidx_t, rank_t)


def _dispatch_kernel(nu_ref, zb_ref, dest_ref, hp_ref, xb_hbm, zero_ref, zsem, sem):
    i = pl.program_id(0)
    tm = hp_ref.shape[0]
    rows = zero_ref.shape[0]
    nz = zb_ref.shape[0]

    @pl.when(i == 0)
    def _():
        zero_ref[...] = jnp.zeros_like(zero_ref)

        def zcopy(j):
            return pltpu.make_async_copy(zero_ref, xb_hbm.at[pl.ds(zb_ref[j] * rows, rows)], zsem)

        def zstart(j, carry):
            @pl.when(zb_ref[j] >= 0)
            def _():
                zcopy(j).start()
            return carry

        def zwait(j, carry):
            @pl.when(zb_ref[j] >= 0)
            def _():
                zcopy(j).wait()
            return carry

        lax.fori_loop(0, nz, zstart, 0)
        lax.fori_loop(0, nz, zwait, 0)

    def body(t, carry):
        for k in range(TOP_K):
            pltpu.make_async_copy(hp_ref.at[t], xb_hbm.at[dest_ref[0, 0, t * TOP_K + k]],
                                  sem).start(priority=k % 2)
        return carry

    lax.fori_loop(0, tm, body, 0, unroll=2)
    for k in range(TOP_K):
        pltpu.make_async_copy(hp_ref, xb_hbm.at[pl.ds(0, tm)], sem).wait()


def _dispatch(hp, dest_t, nused, zero_blocks, nb):
    n, s, lanes = hp.shape
    tm = min(TILE_DISPATCH, n)
    dest = dest_t[:TOP_K].T.reshape(n // tm, 1, tm * TOP_K)
    grid_spec = pltpu.PrefetchScalarGridSpec(
        num_scalar_prefetch=2,
        grid=(n // tm,),
        in_specs=[
            pl.BlockSpec((1, 1, tm * TOP_K), lambda i, nu, zb: (i, 0, 0), memory_space=pltpu.SMEM),
            pl.BlockSpec((tm, s, lanes), lambda i, nu, zb: (i, 0, 0)),
        ],
        out_specs=pl.BlockSpec(memory_space=pl.ANY),
        scratch_shapes=[pltpu.VMEM((ROW_BLOCK, s, lanes), u32), pltpu.SemaphoreType.DMA(()),
                        pltpu.SemaphoreType.DMA(())],
    )
    return pl.pallas_call(
        _dispatch_kernel,
        grid_spec=grid_spec,
        out_shape=jax.ShapeDtypeStruct((nb * ROW_BLOCK, s, lanes), u32),
        compiler_params=_cparams(("arbitrary",)),
        name="dispatch",
    )(nused, zero_blocks, dest, hp)


def _expert_kernel(be_ref, nu_ref, eord_ref, elist_ref, nexp_ref, nv_ref, x_ref, w1_hbm, w3_hbm, w2_hbm, y_ref,
                   w1_st, w3_st, w2_st, w13_s, w2_s, wsem, *, layer):
    b = pl.program_id(0)
    nused = nu_ref[0]
    nvalid = nv_ref[b]
    ff = w2_s.shape[0]
    rows = x_ref.shape[0]
    cr = rows // EXPERT_CHUNKS

    def weight_copies(j, slot):
        e = elist_ref[j]
        return [pltpu.make_async_copy(src.at[layer, e], dst.at[slot], wsem.at[slot])
                for src, dst in ((w1_hbm, w1_st), (w3_hbm, w3_st), (w2_hbm, w2_st))]

    @pl.when(b == 0)
    def _():
        for cp in weight_copies(0, 0):
            cp.start()

    j = eord_ref[b]
    first = (b == 0) | (be_ref[b] != be_ref[jnp.maximum(b - 1, 0)])

    @pl.when(first & (b < nused))
    def _():
        slot = j % 2
        for cp in weight_copies(j, slot):
            cp.wait()

        @pl.when(j + 1 < nexp_ref[0])
        def _():
            for cp in weight_copies(j + 1, 1 - slot):
                cp.start()

        w13_s[:, :ff] = w1_st[slot].astype(bf16)
        w13_s[:, ff:] = w3_st[slot].astype(bf16)
        w2_s[...] = w2_st[slot].astype(bf16)

    def ffn_chunk(c):
        rs = slice(c * cr, (c + 1) * cr)
        lo, hi = _unpack_halves(_tiles_to_rows(x_ref[rs]))
        x = jnp.concatenate([lo, hi], axis=1).astype(bf16)
        hcat = jnp.dot(x, w13_s[...], preferred_element_type=f32)
        a = jax.nn.silu(hcat[:, :ff]) * hcat[:, ff:]
        y = jnp.dot(a.astype(bf16), w2_s[...], preferred_element_type=f32)
        y_ref[rs] = _rows_to_tiles(_pack_halves(y.astype(bf16).astype(f32)))

    @pl.when((b < nused) & (nvalid > rows - cr))
    def _():
        for c in range(EXPERT_CHUNKS):
            ffn_chunk(c)

    for c in range(EXPERT_CHUNKS):
        @pl.when((b < nused) & (nvalid <= rows - cr) & (nvalid > c * cr))
        def _():
            ffn_chunk(c)

        @pl.when((b >= nused) | (nvalid <= c * cr))
        def _():
            y_ref[c * cr:(c + 1) * cr] = jnp.zeros((cr,) + y_ref.shape[1:], y_ref.dtype)


def _experts(xbuf, block_e, nused, expert_ord, expert_list, nexp, nvalid, w1, w3, w2, layer):
    nrows, s, lanes = xbuf.shape
    nb = block_e.shape[0]
    _, ne, d, ff = w1.shape
    rows = ROW_BLOCK
    grid_spec = pltpu.PrefetchScalarGridSpec(
        num_scalar_prefetch=6,
        grid=(nb,),
        in_specs=[
            pl.BlockSpec((rows, s, lanes), lambda b, be, nu, eo, el, nx, nv: (jnp.minimum(b, nu[0] - 1), 0, 0)),
            pl.BlockSpec(memory_space=pl.ANY),
            pl.BlockSpec(memory_space=pl.ANY),
            pl.BlockSpec(memory_space=pl.ANY),
        ],
        out_specs=pl.BlockSpec((rows, s, lanes), lambda b, be, nu, eo, el, nx, nv: (b, 0, 0)),
        scratch_shapes=[pltpu.VMEM((2, d, ff), f32), pltpu.VMEM((2, d, ff), f32), pltpu.VMEM((2, ff, d), f32),
                        pltpu.VMEM((d, 2 * ff), bf16), pltpu.VMEM((ff, d), bf16),
                        pltpu.SemaphoreType.DMA((2,))],
    )
    return pl.pallas_call(
        functools.partial(_expert_kernel, layer=layer),
        grid_spec=grid_spec,
        out_shape=jax.ShapeDtypeStruct((nrows, s, lanes), u32),
        compiler_params=_cparams(("arbitrary",)),
        name="experts",
    )(block_e, nused, expert_ord, expert_list, nexp, nvalid, xbuf, w1, w3, w2)


def _combine_kernel(dc_ref, dn_ref, x_ref, hp_ref, w_ref, g2_ref, sw13_ref, sw2_ref, fg_ref, y_hbm,
                    o_ref, buf0_ref, buf1_ref, sem, *, final):
    i = pl.program_id(0)
    nsteps = pl.num_programs(0)
    slot = i % 2
    tm = x_ref.shape[0]
    half = hp_ref.shape[1] * hp_ref.shape[2]
    sff = sw2_ref.shape[0]
    bufs = (buf0_ref, buf1_ref)

    def row_copy(d_ref, s, t, k):
        return pltpu.make_async_copy(y_hbm.at[d_ref[k, t]], bufs[s].at[k * tm + t], sem.at[s])

    def drain(s):
        for k in range(TOP_K):
            pltpu.make_async_copy(y_hbm.at[pl.ds(0, tm)], bufs[s].at[pl.ds(k * tm, tm)], sem.at[s]).wait()

    @pl.when(i == 0)
    def _():
        def body(t, carry):
            for k in range(TOP_K):
                row_copy(dc_ref, 0, t, k).start(priority=k % 2)
            return carry

        lax.fori_loop(0, tm, body, 0, unroll=2)

    def step(s):
        drain(s)
        for t in range(tm):
            for k in range(TOP_K):
                row_copy(dn_ref, 1 - s, t, k).start(priority=k % 2)

        lo, hi = _unpack_halves(_tiles_to_rows(hp_ref[...]))
        hx = jnp.concatenate([lo, hi], axis=1).astype(bf16)
        hcat = jnp.dot(hx, sw13_ref[...], preferred_element_type=f32)
        a = jax.nn.silu(hcat[:, :sff]) * hcat[:, sff:]
        shared = jnp.dot(a.astype(bf16), sw2_ref[...], preferred_element_type=f32)

        w = w_ref[...]
        acc_lo = shared[:, :half]
        acc_hi = shared[:, half:]
        for k in range(TOP_K):
            ylo, yhi = _unpack_halves(_tiles_to_rows(bufs[s][k * tm:(k + 1) * tm]))
            acc_lo = acc_lo + w[:, k:k + 1] * ylo
            acc_hi = acc_hi + w[:, k:k + 1] * yhi
        g2 = g2_ref[0]
        out_lo = x_ref[:, :half] + g2[:, :half] * acc_lo
        out_hi = x_ref[:, half:] + g2[:, half:] * acc_hi
        if final:
            ms = (jnp.sum(out_lo * out_lo, axis=-1, keepdims=True)
                  + jnp.sum(out_hi * out_hi, axis=-1, keepdims=True)) / (2 * half)
            inv = lax.rsqrt(ms + EPS)
            out_lo = out_lo * inv * fg_ref[:, :half]
            out_hi = out_hi * inv * fg_ref[:, half:]
        o_ref[:, :half] = out_lo
        o_ref[:, half:] = out_hi

    for s in range(2):
        @pl.when(slot == s)
        def _():
            step(s)

        @pl.when((slot == s) & (i + 1 == nsteps))
        def _():
            drain(1 - s)


def _combine(x, hp, wts, dest_t, g2, sw13, sw2, final_g, ybuf, seq, final):
    n, d = x.shape
    _, s, lanes = hp.shape
    tm = min(TILE_COMB, seq)
    nt = n // tm
    tiles_per_seq = seq // tm
    slots = dest_t.shape[0]
    return pl.pallas_call(
        functools.partial(_combine_kernel, final=final),
        grid=(nt,),
        in_specs=[
            pl.BlockSpec((slots, tm), lambda i: (0, i), memory_space=pltpu.SMEM),
            pl.BlockSpec((slots, tm), lambda i: (0, jnp.minimum(i + 1, nt - 1)), memory_space=pltpu.SMEM),
            pl.BlockSpec((tm, d), lambda i: (i, 0)),
            pl.BlockSpec((tm, s, lanes), lambda i: (i, 0, 0)),
            pl.BlockSpec((tm, V7X_SUBLANES), lambda i: (i, 0)),
            pl.BlockSpec((1, 1, d), lambda i: (i // tiles_per_seq, 0, 0)),
            pl.BlockSpec(sw13.shape, lambda i: (0, 0)),
            pl.BlockSpec(sw2.shape, lambda i: (0, 0)),
            pl.BlockSpec((1, d), lambda i: (0, 0)),
            pl.BlockSpec(memory_space=pl.ANY),
        ],
        out_specs=pl.BlockSpec((tm, d), lambda i: (i, 0)),
        out_shape=jax.ShapeDtypeStruct((n, d), f32),
        scratch_shapes=[pltpu.VMEM((tm * TOP_K, s, lanes), u32), pltpu.VMEM((tm * TOP_K, s, lanes), u32),
                        pltpu.SemaphoreType.DMA((2,))],
        compiler_params=_cparams(("arbitrary",)),
        name="combine",
    )(dest_t, dest_t, x, hp, wts.T, g2, sw13, sw2, final_g.reshape(1, d), ybuf)


def _moe(x, g, sc, sh, g2, router_w, router_b, w1, w3, w2, layer, sw1, sw3, sw2, final_g, seq, final):
    n, d = x.shape
    ne = router_w.shape[1]
    hp, eidx_t, wts_t, rank_t, cnt = _route(x, g, sc, sh, router_w, router_b, seq)
    counts = cnt[:, 0]
    pcounts = (counts + ROW_BLOCK - 1) // ROW_BLOCK * ROW_BLOCK
    pends = jnp.cumsum(pcounts)
    poffs = pends - pcounts
    nb = -(-(n * TOP_K) // ROW_BLOCK) + ne
    starts = jnp.arange(nb, dtype=i32) * ROW_BLOCK
    before = pends[None, :] <= starts[:, None]
    block_e = jnp.minimum(jnp.sum(before.astype(i32), axis=1), ne - 1)
    nused = (pends[-1:] // ROW_BLOCK).astype(i32)
    real_end = jnp.min(jnp.where(before, jnp.iinfo(jnp.int32).max, (poffs + counts)[None, :]), axis=1)
    nvalid = jnp.where(starts < pends[-1], jnp.clip(real_end - starts, 0, ROW_BLOCK), 0).astype(i32)
    has_rows = pcounts > 0
    last_blocks = jnp.where(has_rows, pends // ROW_BLOCK - 1, -1)
    tail = nused[0] + jnp.arange(nb - (n * TOP_K) // ROW_BLOCK, dtype=i32)
    zero_blocks = jnp.concatenate([last_blocks, jnp.where(tail < nb, tail, -1)]).astype(i32)
    dest_t = _dest(poffs.astype(i32), eidx_t, rank_t)
    xbuf = _dispatch(hp, dest_t, nused, zero_blocks, nb)
    expert_list = jnp.nonzero(has_rows, size=ne, fill_value=ne - 1)[0].astype(i32)
    nexp = jnp.sum(has_rows.astype(i32)).reshape(1)
    expert_ord = jnp.minimum(jnp.sum((before & has_rows[None, :]).astype(i32), axis=1), nexp - 1)
    ybuf = _experts(xbuf, block_e, nused, expert_ord, expert_list, nexp, nvalid, w1, w3, w2, layer)
    sw13 = jnp.concatenate([sw1, sw3], axis=-1).astype(bf16)
    return _combine(x, hp, wts_t, dest_t, g2, sw13, sw2.astype(bf16), final_g, ybuf, seq, final)


def kernel(x, c, ada_w, ada_b, norm_mix_g, norm_ffn_g, ab_w_in, sgu_ln_g, sgu_ln_b, sgu_w, sgu_b, pool_w,
           pool_scale, ab_w_out, cd_w_in, conv3_w, conv3_b, conv31_w, conv31_b, cd_ln_g, cd_ln_b, cd_w_out,
           router_w, router_b, exp_w1, exp_w3, exp_w2, sh_w1, sh_w3, sh_w2, final_g):
    bsz, seq, d = x.shape
    depth = ada_w.shape[0]
    n = bsz * seq
    xf = x.reshape(n, d)
    mod = _ada(c, ada_w, ada_b)
    for l in range(depth):
        sh1, sc1, g1, sh2, sc2, g2 = [mod[l, :, i * d:(i + 1) * d].reshape(bsz, 1, d) for i in range(6)]
        if l % 2 == 0:
            e = l // 2
            p = _norm_mm(xf, norm_mix_g[l], sc1, sh1, ab_w_in[e].astype(bf16), seq)
            xf = _ab_mix(xf, p, g1, sgu_ln_g[e], sgu_ln_b[e], sgu_w[e], sgu_b[e], pool_w[e], pool_scale[e],
                         ab_w_out[e].astype(bf16), bsz, seq)
        else:
            o = l // 2
            p = _norm_mm(xf, norm_mix_g[l], sc1, sh1, cd_w_in[o].astype(bf16), seq)
            xf = _cd_mix(xf, p, g1, conv3_w[o], conv3_b[o], conv31_w[o], conv31_b[o], cd_ln_g[o], cd_ln_b[o],
                         cd_w_out[o].astype(bf16), bsz, seq)
        xf = _moe(xf, norm_ffn_g[l], sc2, sh2, g2, router_w[l], router_b[l], exp_w1, exp_w3, exp_w2, l,
                  sh_w1[l], sh_w3[l], sh_w2[l], final_g, seq, final=(l == depth - 1))
    return xf.reshape(bsz, seq, d)
```

```python
import functools

import jax
import jax.numpy as jnp
from jax import lax
from jax.experimental import pallas as pl
from jax.experimental.pallas import tpu as pltpu

f32 = jnp.float32
bf16 = jnp.bfloat16
u32 = jnp.uint32
i32 = jnp.int32

EPS = 1e-6
CHUNK = 64
SGU_BLOCK = 128
A_HEADS = 8
POOL_WINDOWS = (2, 4, 8, 16)
SHORT_CONV = 3
CONF_CONV = 31
N_EXPERTS = 64
TOP_K = 6
N_GROUPS = 8
TOPK_GROUPS = 4
ROUTED_SCALE = 2.5

V7X_LANES = 128
V7X_SUBLANES = 8
V7X_VMEM_LIMIT = 56 * 1024 * 1024

ROW_BLOCK = 1024
TILE_MM = 1024
TILE_MIX = 512
TILE_ROUTE = 512
EXPERT_CHUNKS = 4
TILE_DISPATCH = 512
TILE_COMB = 256
CONV_ROWS = 128
CONV_COLS = 128
LN_ROWS = 32
NORM_ROWS = 16
HI_MASK = 0xFFFF0000


def _cparams(sem):
    return pltpu.CompilerParams(dimension_semantics=sem, vmem_limit_bytes=V7X_VMEM_LIMIT)


def _pack_halves(v):
    w = v.shape[1] // 2
    bits = pltpu.bitcast(v, u32)
    return (bits[:, :w] >> 16) | bits[:, w:]


def _unpack_halves(words):
    lo = pltpu.bitcast(words << 16, f32)
    hi = pltpu.bitcast(words & jnp.uint32(HI_MASK), f32)
    return lo, hi


def _rows_to_tiles(m):
    s = m.shape[1] // V7X_LANES
    st = jnp.stack([m[:, i * V7X_LANES:(i + 1) * V7X_LANES] for i in range(s)], axis=0)
    return pltpu.einshape("srl->rsl", st)


def _tiles_to_rows(t3):
    xt = pltpu.einshape("rsl->srl", t3)
    return jnp.concatenate([xt[i] for i in range(t3.shape[1])], axis=1)


def _modulated_rmsnorm(x, g, sc, sh):
    ms = jnp.mean(x * x, axis=-1, keepdims=True)
    y = x * lax.rsqrt(ms + EPS) * g
    return y * (1.0 + sc) + sh


def _ada_kernel(c_ref, w_ref, b_ref, o_ref):
    ca = jax.nn.silu(c_ref[...]).astype(bf16)
    o_ref[0] = jnp.dot(ca, w_ref[0].astype(bf16), preferred_element_type=f32) + b_ref[0]


def _ada(c, ada_w, ada_b):
    depth, d, n6 = ada_w.shape
    bsz = c.shape[0]
    tn = 1024
    return pl.pallas_call(
        _ada_kernel,
        grid=(depth, n6 // tn),
        in_specs=[
            pl.BlockSpec((bsz, d), lambda l, j: (0, 0)),
            pl.BlockSpec((1, d, tn), lambda l, j: (l, 0, j)),
            pl.BlockSpec((1, 1, tn), lambda l, j: (l, 0, j)),
        ],
        out_specs=pl.BlockSpec((1, bsz, tn), lambda l, j: (l, 0, j)),
        out_shape=jax.ShapeDtypeStruct((depth, bsz, n6), f32),
        compiler_params=_cparams(("parallel", "parallel")),
        name="ada_mod",
    )(c, ada_w, ada_b.reshape(depth, 1, n6))


def _norm_mm_kernel(x_ref, g_ref, sc_ref, sh_ref, w_ref, o_ref, h_ref):
    @pl.when(pl.program_id(1) == 0)
    def _():
        def body(r, carry):
            rows = pl.ds(pl.multiple_of(r * NORM_ROWS, NORM_ROWS), NORM_ROWS)
            h_ref[rows, :] = _modulated_rmsnorm(x_ref[rows, :], g_ref[...], sc_ref[0], sh_ref[0]).astype(bf16)
            return carry

        lax.fori_loop(0, x_ref.shape[0] // NORM_ROWS, body, 0, unroll=8)

    o_ref[...] = jnp.dot(h_ref[...], w_ref[...], preferred_element_type=f32).astype(o_ref.dtype)


def _norm_mm(x, g, sc, sh, w, seq):
    n, d = x.shape
    nout = w.shape[1]
    tm = min(TILE_MM, seq)
    tn = 1024
    tiles_per_seq = seq // tm
    return pl.pallas_call(
        _norm_mm_kernel,
        grid=(n // tm, nout // tn),
        in_specs=[
            pl.BlockSpec((tm, d), lambda i, j: (i, 0)),
            pl.BlockSpec((1, d), lambda i, j: (0, 0)),
            pl.BlockSpec((1, 1, d), lambda i, j: (i // tiles_per_seq, 0, 0)),
            pl.BlockSpec((1, 1, d), lambda i, j: (i // tiles_per_seq, 0, 0)),
            pl.BlockSpec((d, tn), lambda i, j: (0, j)),
        ],
        out_specs=pl.BlockSpec((tm, tn), lambda i, j: (i, j)),
        out_shape=jax.ShapeDtypeStruct((n, nout), bf16),
        scratch_shapes=[pltpu.VMEM((tm, d), bf16)],
        compiler_params=_cparams(("parallel", "arbitrary")),
        name="norm_in_proj",
    )(x, g.reshape(1, d), sc, sh, w)


def _ab_mix_kernel(x_ref, p_ref, g1_ref, lng_ref, lnb_ref, sw_ref, sbt_ref, pw_ref, ps_ref, wo_ref,
                   o_ref, ext_ref, cat_ref):
    s_idx = pl.program_id(1)
    ts = x_ref.shape[0]
    wa = lng_ref.shape[1]
    hd = wa // A_HEADS
    gb = pw_ref.shape[1]
    blk = SGU_BLOCK

    @pl.when(s_idx == 0)
    def _():
        ext_ref[0:blk, :] = jnp.zeros((blk, ext_ref.shape[1]), bf16)

    @pl.when(s_idx > 0)
    def _():
        ext_ref[0:blk, :] = ext_ref[ts:ts + blk, :]

    ext_ref[blk:, :] = p_ref[:, 2 * wa:]

    ci = lax.broadcasted_iota(i32, (blk, blk), 0) // CHUNK
    cj = lax.broadcasted_iota(i32, (blk, blk), 1) // CHUNK
    causal = ci >= cj
    trow = lax.broadcasted_iota(i32, (blk, 2 * blk), 0) + blk
    tcol = lax.broadcasted_iota(i32, (blk, 2 * blk), 1)
    tpos = lax.broadcasted_iota(i32, (blk, 1), 0)

    def block_body(bi, carry):
        r0 = pl.multiple_of(bi * blk, blk)
        rows = pl.ds(r0, blk)
        v = p_ref[rows, wa:2 * wa].astype(f32)
        mu = jnp.mean(v, axis=-1, keepdims=True)
        vc = v - mu
        var = jnp.mean(vc * vc, axis=-1, keepdims=True)
        vn = (vc * lax.rsqrt(var + EPS) * lng_ref[...] + lnb_ref[...]).astype(bf16)
        for h in range(A_HEADS):
            cols = slice(h * hd, (h + 1) * hd)
            wm = jnp.where(causal, sw_ref[h], 0.0).astype(bf16)
            s = jnp.dot(wm, vn[:, cols], preferred_element_type=f32) + sbt_ref[:, h:h + 1]
            u = p_ref[rows, cols].astype(f32)
            cat_ref[rows, cols] = (u * s).astype(bf16)
        win_rows = pl.ds(r0, 2 * blk)
        pos = (s_idx * ts + r0 + tpos + 1).astype(f32)
        for gi, win in enumerate(POOL_WINDOWS):
            cols = slice(gi * gb, (gi + 1) * gb)
            band = jnp.where((tcol <= trow) & (tcol > trow - win), 1.0, 0.0).astype(bf16)
            window = ext_ref[win_rows, cols]
            sums = jnp.dot(band, window, preferred_element_type=f32)
            xb = window[blk:, :].astype(f32)
            pooled = sums / jnp.minimum(pos, float(win)) - xb
            yb = jnp.dot(pooled.astype(bf16), pw_ref[gi], preferred_element_type=f32) * ps_ref[:, cols]
            cat_ref[rows, wa + gi * gb:wa + (gi + 1) * gb] = yb.astype(bf16)
        return carry

    lax.fori_loop(0, ts // blk, block_body, 0)
    mix = jnp.dot(cat_ref[...], wo_ref[...], preferred_element_type=f32)
    o_ref[...] = x_ref[...] + g1_ref[0] * mix


def _ab_mix(x, p, g1, ln_g, ln_b, sgu_w, sgu_b, pool_w, pool_scale, w_out, bsz, seq):
    n, d = x.shape
    wa = ln_g.shape[0]
    wb = pool_scale.shape[0]
    ts = min(TILE_MIX, seq)
    nst = seq // ts
    const2 = lambda b, s: (0, 0)
    const3 = lambda b, s: (0, 0, 0)
    return pl.pallas_call(
        _ab_mix_kernel,
        grid=(bsz, nst),
        in_specs=[
            pl.BlockSpec((ts, d), lambda b, s: (b * nst + s, 0)),
            pl.BlockSpec((ts, p.shape[1]), lambda b, s: (b * nst + s, 0)),
            pl.BlockSpec((1, 1, d), lambda b, s: (b, 0, 0)),
            pl.BlockSpec((1, wa), const2),
            pl.BlockSpec((1, wa), const2),
            pl.BlockSpec(sgu_w.shape, const3),
            pl.BlockSpec((SGU_BLOCK, A_HEADS), const2),
            pl.BlockSpec(pool_w.shape, const3),
            pl.BlockSpec((1, wb), const2),
            pl.BlockSpec(w_out.shape, const2),
        ],
        out_specs=pl.BlockSpec((ts, d), lambda b, s: (b * nst + s, 0)),
        out_shape=jax.ShapeDtypeStruct((n, d), f32),
        scratch_shapes=[pltpu.VMEM((ts + SGU_BLOCK, wb), bf16), pltpu.VMEM((ts, wa + wb), bf16)],
        compiler_params=_cparams(("parallel", "arbitrary")),
        name="mixer_ab",
    )(x, p, g1, ln_g.reshape(1, wa), ln_b.reshape(1, wa), sgu_w, sgu_b.T, pool_w.astype(bf16),
      pool_scale.reshape(1, wb), w_out)


def _cd_mix_kernel(x_ref, p_ref, g1_ref, c3w_ref, c3b_ref, c31w_ref, c31b_ref, lng_ref, lnb_ref, wo_ref,
                   o_ref, e3_ref, e31_ref, d_ref, cat_ref):
    s_idx = pl.program_id(1)
    ts = x_ref.shape[0]
    wc = c3b_ref.shape[1]
    wd = c31b_ref.shape[1]
    h3 = e3_ref.shape[0] - ts
    h31 = e31_ref.shape[0] - ts

    @pl.when(s_idx == 0)
    def _():
        e3_ref[0:h3, :] = jnp.zeros((h3, wc), f32)
        e31_ref[0:h31, :] = jnp.zeros((h31, wd), f32)

    @pl.when(s_idx > 0)
    def _():
        e3_ref[0:h3, :] = e3_ref[ts:ts + h3, :]
        e31_ref[0:h31, :] = e31_ref[ts:ts + h31, :]

    cg = p_ref[:, wc:2 * wc].astype(f32)
    xc = p_ref[:, 2 * wc:3 * wc].astype(f32)
    e3_ref[h3:, :] = cg * xc
    da = p_ref[:, 3 * wc:3 * wc + wd].astype(f32)
    dg = p_ref[:, 3 * wc + wd:].astype(f32)
    e31_ref[h31:, :] = da * jax.nn.sigmoid(dg)

    cw = CONV_COLS
    rr = CONV_ROWS
    sub = V7X_SUBLANES

    def causal_conv(w_ref, b_ref, e_ref, halo, r0, cols):
        ntaps = w_ref.shape[0]
        acc = jnp.broadcast_to(b_ref[:, cols], (rr, cw))
        for b in range(sub):
            taps = [k for k in range(ntaps) if (halo - (ntaps - 1) + k) % sub == b]
            if not taps:
                continue
            nload = rr + sub if b else rr
            z = None
            for k in taps:
                a8 = (halo - (ntaps - 1) + k) // sub * sub
                t = w_ref[k:k + 1, cols] * e_ref[pl.ds(r0 + a8, nload), cols]
                z = t if z is None else z + t
            if b:
                z = pltpu.roll(z, nload - b, 0)[:rr]
            acc = acc + z
        return acc

    def rows_body(ri, carry):
        r0 = pl.multiple_of(ri * rr, rr)
        rows = pl.ds(r0, rr)
        for c in range(wc // cw):
            cols = slice(c * cw, (c + 1) * cw)
            acc = causal_conv(c3w_ref, c3b_ref, e3_ref, h3, r0, cols)
            cat_ref[rows, cols] = (p_ref[rows, cols].astype(f32) * acc).astype(bf16)
        for c in range(wd // cw):
            cols = slice(c * cw, (c + 1) * cw)
            d_ref[rows, cols] = causal_conv(c31w_ref, c31b_ref, e31_ref, h31, r0, cols)
        for q in range(rr // LN_ROWS):
            qrows = pl.ds(r0 + q * LN_ROWS, LN_ROWS)
            dv = d_ref[qrows, :]
            mu = jnp.mean(dv, axis=-1, keepdims=True)
            dc = dv - mu
            var = jnp.mean(dc * dc, axis=-1, keepdims=True)
            dn = dc * lax.rsqrt(var + EPS) * lng_ref[...] + lnb_ref[...]
            cat_ref[qrows, wc:] = jax.nn.silu(dn).astype(bf16)
        return carry

    lax.fori_loop(0, ts // rr, rows_body, 0)
    mix = jnp.dot(cat_ref[...], wo_ref[...], preferred_element_type=f32)
    o_ref[...] = x_ref[...] + g1_ref[0] * mix


def _cd_mix(x, p, g1, c3w, c3b, c31w, c31b, ln_g, ln_b, w_out, bsz, seq):
    n, d = x.shape
    wc = c3b.shape[0]
    wd = c31b.shape[0]
    ts = min(TILE_MIX, seq)
    nst = seq // ts
    h3 = V7X_SUBLANES
    h31 = 4 * V7X_SUBLANES
    assert h3 >= SHORT_CONV - 1 and h31 >= CONF_CONV - 1
    const2 = lambda b, s: (0, 0)
    return pl.pallas_call(
        _cd_mix_kernel,
        grid=(bsz, nst),
        in_specs=[
            pl.BlockSpec((ts, d), lambda b, s: (b * nst + s, 0)),
            pl.BlockSpec((ts, p.shape[1]), lambda b, s: (b * nst + s, 0)),
            pl.BlockSpec((1, 1, d), lambda b, s: (b, 0, 0)),
            pl.BlockSpec(c3w.shape, const2),
            pl.BlockSpec((1, wc), const2),
            pl.BlockSpec(c31w.shape, const2),
            pl.BlockSpec((1, wd), const2),
            pl.BlockSpec((1, wd), const2),
            pl.BlockSpec((1, wd), const2),
            pl.BlockSpec(w_out.shape, const2),
        ],
        out_specs=pl.BlockSpec((ts, d), lambda b, s: (b * nst + s, 0)),
        out_shape=jax.ShapeDtypeStruct((n, d), f32),
        scratch_shapes=[pltpu.VMEM((ts + h3, wc), f32), pltpu.VMEM((ts + h31, wd), f32),
                        pltpu.VMEM((ts, wd), f32), pltpu.VMEM((ts, wc + wd), bf16)],
        compiler_params=_cparams(("parallel", "arbitrary")),
        name="mixer_cd",
    )(x, p, g1, c3w, c3b.reshape(1, wc), c31w, c31b.reshape(1, wd), ln_g.reshape(1, wd), ln_b.reshape(1, wd),
      w_out)


def _route_kernel(x_ref, g_ref, sc_ref, sh_ref, rwh_ref, rwl_ref, rb_ref,
                  hp_ref, eidx_ref, wts_ref, rank_ref, cnt_ref, carry_ref):
    tm = x_ref.shape[0]
    ne = rwh_ref.shape[0]
    gsz = ne // N_GROUPS
    neg = -jnp.inf

    @pl.when(pl.program_id(0) == 0)
    def _():
        carry_ref[...] = jnp.zeros_like(carry_ref)

    h = _modulated_rmsnorm(x_ref[...], g_ref[...], sc_ref[0], sh_ref[0])
    hh = h.astype(bf16)
    hf = hh.astype(f32)
    hl = (h - hf).astype(bf16)
    hp_ref[...] = _rows_to_tiles(_pack_halves(hf))

    nt = (((1,), (1,)), ((), ()))
    logits = (lax.dot_general(rwh_ref[...], hh, nt, preferred_element_type=f32)
              + lax.dot_general(rwh_ref[...], hl, nt, preferred_element_type=f32)
              + lax.dot_general(rwl_ref[...], hh, nt, preferred_element_type=f32))
    scores = jax.nn.sigmoid(logits)
    choice = scores + rb_ref[...]

    c3 = choice.reshape(N_GROUPS, gsz, tm)
    sub = lax.broadcasted_iota(i32, c3.shape, 1)
    m1 = jnp.max(c3, axis=1, keepdims=True)
    first = jnp.min(jnp.where(c3 == m1, sub, gsz), axis=1, keepdims=True)
    m2 = jnp.max(jnp.where(sub == first, neg, c3), axis=1, keepdims=True)
    gs = m1 + m2
    keep = []
    for gi in range(N_GROUPS):
        beaten = jnp.zeros((1, tm), i32)
        for gj in range(N_GROUPS):
            if gj == gi:
                continue
            wins = (gs[gj] > gs[gi]) | ((gs[gj] == gs[gi]) & (gj < gi))
            beaten = beaten + wins.astype(i32)
        keep.append(beaten < TOPK_GROUPS)
    masked = jnp.concatenate(
        [jnp.where(keep[gi], c3[gi], neg) for gi in range(N_GROUPS)], axis=0)

    eio = lax.broadcasted_iota(i32, (ne, tm), 0)
    cur = masked
    onehot = jnp.zeros((ne, tm), f32)
    idxs, sels, svals = [], [], []
    for _ in range(TOP_K):
        m = jnp.max(cur, axis=0, keepdims=True)
        idx = jnp.min(jnp.where(cur == m, eio, ne), axis=0, keepdims=True)
        sel = eio == idx
        svals.append(jnp.sum(jnp.where(sel, scores, 0.0), axis=0, keepdims=True))
        cur = jnp.where(sel, neg, cur)
        onehot = onehot + sel.astype(f32)
        idxs.append(idx)
        sels.append(sel)
    ssum = svals[0]
    for sv in svals[1:]:
        ssum = ssum + sv

    tr = lax.broadcasted_iota(i32, (tm, tm), 0)
    tc = lax.broadcasted_iota(i32, (tm, tm), 1)
    upper = jnp.where(tr < tc, 1.0, 0.0).astype(bf16)
    before = jnp.dot(onehot.astype(bf16), upper, preferred_element_type=f32) + carry_ref[...]
    new_carry = carry_ref[...] + jnp.sum(onehot, axis=1, keepdims=True)
    carry_ref[...] = new_carry
    cnt_ref[...] = jnp.broadcast_to(new_carry, cnt_ref.shape).astype(i32)

    zrow_i = jnp.zeros((1, tm), i32)
    zrow_f = jnp.zeros((1, tm), f32)
    ranks = [jnp.sum(jnp.where(sel, before, 0.0), axis=0, keepdims=True).astype(i32) for sel in sels]
    pad = V7X_SUBLANES - TOP_K
    eidx_ref[...] = jnp.concatenate(idxs + [zrow_i] * pad, axis=0)
    wts_ref[...] = jnp.concatenate([sv / ssum * ROUTED_SCALE for sv in svals] + [zrow_f] * pad, axis=0)
    rank_ref[...] = jnp.concatenate(ranks + [zrow_i] * pad, axis=0)


def _route(x, g, sc, sh, router_w, router_b, seq):
    n, d = x.shape
    ne = router_w.shape[1]
    tm = min(TILE_ROUTE, seq)
    tiles_per_seq = seq // tm
    rwt = router_w.T
    rwh = rwt.astype(bf16)
    rwl = (rwt - rwh.astype(f32)).astype(bf16)
    slots = V7X_SUBLANES
    return pl.pallas_call(
        _route_kernel,
        grid=(n // tm,),
        in_specs=[
            pl.BlockSpec((tm, d), lambda i: (i, 0)),
            pl.BlockSpec((1, d), lambda i: (0, 0)),
            pl.BlockSpec((1, 1, d), lambda i: (i // tiles_per_seq, 0, 0)),
            pl.BlockSpec((1, 1, d), lambda i: (i // tiles_per_seq, 0, 0)),
            pl.BlockSpec((ne, d), lambda i: (0, 0)),
            pl.BlockSpec((ne, d), lambda i: (0, 0)),
            pl.BlockSpec((ne, 1), lambda i: (0, 0)),
        ],
        out_specs=[
            pl.BlockSpec((tm, d // 2 // V7X_LANES, V7X_LANES), lambda i: (i, 0, 0)),
            pl.BlockSpec((slots, tm), lambda i: (0, i)),
            pl.BlockSpec((slots, tm), lambda i: (0, i)),
            pl.BlockSpec((slots, tm), lambda i: (0, i)),
            pl.BlockSpec((ne, V7X_LANES), lambda i: (0, 0)),
        ],
        out_shape=[
            jax.ShapeDtypeStruct((n, d // 2 // V7X_LANES, V7X_LANES), u32),
            jax.ShapeDtypeStruct((slots, n), i32),
            jax.ShapeDtypeStruct((slots, n), f32),
            jax.ShapeDtypeStruct((slots, n), i32),
            jax.ShapeDtypeStruct((ne, V7X_LANES), i32),
        ],
        scratch_shapes=[pltpu.VMEM((ne, 1), f32)],
        compiler_params=_cparams(("arbitrary",)),
        name="route",
    )(x, g.reshape(1, d), sc, sh, rwh, rwl, router_b.reshape(ne, 1))


def _dest_kernel(poffs_ref, eidx_ref, rank_ref, dest_ref):
    eidx = eidx_ref[...]
    base = jnp.zeros(eidx.shape, i32)
    for e in range(N_EXPERTS):
        base = jnp.where(eidx == e, poffs_ref[e], base)
    dest_ref[...] = base + rank_ref[...]


def _dest(poffs, eidx_t, rank_t):
    slots, n = eidx_t.shape
    td = min(4096, n)
    grid_spec = pltpu.PrefetchScalarGridSpec(
        num_scalar_prefetch=1,
        grid=(n // td,),
        in_specs=[pl.BlockSpec((slots, td), lambda i, po: (0, i)),
                  pl.BlockSpec((slots, td), lambda i, po: (0, i))],
        out_specs=pl.BlockSpec((slots, td), lambda i, po: (0, i)),
    )
    return pl.pallas_call(
        _dest_kernel,
        grid_spec=grid_spec,
        out_shape=jax.ShapeDtypeStruct((slots, n), i32),
        compiler_params=_cparams(("parallel",)),
        name="dest_rows",
    )(poffs, eidx_t, rank_t)


def _dispatch_kernel(nu_ref, zb_ref, dest_ref, hp_ref, xb_hbm, zero_ref, zsem, sem):
    i = pl.program_id(0)
    tm = hp_ref.shape[0]
    rows = zero_ref.shape[0]
    nz = zb_ref.shape[0]

    @pl.when(i == 0)
    def _():
        zero_ref[...] = jnp.zeros_like(zero_ref)

        def zcopy(j):
            return pltpu.make_async_copy(zero_ref, xb_hbm.at[pl.ds(zb_ref[j] * rows, rows)], zsem)

        def zstart(j, carry):
            @pl.when(zb_ref[j] >= 0)
            def _():
                zcopy(j).start()
            return carry

        def zwait(j, carry):
            @pl.when(zb_ref[j] >= 0)
            def _():
                zcopy(j).wait()
            return carry

        lax.fori_loop(0, nz, zstart, 0)
        lax.fori_loop(0, nz, zwait, 0)

    for t in range(tm):
        for k in range(TOP_K):
            pltpu.make_async_copy(hp_ref.at[t], xb_hbm.at[dest_ref[k, t]], sem).start(priority=k % 2)
    for k in range(TOP_K):
        pltpu.make_async_copy(hp_ref, xb_hbm.at[pl.ds(0, tm)], sem).wait()


def _dispatch(hp, dest_t, nused, zero_blocks, nb):
    n, s, lanes = hp.shape
    tm = min(TILE_DISPATCH, n)
    grid_spec = pltpu.PrefetchScalarGridSpec(
        num_scalar_prefetch=2,
        grid=(n // tm,),
        in_specs=[
            pl.BlockSpec((dest_t.shape[0], tm), lambda i, nu, zb: (0, i), memory_space=pltpu.SMEM),
            pl.BlockSpec((tm, s, lanes), lambda i, nu, zb: (i, 0, 0)),
        ],
        out_specs=pl.BlockSpec(memory_space=pl.ANY),
        scratch_shapes=[pltpu.VMEM((ROW_BLOCK, s, lanes), u32), pltpu.SemaphoreType.DMA(()),
                        pltpu.SemaphoreType.DMA(())],
    )
    return pl.pallas_call(
        _dispatch_kernel,
        grid_spec=grid_spec,
        out_shape=jax.ShapeDtypeStruct((nb * ROW_BLOCK, s, lanes), u32),
        compiler_params=_cparams(("arbitrary",)),
        name="dispatch",
    )(nused, zero_blocks, dest_t, hp)


def _expert_kernel(be_ref, nu_ref, eord_ref, elist_ref, nexp_ref, nv_ref, x_ref, w1_hbm, w3_hbm, w2_hbm, y_ref,
                   w1_st, w3_st, w2_st, w13_s, w2_s, wsem, *, layer):
    b = pl.program_id(0)
    nused = nu_ref[0]
    nvalid = nv_ref[b]
    ff = w2_s.shape[0]
    rows = x_ref.shape[0]
    cr = rows // EXPERT_CHUNKS

    def weight_copies(j, slot):
        e = elist_ref[j]
        return [pltpu.make_async_copy(src.at[layer, e], dst.at[slot], wsem.at[slot])
                for src, dst in ((w1_hbm, w1_st), (w3_hbm, w3_st), (w2_hbm, w2_st))]

    @pl.when(b == 0)
    def _():
        for cp in weight_copies(0, 0):
            cp.start()

    j = eord_ref[b]
    first = (b == 0) | (be_ref[b] != be_ref[jnp.maximum(b - 1, 0)])

    @pl.when(first & (b < nused))
    def _():
        slot = j % 2
        for cp in weight_copies(j, slot):
            cp.wait()

        @pl.when(j + 1 < nexp_ref[0])
        def _():
            for cp in weight_copies(j + 1, 1 - slot):
                cp.start()

        w13_s[:, :ff] = w1_st[slot].astype(bf16)
        w13_s[:, ff:] = w3_st[slot].astype(bf16)
        w2_s[...] = w2_st[slot].astype(bf16)

    def ffn_chunk(c):
        rs = slice(c * cr, (c + 1) * cr)
        lo, hi = _unpack_halves(_tiles_to_rows(x_ref[rs]))
        x = jnp.concatenate([lo, hi], axis=1).astype(bf16)
        hcat = jnp.dot(x, w13_s[...], preferred_element_type=f32)
        a = jax.nn.silu(hcat[:, :ff]) * hcat[:, ff:]
        y = jnp.dot(a.astype(bf16), w2_s[...], preferred_element_type=f32)
        y_ref[rs] = _rows_to_tiles(_pack_halves(y.astype(bf16).astype(f32)))

    @pl.when((b < nused) & (nvalid > rows - cr))
    def _():
        for c in range(EXPERT_CHUNKS):
            ffn_chunk(c)

    for c in range(EXPERT_CHUNKS):
        @pl.when((b < nused) & (nvalid <= rows - cr) & (nvalid > c * cr))
        def _():
            ffn_chunk(c)

        @pl.when((b >= nused) | (nvalid <= c * cr))
        def _():
            y_ref[c * cr:(c + 1) * cr] = jnp.zeros((cr,) + y_ref.shape[1:], y_ref.dtype)


def _experts(xbuf, block_e, nused, expert_ord, expert_list, nexp, nvalid, w1, w3, w2, layer):
    nrows, s, lanes = xbuf.shape
    nb = block_e.shape[0]
    _, ne, d, ff = w1.shape
    rows = ROW_BLOCK
    grid_spec = pltpu.PrefetchScalarGridSpec(
        num_scalar_prefetch=6,
        grid=(nb,),
        in_specs=[
            pl.BlockSpec((rows, s, lanes), lambda b, be, nu, eo, el, nx, nv: (jnp.minimum(b, nu[0] - 1), 0, 0)),
            pl.BlockSpec(memory_space=pl.ANY),
            pl.BlockSpec(memory_space=pl.ANY),
            pl.BlockSpec(memory_space=pl.ANY),
        ],
        out_specs=pl.BlockSpec((rows, s, lanes), lambda b, be, nu, eo, el, nx, nv: (b, 0, 0)),
        scratch_shapes=[pltpu.VMEM((2, d, ff), f32), pltpu.VMEM((2, d, ff), f32), pltpu.VMEM((2, ff, d), f32),
                        pltpu.VMEM((d, 2 * ff), bf16), pltpu.VMEM((ff, d), bf16),
                        pltpu.SemaphoreType.DMA((2,))],
    )
    return pl.pallas_call(
        functools.partial(_expert_kernel, layer=layer),
        grid_spec=grid_spec,
        out_shape=jax.ShapeDtypeStruct((nrows, s, lanes), u32),
        compiler_params=_cparams(("arbitrary",)),
        name="experts",
    )(block_e, nused, expert_ord, expert_list, nexp, nvalid, xbuf, w1, w3, w2)


def _combine_kernel(dc_ref, dn_ref, x_ref, hp_ref, w_ref, g2_ref, sw13_ref, sw2_ref, fg_ref, y_hbm,
                    o_ref, buf0_ref, buf1_ref, sem, *, final):
    i = pl.program_id(0)
    nsteps = pl.num_programs(0)
    slot = i % 2
    tm = x_ref.shape[0]
    half = hp_ref.shape[1] * hp_ref.shape[2]
    sff = sw2_ref.shape[0]
    bufs = (buf0_ref, buf1_ref)

    def row_copy(d_ref, s, t, k):
        return pltpu.make_async_copy(y_hbm.at[d_ref[k, t]], bufs[s].at[k * tm + t], sem.at[s])

    def drain(s):
        for k in range(TOP_K):
            pltpu.make_async_copy(y_hbm.at[pl.ds(0, tm)], bufs[s].at[pl.ds(k * tm, tm)], sem.at[s]).wait()

    @pl.when(i == 0)
    def _():
        def body(t, carry):
            for k in range(TOP_K):
                row_copy(dc_ref, 0, t, k).start(priority=k % 2)
            return carry

        lax.fori_loop(0, tm, body, 0, unroll=2)

    def step(s):
        drain(s)
        for t in range(tm):
            for k in range(TOP_K):
                row_copy(dn_ref, 1 - s, t, k).start(priority=k % 2)

        lo, hi = _unpack_halves(_tiles_to_rows(hp_ref[...]))
        hx = jnp.concatenate([lo, hi], axis=1).astype(bf16)
        hcat = jnp.dot(hx, sw13_ref[...], preferred_element_type=f32)
        a = jax.nn.silu(hcat[:, :sff]) * hcat[:, sff:]
        shared = jnp.dot(a.astype(bf16), sw2_ref[...], preferred_element_type=f32)

        w = w_ref[...]
        acc_lo = shared[:, :half]
        acc_hi = shared[:, half:]
        for k in range(TOP_K):
            ylo, yhi = _unpack_halves(_tiles_to_rows(bufs[s][k * tm:(k + 1) * tm]))
            acc_lo = acc_lo + w[:, k:k + 1] * ylo
            acc_hi = acc_hi + w[:, k:k + 1] * yhi
        g2 = g2_ref[0]
        out_lo = x_ref[:, :half] + g2[:, :half] * acc_lo
        out_hi = x_ref[:, half:] + g2[:, half:] * acc_hi
        if final:
            ms = (jnp.sum(out_lo * out_lo, axis=-1, keepdims=True)
                  + jnp.sum(out_hi * out_hi, axis=-1, keepdims=True)) / (2 * half)
            inv = lax.rsqrt(ms + EPS)
            out_lo = out_lo * inv * fg_ref[:, :half]
            out_hi = out_hi * inv * fg_ref[:, half:]
        o_ref[:, :half] = out_lo
        o_ref[:, half:] = out_hi

    for s in range(2):
        @pl.when(slot == s)
        def _():
            step(s)

        @pl.when((slot == s) & (i + 1 == nsteps))
        def _():
            drain(1 - s)


def _combine(x, hp, wts, dest_t, g2, sw13, sw2, final_g, ybuf, seq, final):
    n, d = x.shape
    _, s, lanes = hp.shape
    tm = min(TILE_COMB, seq)
    nt = n // tm
    tiles_per_seq = seq // tm
    slots = dest_t.shape[0]
    return pl.pallas_call(
        functools.partial(_combine_kernel, final=final),
        grid=(nt,),
        in_specs=[
            pl.BlockSpec((slots, tm), lambda i: (0, i), memory_space=pltpu.SMEM),
            pl.BlockSpec((slots, tm), lambda i: (0, jnp.minimum(i + 1, nt - 1)), memory_space=pltpu.SMEM),
            pl.BlockSpec((tm, d), lambda i: (i, 0)),
            pl.BlockSpec((tm, s, lanes), lambda i: (i, 0, 0)),
            pl.BlockSpec((tm, V7X_SUBLANES), lambda i: (i, 0)),
            pl.BlockSpec((1, 1, d), lambda i: (i // tiles_per_seq, 0, 0)),
            pl.BlockSpec(sw13.shape, lambda i: (0, 0)),
            pl.BlockSpec(sw2.shape, lambda i: (0, 0)),
            pl.BlockSpec((1, d), lambda i: (0, 0)),
            pl.BlockSpec(memory_space=pl.ANY),
        ],
        out_specs=pl.BlockSpec((tm, d), lambda i: (i, 0)),
        out_shape=jax.ShapeDtypeStruct((n, d), f32),
        scratch_shapes=[pltpu.VMEM((tm * TOP_K, s, lanes), u32), pltpu.VMEM((tm * TOP_K, s, lanes), u32),
                        pltpu.SemaphoreType.DMA((2,))],
        compiler_params=_cparams(("arbitrary",)),
        name="combine",
    )(dest_t, dest_t, x, hp, wts.T, g2, sw13, sw2, final_g.reshape(1, d), ybuf)


def _moe(x, g, sc, sh, g2, router_w, router_b, w1, w3, w2, layer, sw1, sw3, sw2, final_g, seq, final):
    n, d = x.shape
    ne = router_w.shape[1]
    hp, eidx_t, wts_t, rank_t, cnt = _route(x, g, sc, sh, router_w, router_b, seq)
    counts = cnt[:, 0]
    pcounts = (counts + ROW_BLOCK - 1) // ROW_BLOCK * ROW_BLOCK
    pends = jnp.cumsum(pcounts)
    poffs = pends - pcounts
    nb = -(-(n * TOP_K) // ROW_BLOCK) + ne
    starts = jnp.arange(nb, dtype=i32) * ROW_BLOCK
    before = pends[None, :] <= starts[:, None]
    block_e = jnp.minimum(jnp.sum(before.astype(i32), axis=1), ne - 1)
    nused = (pends[-1:] // ROW_BLOCK).astype(i32)
    real_end = jnp.min(jnp.where(before, jnp.iinfo(jnp.int32).max, (poffs + counts)[None, :]), axis=1)
    nvalid = jnp.where(starts < pends[-1], jnp.clip(real_end - starts, 0, ROW_BLOCK), 0).astype(i32)
    has_rows = pcounts > 0
    last_blocks = jnp.where(has_rows, pends // ROW_BLOCK - 1, -1)
    tail = nused[0] + jnp.arange(nb - (n * TOP_K) // ROW_BLOCK, dtype=i32)
    zero_blocks = jnp.concatenate([last_blocks, jnp.where(tail < nb, tail, -1)]).astype(i32)
    dest_t = _dest(poffs.astype(i32), eidx_t, rank_t)
    xbuf = _dispatch(hp, dest_t, nused, zero_blocks, nb)
    expert_list = jnp.nonzero(has_rows, size=ne, fill_value=ne - 1)[0].astype(i32)
    nexp = jnp.sum(has_rows.astype(i32)).reshape(1)
    expert_ord = jnp.minimum(jnp.sum((before & has_rows[None, :]).astype(i32), axis=1), nexp - 1)
    ybuf = _experts(xbuf, block_e, nused, expert_ord, expert_list, nexp, nvalid, w1, w3, w2, layer)
    sw13 = jnp.concatenate([sw1, sw3], axis=-1).astype(bf16)
    return _combine(x, hp, wts_t, dest_t, g2, sw13, sw2.astype(bf16), final_g, ybuf, seq, final)


def kernel(x, c, ada_w, ada_b, norm_mix_g, norm_ffn_g, ab_w_in, sgu_ln_g, sgu_ln_b, sgu_w, sgu_b, pool_w,
           pool_scale, ab_w_out, cd_w_in, conv3_w, conv3_b, conv31_w, conv31_b, cd_ln_g, cd_ln_b, cd_w_out,
           router_w, router_b, exp_w1, exp_w3, exp_w2, sh_w1, sh_w3, sh_w2, final_g):
    bsz, seq, d = x.shape
    depth = ada_w.shape[0]
    n = bsz * seq
    xf = x.reshape(n, d)
    mod = _ada(c, ada_w, ada_b)
    for l in range(depth):
        sh1, sc1, g1, sh2, sc2, g2 = [mod[l, :, i * d:(i + 1) * d].reshape(bsz, 1, d) for i in range(6)]
        if l % 2 == 0:
            e = l // 2
            p = _norm_mm(xf, norm_mix_g[l], sc1, sh1, ab_w_in[e].astype(bf16), seq)
            xf = _ab_mix(xf, p, g1, sgu_ln_g[e], sgu_ln_b[e], sgu_w[e], sgu_b[e], pool_w[e], pool_scale[e],
                         ab_w_out[e].astype(bf16), bsz, seq)
        else:
            o = l // 2
            p = _norm_mm(xf, norm_mix_g[l], sc1, sh1, cd_w_in[o].astype(bf16), seq)
            xf = _cd_mix(xf, p, g1, conv3_w[o], conv3_b[o], conv31_w[o], conv31_b[o], cd_ln_g[o], cd_ln_b[o],
                         cd_w_out[o].astype(bf16), bsz, seq)
        xf = _moe(xf, norm_ffn_g[l], sc2, sh2, g2, router_w[l], router_b[l], exp_w1, exp_w3, exp_w2, l,
                  sh_w1[l], sh_w3[l], sh_w2[l], final_g, seq, final=(l == depth - 1))
    return xf.reshape(bsz, seq, d)
```

```python
import functools

import jax
import jax.numpy as jnp
from jax import lax
from jax.experimental import pallas as pl
from jax.experimental.pallas import tpu as pltpu

f32 = jnp.float32
bf16 = jnp.bfloat16
u32 = jnp.uint32
i32 = jnp.int32

EPS = 1e-6
CHUNK = 64
SGU_BLOCK = 128
A_HEADS = 8
POOL_WINDOWS = (2, 4, 8, 16)
SHORT_CONV = 3
CONF_CONV = 31
N_EXPERTS = 64
TOP_K = 6
N_GROUPS = 8
TOPK_GROUPS = 4
ROUTED_SCALE = 2.5

V7X_LANES = 128
V7X_SUBLANES = 8
V7X_VMEM_LIMIT = 56 * 1024 * 1024

ROW_BLOCK = 1024
TILE_MM = 1024
TILE_N = 1024
TILE_DEST = 4096
TILE_MIX = 512
TILE_ROUTE = 512
EXPERT_CHUNKS = 4
TILE_DISPATCH = 1024
TILE_COMB = 256
CONV_ROWS = 128
CONV_COLS = 128
LN_ROWS = 32
NORM_ROWS = 16
HI_MASK = 0xFFFF0000


def _cparams(sem):
    return pltpu.CompilerParams(dimension_semantics=sem, vmem_limit_bytes=V7X_VMEM_LIMIT)


def _pack_halves(v):
    w = v.shape[1] // 2
    bits = pltpu.bitcast(v, u32)
    return (bits[:, :w] >> 16) | bits[:, w:]


def _unpack_halves(words):
    lo = pltpu.bitcast(words << 16, f32)
    hi = pltpu.bitcast(words & jnp.uint32(HI_MASK), f32)
    return lo, hi


def _rows_to_tiles(m):
    s = m.shape[1] // V7X_LANES
    st = jnp.stack([m[:, i * V7X_LANES:(i + 1) * V7X_LANES] for i in range(s)], axis=0)
    return pltpu.einshape("srl->rsl", st)


def _tiles_to_rows(t3):
    xt = pltpu.einshape("rsl->srl", t3)
    return jnp.concatenate([xt[i] for i in range(t3.shape[1])], axis=1)


def _modulated_rmsnorm(x, g, sc, sh):
    ms = jnp.mean(x * x, axis=-1, keepdims=True)
    y = x * lax.rsqrt(ms + EPS) * g
    return y * (1.0 + sc) + sh


def _ada_kernel(c_ref, w_ref, b_ref, o_ref):
    ca = jax.nn.silu(c_ref[...]).astype(bf16)
    o_ref[0] = jnp.dot(ca, w_ref[0].astype(bf16), preferred_element_type=f32) + b_ref[0]


def _ada(c, ada_w, ada_b):
    depth, d, n6 = ada_w.shape
    bsz = c.shape[0]
    tn = TILE_N
    return pl.pallas_call(
        _ada_kernel,
        grid=(depth, n6 // tn),
        in_specs=[
            pl.BlockSpec((bsz, d), lambda l, j: (0, 0)),
            pl.BlockSpec((1, d, tn), lambda l, j: (l, 0, j)),
            pl.BlockSpec((1, 1, tn), lambda l, j: (l, 0, j)),
        ],
        out_specs=pl.BlockSpec((1, bsz, tn), lambda l, j: (l, 0, j)),
        out_shape=jax.ShapeDtypeStruct((depth, bsz, n6), f32),
        compiler_params=_cparams(("parallel", "parallel")),
        name="ada_mod",
    )(c, ada_w, ada_b.reshape(depth, 1, n6))


def _norm_mm_kernel(x_ref, g_ref, sc_ref, sh_ref, w_ref, o_ref, h_ref):
    @pl.when(pl.program_id(1) == 0)
    def _():
        def body(r, carry):
            rows = pl.ds(pl.multiple_of(r * NORM_ROWS, NORM_ROWS), NORM_ROWS)
            h_ref[rows, :] = _modulated_rmsnorm(x_ref[rows, :], g_ref[...], sc_ref[0], sh_ref[0]).astype(bf16)
            return carry

        lax.fori_loop(0, x_ref.shape[0] // NORM_ROWS, body, 0, unroll=8)

    o_ref[...] = jnp.dot(h_ref[...], w_ref[...], preferred_element_type=f32).astype(o_ref.dtype)


def _norm_mm(x, g, sc, sh, w, seq):
    n, d = x.shape
    nout = w.shape[1]
    tm = min(TILE_MM, seq)
    tn = TILE_N
    tiles_per_seq = seq // tm
    return pl.pallas_call(
        _norm_mm_kernel,
        grid=(n // tm, nout // tn),
        in_specs=[
            pl.BlockSpec((tm, d), lambda i, j: (i, 0)),
            pl.BlockSpec((1, d), lambda i, j: (0, 0)),
            pl.BlockSpec((1, 1, d), lambda i, j: (i // tiles_per_seq, 0, 0)),
            pl.BlockSpec((1, 1, d), lambda i, j: (i // tiles_per_seq, 0, 0)),
            pl.BlockSpec((d, tn), lambda i, j: (0, j)),
        ],
        out_specs=pl.BlockSpec((tm, tn), lambda i, j: (i, j)),
        out_shape=jax.ShapeDtypeStruct((n, nout), bf16),
        scratch_shapes=[pltpu.VMEM((tm, d), bf16)],
        compiler_params=_cparams(("parallel", "arbitrary")),
        name="norm_in_proj",
    )(x, g.reshape(1, d), sc, sh, w)


def _ab_mix_kernel(x_ref, p_ref, g1_ref, lng_ref, lnb_ref, sw_ref, sbt_ref, pw_ref, ps_ref, wo_ref,
                   o_ref, ext_ref, cat_ref):
    s_idx = pl.program_id(1)
    ts = x_ref.shape[0]
    wa = lng_ref.shape[1]
    hd = wa // A_HEADS
    gb = pw_ref.shape[1]
    blk = SGU_BLOCK

    @pl.when(s_idx == 0)
    def _():
        ext_ref[0:blk, :] = jnp.zeros((blk, ext_ref.shape[1]), bf16)

    @pl.when(s_idx > 0)
    def _():
        ext_ref[0:blk, :] = ext_ref[ts:ts + blk, :]

    ext_ref[blk:, :] = p_ref[:, 2 * wa:]

    ci = lax.broadcasted_iota(i32, (blk, blk), 0) // CHUNK
    cj = lax.broadcasted_iota(i32, (blk, blk), 1) // CHUNK
    causal = ci >= cj
    trow = lax.broadcasted_iota(i32, (blk, 2 * blk), 0) + blk
    tcol = lax.broadcasted_iota(i32, (blk, 2 * blk), 1)
    tpos = lax.broadcasted_iota(i32, (blk, 1), 0)

    def block_body(bi, carry):
        r0 = pl.multiple_of(bi * blk, blk)
        rows = pl.ds(r0, blk)
        v = p_ref[rows, wa:2 * wa].astype(f32)
        mu = jnp.mean(v, axis=-1, keepdims=True)
        vc = v - mu
        var = jnp.mean(vc * vc, axis=-1, keepdims=True)
        vn = (vc * lax.rsqrt(var + EPS) * lng_ref[...] + lnb_ref[...]).astype(bf16)
        for h in range(A_HEADS):
            cols = slice(h * hd, (h + 1) * hd)
            wm = jnp.where(causal, sw_ref[h], 0.0).astype(bf16)
            s = jnp.dot(wm, vn[:, cols], preferred_element_type=f32) + sbt_ref[:, h:h + 1]
            u = p_ref[rows, cols].astype(f32)
            cat_ref[rows, cols] = (u * s).astype(bf16)
        win_rows = pl.ds(r0, 2 * blk)
        pos = (s_idx * ts + r0 + tpos + 1).astype(f32)
        for gi, win in enumerate(POOL_WINDOWS):
            cols = slice(gi * gb, (gi + 1) * gb)
            band = jnp.where((tcol <= trow) & (tcol > trow - win), 1.0, 0.0).astype(bf16)
            window = ext_ref[win_rows, cols]
            sums = jnp.dot(band, window, preferred_element_type=f32)
            xb = window[blk:, :].astype(f32)
            pooled = sums / jnp.minimum(pos, float(win)) - xb
            yb = jnp.dot(pooled.astype(bf16), pw_ref[gi], preferred_element_type=f32) * ps_ref[:, cols]
            cat_ref[rows, wa + gi * gb:wa + (gi + 1) * gb] = yb.astype(bf16)
        return carry

    lax.fori_loop(0, ts // blk, block_body, 0)
    mix = jnp.dot(cat_ref[...], wo_ref[...], preferred_element_type=f32)
    o_ref[...] = x_ref[...] + g1_ref[0] * mix


def _ab_mix(x, p, g1, ln_g, ln_b, sgu_w, sgu_b, pool_w, pool_scale, w_out, bsz, seq):
    n, d = x.shape
    wa = ln_g.shape[0]
    wb = pool_scale.shape[0]
    ts = min(TILE_MIX, seq)
    nst = seq // ts
    const2 = lambda b, s: (0, 0)
    const3 = lambda b, s: (0, 0, 0)
    return pl.pallas_call(
        _ab_mix_kernel,
        grid=(bsz, nst),
        in_specs=[
            pl.BlockSpec((ts, d), lambda b, s: (b * nst + s, 0)),
            pl.BlockSpec((ts, p.shape[1]), lambda b, s: (b * nst + s, 0)),
            pl.BlockSpec((1, 1, d), lambda b, s: (b, 0, 0)),
            pl.BlockSpec((1, wa), const2),
            pl.BlockSpec((1, wa), const2),
            pl.BlockSpec(sgu_w.shape, const3),
            pl.BlockSpec((SGU_BLOCK, A_HEADS), const2),
            pl.BlockSpec(pool_w.shape, const3),
            pl.BlockSpec((1, wb), const2),
            pl.BlockSpec(w_out.shape, const2),
        ],
        out_specs=pl.BlockSpec((ts, d), lambda b, s: (b * nst + s, 0)),
        out_shape=jax.ShapeDtypeStruct((n, d), f32),
        scratch_shapes=[pltpu.VMEM((ts + SGU_BLOCK, wb), bf16), pltpu.VMEM((ts, wa + wb), bf16)],
        compiler_params=_cparams(("parallel", "arbitrary")),
        name="mixer_ab",
    )(x, p, g1, ln_g.reshape(1, wa), ln_b.reshape(1, wa), sgu_w, sgu_b.T, pool_w.astype(bf16),
      pool_scale.reshape(1, wb), w_out)


def _cd_mix_kernel(x_ref, p_ref, g1_ref, c3w_ref, c3b_ref, c31w_ref, c31b_ref, lng_ref, lnb_ref, wo_ref,
                   o_ref, e3_ref, e31_ref, d_ref, cat_ref):
    s_idx = pl.program_id(1)
    ts = x_ref.shape[0]
    wc = c3b_ref.shape[1]
    wd = c31b_ref.shape[1]
    h3 = e3_ref.shape[0] - ts
    h31 = e31_ref.shape[0] - ts

    @pl.when(s_idx == 0)
    def _():
        e3_ref[0:h3, :] = jnp.zeros((h3, wc), f32)
        e31_ref[0:h31, :] = jnp.zeros((h31, wd), f32)

    @pl.when(s_idx > 0)
    def _():
        e3_ref[0:h3, :] = e3_ref[ts:ts + h3, :]
        e31_ref[0:h31, :] = e31_ref[ts:ts + h31, :]

    cg = p_ref[:, wc:2 * wc].astype(f32)
    xc = p_ref[:, 2 * wc:3 * wc].astype(f32)
    e3_ref[h3:, :] = cg * xc
    da = p_ref[:, 3 * wc:3 * wc + wd].astype(f32)
    dg = p_ref[:, 3 * wc + wd:].astype(f32)
    e31_ref[h31:, :] = da * jax.nn.sigmoid(dg)

    cw = CONV_COLS
    rr = CONV_ROWS
    sub = V7X_SUBLANES

    def causal_conv(w_ref, b_ref, e_ref, halo, r0, cols):
        ntaps = w_ref.shape[0]
        acc = jnp.broadcast_to(b_ref[:, cols], (rr, cw))
        for b in range(sub):
            taps = [k for k in range(ntaps) if (halo - (ntaps - 1) + k) % sub == b]
            if not taps:
                continue
            nload = rr + sub if b else rr
            z = None
            for k in taps:
                a8 = (halo - (ntaps - 1) + k) // sub * sub
                t = w_ref[k:k + 1, cols] * e_ref[pl.ds(r0 + a8, nload), cols]
                z = t if z is None else z + t
            if b:
                z = pltpu.roll(z, nload - b, 0)[:rr]
            acc = acc + z
        return acc

    def conv_rows(ri):
        r0 = pl.multiple_of(ri * rr, rr)
        rows = pl.ds(r0, rr)
        for c in range(wc // cw):
            cols = slice(c * cw, (c + 1) * cw)
            acc = causal_conv(c3w_ref, c3b_ref, e3_ref, h3, r0, cols)
            cat_ref[rows, cols] = (p_ref[rows, cols].astype(f32) * acc).astype(bf16)
        for c in range(wd // cw):
            cols = slice(c * cw, (c + 1) * cw)
            d_ref[rows, cols] = causal_conv(c31w_ref, c31b_ref, e31_ref, h31, r0, cols)
        for q in range(rr // LN_ROWS):
            qrows = pl.ds(r0 + q * LN_ROWS, LN_ROWS)
            dv = d_ref[qrows, :]
            mu = jnp.mean(dv, axis=-1, keepdims=True)
            dc = dv - mu
            var = jnp.mean(dc * dc, axis=-1, keepdims=True)
            dn = dc * lax.rsqrt(var + EPS) * lng_ref[...] + lnb_ref[...]
            cat_ref[qrows, wc:] = jax.nn.silu(dn).astype(bf16)

    def rows_body(ri, carry):
        conv_rows(ri)
        return carry

    lax.fori_loop(0, ts // rr, rows_body, 0)
    mix = jnp.dot(cat_ref[...], wo_ref[...], preferred_element_type=f32)
    o_ref[...] = x_ref[...] + g1_ref[0] * mix


def _cd_mix(x, p, g1, c3w, c3b, c31w, c31b, ln_g, ln_b, w_out, bsz, seq):
    n, d = x.shape
    wc = c3b.shape[0]
    wd = c31b.shape[0]
    ts = min(TILE_MIX, seq)
    nst = seq // ts
    h3 = V7X_SUBLANES
    h31 = 4 * V7X_SUBLANES
    assert h3 >= SHORT_CONV - 1 and h31 >= CONF_CONV - 1
    const2 = lambda b, s: (0, 0)
    return pl.pallas_call(
        _cd_mix_kernel,
        grid=(bsz, nst),
        in_specs=[
            pl.BlockSpec((ts, d), lambda b, s: (b * nst + s, 0)),
            pl.BlockSpec((ts, p.shape[1]), lambda b, s: (b * nst + s, 0)),
            pl.BlockSpec((1, 1, d), lambda b, s: (b, 0, 0)),
            pl.BlockSpec(c3w.shape, const2),
            pl.BlockSpec((1, wc), const2),
            pl.BlockSpec(c31w.shape, const2),
            pl.BlockSpec((1, wd), const2),
            pl.BlockSpec((1, wd), const2),
            pl.BlockSpec((1, wd), const2),
            pl.BlockSpec(w_out.shape, const2),
        ],
        out_specs=pl.BlockSpec((ts, d), lambda b, s: (b * nst + s, 0)),
        out_shape=jax.ShapeDtypeStruct((n, d), f32),
        scratch_shapes=[pltpu.VMEM((ts + h3, wc), f32), pltpu.VMEM((ts + h31, wd), f32),
                        pltpu.VMEM((ts, wd), f32), pltpu.VMEM((ts, wc + wd), bf16)],
        compiler_params=_cparams(("parallel", "arbitrary")),
        name="mixer_cd",
    )(x, p, g1, c3w, c3b.reshape(1, wc), c31w, c31b.reshape(1, wd), ln_g.reshape(1, wd), ln_b.reshape(1, wd),
      w_out)


def _route_kernel(x_ref, g_ref, sc_ref, sh_ref, rwh_ref, rwl_ref, rb_ref,
                  hp_ref, eidx_ref, wts_ref, rank_ref, cnt_ref, carry_ref):
    tm = x_ref.shape[0]
    ne = rwh_ref.shape[0]
    gsz = ne // N_GROUPS
    neg = -jnp.inf

    @pl.when(pl.program_id(0) == 0)
    def _():
        carry_ref[...] = jnp.zeros_like(carry_ref)

    h = _modulated_rmsnorm(x_ref[...], g_ref[...], sc_ref[0], sh_ref[0])
    hh = h.astype(bf16)
    hf = hh.astype(f32)
    hl = (h - hf).astype(bf16)
    hp_ref[...] = _rows_to_tiles(_pack_halves(hf))

    nt = (((1,), (1,)), ((), ()))
    logits = (lax.dot_general(rwh_ref[...], hh, nt, preferred_element_type=f32)
              + lax.dot_general(rwh_ref[...], hl, nt, preferred_element_type=f32)
              + lax.dot_general(rwl_ref[...], hh, nt, preferred_element_type=f32))
    scores = jax.nn.sigmoid(logits)
    choice = scores + rb_ref[...]

    c3 = choice.reshape(N_GROUPS, gsz, tm)
    sub = lax.broadcasted_iota(i32, c3.shape, 1)
    m1 = jnp.max(c3, axis=1, keepdims=True)
    first = jnp.min(jnp.where(c3 == m1, sub, gsz), axis=1, keepdims=True)
    m2 = jnp.max(jnp.where(sub == first, neg, c3), axis=1, keepdims=True)
    gs = m1 + m2
    keep = []
    for gi in range(N_GROUPS):
        beaten = jnp.zeros((1, tm), i32)
        for gj in range(N_GROUPS):
            if gj == gi:
                continue
            wins = (gs[gj] > gs[gi]) | ((gs[gj] == gs[gi]) & (gj < gi))
            beaten = beaten + wins.astype(i32)
        keep.append(beaten < TOPK_GROUPS)
    masked = jnp.concatenate(
        [jnp.where(keep[gi], c3[gi], neg) for gi in range(N_GROUPS)], axis=0)

    eio = lax.broadcasted_iota(i32, (ne, tm), 0)
    cur = masked
    onehot = jnp.zeros((ne, tm), f32)
    idxs, sels, svals = [], [], []
    for _ in range(TOP_K):
        m = jnp.max(cur, axis=0, keepdims=True)
        idx = jnp.min(jnp.where(cur == m, eio, ne), axis=0, keepdims=True)
        sel = eio == idx
        svals.append(jnp.sum(jnp.where(sel, scores, 0.0), axis=0, keepdims=True))
        cur = jnp.where(sel, neg, cur)
        onehot = onehot + sel.astype(f32)
        idxs.append(idx)
        sels.append(sel)
    ssum = svals[0]
    for sv in svals[1:]:
        ssum = ssum + sv

    tr = lax.broadcasted_iota(i32, (tm, tm), 0)
    tc = lax.broadcasted_iota(i32, (tm, tm), 1)
    upper = jnp.where(tr < tc, 1.0, 0.0).astype(bf16)
    before = jnp.dot(onehot.astype(bf16), upper, preferred_element_type=f32) + carry_ref[...]
    new_carry = carry_ref[...] + jnp.sum(onehot, axis=1, keepdims=True)
    carry_ref[...] = new_carry
    cnt_ref[...] = jnp.broadcast_to(new_carry, cnt_ref.shape).astype(i32)

    zrow_i = jnp.zeros((1, tm), i32)
    zrow_f = jnp.zeros((1, tm), f32)
    ranks = [jnp.sum(jnp.where(sel, before, 0.0), axis=0, keepdims=True).astype(i32) for sel in sels]
    pad = V7X_SUBLANES - TOP_K
    eidx_ref[...] = jnp.concatenate(idxs + [zrow_i] * pad, axis=0)
    wts_ref[...] = jnp.concatenate([sv / ssum * ROUTED_SCALE for sv in svals] + [zrow_f] * pad, axis=0)
    rank_ref[...] = jnp.concatenate(ranks + [zrow_i] * pad, axis=0)


def _route(x, g, sc, sh, router_w, router_b, seq):
    n, d = x.shape
    ne = router_w.shape[1]
    tm = min(TILE_ROUTE, seq)
    tiles_per_seq = seq // tm
    rwt = router_w.T
    rwh = rwt.astype(bf16)
    rwl = (rwt - rwh.astype(f32)).astype(bf16)
    slots = V7X_SUBLANES
    return pl.pallas_call(
        _route_kernel,
        grid=(n // tm,),
        in_specs=[
            pl.BlockSpec((tm, d), lambda i: (i, 0)),
            pl.BlockSpec((1, d), lambda i: (0, 0)),
            pl.BlockSpec((1, 1, d), lambda i: (i // tiles_per_seq, 0, 0)),
            pl.BlockSpec((1, 1, d), lambda i: (i // tiles_per_seq, 0, 0)),
            pl.BlockSpec((ne, d), lambda i: (0, 0)),
            pl.BlockSpec((ne, d), lambda i: (0, 0)),
            pl.BlockSpec((ne, 1), lambda i: (0, 0)),
        ],
        out_specs=[
            pl.BlockSpec((tm, d // 2 // V7X_LANES, V7X_LANES), lambda i: (i, 0, 0)),
            pl.BlockSpec((slots, tm), lambda i: (0, i)),
            pl.BlockSpec((slots, tm), lambda i: (0, i)),
            pl.BlockSpec((slots, tm), lambda i: (0, i)),
            pl.BlockSpec((ne, V7X_LANES), lambda i: (0, 0)),
        ],
        out_shape=[
            jax.ShapeDtypeStruct((n, d // 2 // V7X_LANES, V7X_LANES), u32),
            jax.ShapeDtypeStruct((slots, n), i32),
            jax.ShapeDtypeStruct((slots, n), f32),
            jax.ShapeDtypeStruct((slots, n), i32),
            jax.ShapeDtypeStruct((ne, V7X_LANES), i32),
        ],
        scratch_shapes=[pltpu.VMEM((ne, 1), f32)],
        compiler_params=_cparams(("arbitrary",)),
        name="route",
    )(x, g.reshape(1, d), sc, sh, rwh, rwl, router_b.reshape(ne, 1))


def _dest_kernel(poffs_ref, eidx_ref, rank_ref, dest_ref):
    eidx = eidx_ref[...]
    base = jnp.zeros(eidx.shape, i32)
    for e in range(N_EXPERTS):
        base = jnp.where(eidx == e, poffs_ref[e], base)
    dest_ref[...] = base + rank_ref[...]


def _dest(poffs, eidx_t, rank_t):
    slots, n = eidx_t.shape
    td = min(TILE_DEST, n)
    grid_spec = pltpu.PrefetchScalarGridSpec(
        num_scalar_prefetch=1,
        grid=(n // td,),
        in_specs=[pl.BlockSpec((slots, td), lambda i, po: (0, i)),
                  pl.BlockSpec((slots, td), lambda i, po: (0, i))],
        out_specs=pl.BlockSpec((slots, td), lambda i, po: (0, i)),
    )
    return pl.pallas_call(
        _dest_kernel,
        grid_spec=grid_spec,
        out_shape=jax.ShapeDtypeStruct((slots, n), i32),
        compiler_params=_cparams(("parallel",)),
        name="dest_rows",
    )(poffs, eidx_t, rank_t)


def _dispatch_kernel(nu_ref, zb_ref, dest_ref, hp_ref, xb_hbm, zero_ref, zsem, sem):
    i = pl.program_id(0)
    tm = hp_ref.shape[0]
    rows = zero_ref.shape[0]
    nz = zb_ref.shape[0]

    @pl.when(i == 0)
    def _():
        zero_ref[...] = jnp.zeros_like(zero_ref)

        def zcopy(j):
            return pltpu.make_async_copy(zero_ref, xb_hbm.at[pl.ds(zb_ref[j] * rows, rows)], zsem)

        def zstart(j, carry):
            @pl.when(zb_ref[j] >= 0)
            def _():
                zcopy(j).start()
            return carry

        def zwait(j, carry):
            @pl.when(zb_ref[j] >= 0)
            def _():
                zcopy(j).wait()
            return carry

        lax.fori_loop(0, nz, zstart, 0)
        lax.fori_loop(0, nz, zwait, 0)

    for t in range(tm):
        for k in range(TOP_K):
            pltpu.make_async_copy(hp_ref.at[t], xb_hbm.at[dest_ref[k, t]], sem).start(priority=k % 2)
    for k in range(TOP_K):
        pltpu.make_async_copy(hp_ref, xb_hbm.at[pl.ds(0, tm)], sem).wait()


def _dispatch(hp, dest_t, nused, zero_blocks, nb):
    n, s, lanes = hp.shape
    tm = min(TILE_DISPATCH, n)
    grid_spec = pltpu.PrefetchScalarGridSpec(
        num_scalar_prefetch=2,
        grid=(n // tm,),
        in_specs=[
            pl.BlockSpec((dest_t.shape[0], tm), lambda i, nu, zb: (0, i), memory_space=pltpu.SMEM),
            pl.BlockSpec((tm, s, lanes), lambda i, nu, zb: (i, 0, 0)),
        ],
        out_specs=pl.BlockSpec(memory_space=pl.ANY),
        scratch_shapes=[pltpu.VMEM((ROW_BLOCK, s, lanes), u32), pltpu.SemaphoreType.DMA(()),
                        pltpu.SemaphoreType.DMA(())],
    )
    return pl.pallas_call(
        _dispatch_kernel,
        grid_spec=grid_spec,
        out_shape=jax.ShapeDtypeStruct((nb * ROW_BLOCK, s, lanes), u32),
        compiler_params=_cparams(("arbitrary",)),
        name="dispatch",
    )(nused, zero_blocks, dest_t, hp)


def _expert_kernel(be_ref, nu_ref, eord_ref, elist_ref, nexp_ref, nv_ref, x_ref, w1_hbm, w3_hbm, w2_hbm, y_ref,
                   w1_st, w3_st, w2_st, w13_s, w2_s, wsem, *, layer):
    b = pl.program_id(0)
    nused = nu_ref[0]
    nvalid = nv_ref[b]
    ff = w2_s.shape[0]
    rows = x_ref.shape[0]
    cr = rows // EXPERT_CHUNKS

    def weight_copies(j, slot):
        e = elist_ref[j]
        return [pltpu.make_async_copy(src.at[layer, e], dst.at[slot], wsem.at[slot])
                for src, dst in ((w1_hbm, w1_st), (w3_hbm, w3_st), (w2_hbm, w2_st))]

    @pl.when(b == 0)
    def _():
        for cp in weight_copies(0, 0):
            cp.start()

    j = eord_ref[b]
    first = (b == 0) | (be_ref[b] != be_ref[jnp.maximum(b - 1, 0)])

    @pl.when(first & (b < nused))
    def _():
        slot = j % 2
        for cp in weight_copies(j, slot):
            cp.wait()

        @pl.when(j + 1 < nexp_ref[0])
        def _():
            for cp in weight_copies(j + 1, 1 - slot):
                cp.start()

        w13_s[:, :ff] = w1_st[slot].astype(bf16)
        w13_s[:, ff:] = w3_st[slot].astype(bf16)
        w2_s[...] = w2_st[slot].astype(bf16)

    def ffn_chunk(c):
        rs = slice(c * cr, (c + 1) * cr)
        lo, hi = _unpack_halves(_tiles_to_rows(x_ref[rs]))
        x = jnp.concatenate([lo, hi], axis=1).astype(bf16)
        hcat = jnp.dot(x, w13_s[...], preferred_element_type=f32)
        a = jax.nn.silu(hcat[:, :ff]) * hcat[:, ff:]
        y = jnp.dot(a.astype(bf16), w2_s[...], preferred_element_type=f32)
        y_ref[rs] = _rows_to_tiles(_pack_halves(y.astype(bf16).astype(f32)))

    @pl.when((b < nused) & (nvalid > rows - cr))
    def _():
        for c in range(EXPERT_CHUNKS):
            ffn_chunk(c)

    for c in range(EXPERT_CHUNKS):
        @pl.when((b < nused) & (nvalid <= rows - cr) & (nvalid > c * cr))
        def _():
            ffn_chunk(c)

        @pl.when((b >= nused) | (nvalid <= c * cr))
        def _():
            y_ref[c * cr:(c + 1) * cr] = jnp.zeros((cr,) + y_ref.shape[1:], y_ref.dtype)


def _experts(xbuf, block_e, nused, expert_ord, expert_list, nexp, nvalid, w1, w3, w2, layer):
    nrows, s, lanes = xbuf.shape
    nb = block_e.shape[0]
    _, ne, d, ff = w1.shape
    rows = ROW_BLOCK
    grid_spec = pltpu.PrefetchScalarGridSpec(
        num_scalar_prefetch=6,
        grid=(nb,),
        in_specs=[
            pl.BlockSpec((rows, s, lanes), lambda b, be, nu, eo, el, nx, nv: (jnp.minimum(b, nu[0] - 1), 0, 0)),
            pl.BlockSpec(memory_space=pl.ANY),
            pl.BlockSpec(memory_space=pl.ANY),
            pl.BlockSpec(memory_space=pl.ANY),
        ],
        out_specs=pl.BlockSpec((rows, s, lanes), lambda b, be, nu, eo, el, nx, nv: (b, 0, 0)),
        scratch_shapes=[pltpu.VMEM((2, d, ff), f32), pltpu.VMEM((2, d, ff), f32), pltpu.VMEM((2, ff, d), f32),
                        pltpu.VMEM((d, 2 * ff), bf16), pltpu.VMEM((ff, d), bf16),
                        pltpu.SemaphoreType.DMA((2,))],
    )
    return pl.pallas_call(
        functools.partial(_expert_kernel, layer=layer),
        grid_spec=grid_spec,
        out_shape=jax.ShapeDtypeStruct((nrows, s, lanes), u32),
        compiler_params=_cparams(("arbitrary",)),
        name="experts",
    )(block_e, nused, expert_ord, expert_list, nexp, nvalid, xbuf, w1, w3, w2)


def _combine_kernel(dc_ref, dn_ref, x_ref, hp_ref, w_ref, g2_ref, sw13_ref, sw2_ref, fg_ref, y_hbm,
                    o_ref, buf0_ref, buf1_ref, sem, *, final):
    i = pl.program_id(0)
    nsteps = pl.num_programs(0)
    slot = i % 2
    tm = x_ref.shape[0]
    half = hp_ref.shape[1] * hp_ref.shape[2]
    sff = sw2_ref.shape[0]
    bufs = (buf0_ref, buf1_ref)

    def row_copy(d_ref, s, t, k):
        return pltpu.make_async_copy(y_hbm.at[d_ref[k, t]], bufs[s].at[k * tm + t], sem.at[s])

    def drain(s):
        for k in range(TOP_K):
            pltpu.make_async_copy(y_hbm.at[pl.ds(0, tm)], bufs[s].at[pl.ds(k * tm, tm)], sem.at[s]).wait()

    @pl.when(i == 0)
    def _():
        def body(t, carry):
            for k in range(TOP_K):
                row_copy(dc_ref, 0, t, k).start(priority=k % 2)
            return carry

        lax.fori_loop(0, tm, body, 0, unroll=2)

    def step(s):
        drain(s)
        for t in range(tm):
            for k in range(TOP_K):
                row_copy(dn_ref, 1 - s, t, k).start(priority=k % 2)

        lo, hi = _unpack_halves(_tiles_to_rows(hp_ref[...]))
        hx = jnp.concatenate([lo, hi], axis=1).astype(bf16)
        hcat = jnp.dot(hx, sw13_ref[...], preferred_element_type=f32)
        a = jax.nn.silu(hcat[:, :sff]) * hcat[:, sff:]
        shared = jnp.dot(a.astype(bf16), sw2_ref[...], preferred_element_type=f32)

        w = w_ref[...]
        acc_lo = shared[:, :half]
        acc_hi = shared[:, half:]
        for k in range(TOP_K):
            ylo, yhi = _unpack_halves(_tiles_to_rows(bufs[s][k * tm:(k + 1) * tm]))
            acc_lo = acc_lo + w[:, k:k + 1] * ylo
            acc_hi = acc_hi + w[:, k:k + 1] * yhi
        g2 = g2_ref[0]
        out_lo = x_ref[:, :half] + g2[:, :half] * acc_lo
        out_hi = x_ref[:, half:] + g2[:, half:] * acc_hi
        if final:
            ms = (jnp.sum(out_lo * out_lo, axis=-1, keepdims=True)
                  + jnp.sum(out_hi * out_hi, axis=-1, keepdims=True)) / (2 * half)
            inv = lax.rsqrt(ms + EPS)
            out_lo = out_lo * inv * fg_ref[:, :half]
            out_hi = out_hi * inv * fg_ref[:, half:]
        o_ref[:, :half] = out_lo
        o_ref[:, half:] = out_hi

    for s in range(2):
        @pl.when(slot == s)
        def _():
            step(s)

        @pl.when((slot == s) & (i + 1 == nsteps))
        def _():
            drain(1 - s)


def _combine(x, hp, wts, dest_t, g2, sw13, sw2, final_g, ybuf, seq, final):
    n, d = x.shape
    _, s, lanes = hp.shape
    tm = min(TILE_COMB, seq)
    nt = n // tm
    tiles_per_seq = seq // tm
    slots = dest_t.shape[0]
    return pl.pallas_call(
        functools.partial(_combine_kernel, final=final),
        grid=(nt,),
        in_specs=[
            pl.BlockSpec((slots, tm), lambda i: (0, i), memory_space=pltpu.SMEM),
            pl.BlockSpec((slots, tm), lambda i: (0, jnp.minimum(i + 1, nt - 1)), memory_space=pltpu.SMEM),
            pl.BlockSpec((tm, d), lambda i: (i, 0)),
            pl.BlockSpec((tm, s, lanes), lambda i: (i, 0, 0)),
            pl.BlockSpec((tm, V7X_SUBLANES), lambda i: (i, 0)),
            pl.BlockSpec((1, 1, d), lambda i: (i // tiles_per_seq, 0, 0)),
            pl.BlockSpec(sw13.shape, lambda i: (0, 0)),
            pl.BlockSpec(sw2.shape, lambda i: (0, 0)),
            pl.BlockSpec((1, d), lambda i: (0, 0)),
            pl.BlockSpec(memory_space=pl.ANY),
        ],
        out_specs=pl.BlockSpec((tm, d), lambda i: (i, 0)),
        out_shape=jax.ShapeDtypeStruct((n, d), f32),
        scratch_shapes=[pltpu.VMEM((tm * TOP_K, s, lanes), u32), pltpu.VMEM((tm * TOP_K, s, lanes), u32),
                        pltpu.SemaphoreType.DMA((2,))],
        compiler_params=_cparams(("arbitrary",)),
        name="combine",
    )(dest_t, dest_t, x, hp, wts.T, g2, sw13, sw2, final_g.reshape(1, d), ybuf)


def _moe(x, g, sc, sh, g2, router_w, router_b, w1, w3, w2, layer, sw1, sw3, sw2, final_g, seq, final):
    n, d = x.shape
    ne = router_w.shape[1]
    hp, eidx_t, wts_t, rank_t, cnt = _route(x, g, sc, sh, router_w, router_b, seq)
    counts = cnt[:, 0]
    pcounts = (counts + ROW_BLOCK - 1) // ROW_BLOCK * ROW_BLOCK
    pends = jnp.cumsum(pcounts)
    poffs = pends - pcounts
    nb = -(-(n * TOP_K) // ROW_BLOCK) + ne
    starts = jnp.arange(nb, dtype=i32) * ROW_BLOCK
    before = pends[None, :] <= starts[:, None]
    block_e = jnp.minimum(jnp.sum(before.astype(i32), axis=1), ne - 1)
    nused = (pends[-1:] // ROW_BLOCK).astype(i32)
    real_end = jnp.min(jnp.where(before, jnp.iinfo(jnp.int32).max, (poffs + counts)[None, :]), axis=1)
    nvalid = jnp.where(starts < pends[-1], jnp.clip(real_end - starts, 0, ROW_BLOCK), 0).astype(i32)
    has_rows = pcounts > 0
    last_blocks = jnp.where(has_rows, pends // ROW_BLOCK - 1, -1)
    tail = nused[0] + jnp.arange(nb - (n * TOP_K) // ROW_BLOCK, dtype=i32)
    zero_blocks = jnp.concatenate([last_blocks, jnp.where(tail < nb, tail, -1)]).astype(i32)
    dest_t = _dest(poffs.astype(i32), eidx_t, rank_t)
    xbuf = _dispatch(hp, dest_t, nused, zero_blocks, nb)
    expert_list = jnp.nonzero(has_rows, size=ne, fill_value=ne - 1)[0].astype(i32)
    nexp = jnp.sum(has_rows.astype(i32)).reshape(1)
    expert_ord = jnp.minimum(jnp.sum((before & has_rows[None, :]).astype(i32), axis=1), nexp - 1)
    ybuf = _experts(xbuf, block_e, nused, expert_ord, expert_list, nexp, nvalid, w1, w3, w2, layer)
    sw13 = jnp.concatenate([sw1, sw3], axis=-1).astype(bf16)
    return _combine(x, hp, wts_t, dest_t, g2, sw13, sw2.astype(bf16), final_g, ybuf, seq, final)


def kernel(x, c, ada_w, ada_b, norm_mix_g, norm_ffn_g, ab_w_in, sgu_ln_g, sgu_ln_b, sgu_w, sgu_b, pool_w,
           pool_scale, ab_w_out, cd_w_in, conv3_w, conv3_b, conv31_w, conv31_b, cd_ln_g, cd_ln_b, cd_w_out,
           router_w, router_b, exp_w1, exp_w3, exp_w2, sh_w1, sh_w3, sh_w2, final_g):
    bsz, seq, d = x.shape
    depth = ada_w.shape[0]
    n = bsz * seq
    xf = x.reshape(n, d)
    mod = _ada(c, ada_w, ada_b)
    for l in range(depth):
        sh1, sc1, g1, sh2, sc2, g2 = [mod[l, :, i * d:(i + 1) * d].reshape(bsz, 1, d) for i in range(6)]
        if l % 2 == 0:
            e = l // 2
            p = _norm_mm(xf, norm_mix_g[l], sc1, sh1, ab_w_in[e].astype(bf16), seq)
            xf = _ab_mix(xf, p, g1, sgu_ln_g[e], sgu_ln_b[e], sgu_w[e], sgu_b[e], pool_w[e], pool_scale[e],
                         ab_w_out[e].astype(bf16), bsz, seq)
        else:
            o = l // 2
            p = _norm_mm(xf, norm_mix_g[l], sc1, sh1, cd_w_in[o].astype(bf16), seq)
            xf = _cd_mix(xf, p, g1, conv3_w[o], conv3_b[o], conv31_w[o], conv31_b[o], cd_ln_g[o], cd_ln_b[o],
                         cd_w_out[o].astype(bf16), bsz, seq)
        xf = _moe(xf, norm_ffn_g[l], sc2, sh2, g2, router_w[l], router_b[l], exp_w1, exp_w3, exp_w2, l,
                  sh_w1[l], sh_w3[l], sh_w2[l], final_g, seq, final=(l == depth - 1))
    return xf.reshape(bsz, seq, d)
```

```python
import functools

import jax
import jax.numpy as jnp
from jax import lax
from jax.experimental import pallas as pl
from jax.experimental.pallas import tpu as pltpu

f32 = jnp.float32
bf16 = jnp.bfloat16
u32 = jnp.uint32
i32 = jnp.int32

EPS = 1e-6
CHUNK = 64
SGU_BLOCK = 128
A_HEADS = 8
POOL_WINDOWS = (2, 4, 8, 16)
SHORT_CONV = 3
CONF_CONV = 31
N_EXPERTS = 64
TOP_K = 6
N_GROUPS = 8
TOPK_GROUPS = 4
ROUTED_SCALE = 2.5

V7X_LANES = 128
V7X_SUBLANES = 8
V7X_VMEM_LIMIT = 56 * 1024 * 1024

ROW_BLOCK = 1024
TILE_MM = 1024
TILE_N = 1024
IN_PROJ_COL_STEPS = 2
TILE_DEST = 4096
TILE_MIX = 512
TILE_ROUTE = 512
EXPERT_CHUNKS = 4
TILE_DISPATCH = 1024
TILE_COMB = 256
CONV_ROWS = 128
CONV_COLS = 128
LN_ROWS = 32
NORM_ROWS = 16
HI_MASK = 0xFFFF0000


def _cparams(sem):
    return pltpu.CompilerParams(dimension_semantics=sem, vmem_limit_bytes=V7X_VMEM_LIMIT)


def _pack_halves(v):
    w = v.shape[1] // 2
    bits = pltpu.bitcast(v, u32)
    return (bits[:, :w] >> 16) | bits[:, w:]


def _unpack_halves(words):
    lo = pltpu.bitcast(words << 16, f32)
    hi = pltpu.bitcast(words & jnp.uint32(HI_MASK), f32)
    return lo, hi


def _rows_to_tiles(m):
    s = m.shape[1] // V7X_LANES
    st = jnp.stack([m[:, i * V7X_LANES:(i + 1) * V7X_LANES] for i in range(s)], axis=0)
    return pltpu.einshape("srl->rsl", st)


def _tiles_to_rows(t3):
    xt = pltpu.einshape("rsl->srl", t3)
    return jnp.concatenate([xt[i] for i in range(t3.shape[1])], axis=1)


def _modulated_rmsnorm(x, g, sc, sh):
    ms = jnp.mean(x * x, axis=-1, keepdims=True)
    y = x * lax.rsqrt(ms + EPS) * g
    return y * (1.0 + sc) + sh


def _ada_kernel(c_ref, w_ref, b_ref, o_ref):
    ca = jax.nn.silu(c_ref[...]).astype(bf16)
    o_ref[0] = jnp.dot(ca, w_ref[0].astype(bf16), preferred_element_type=f32) + b_ref[0]


def _ada(c, ada_w, ada_b):
    depth, d, n6 = ada_w.shape
    bsz = c.shape[0]
    tn = TILE_N
    return pl.pallas_call(
        _ada_kernel,
        grid=(depth, n6 // tn),
        in_specs=[
            pl.BlockSpec((bsz, d), lambda l, j: (0, 0)),
            pl.BlockSpec((1, d, tn), lambda l, j: (l, 0, j)),
            pl.BlockSpec((1, 1, tn), lambda l, j: (l, 0, j)),
        ],
        out_specs=pl.BlockSpec((1, bsz, tn), lambda l, j: (l, 0, j)),
        out_shape=jax.ShapeDtypeStruct((depth, bsz, n6), f32),
        compiler_params=_cparams(("parallel", "parallel")),
        name="ada_mod",
    )(c, ada_w, ada_b.reshape(depth, 1, n6))


def _norm_mm_kernel(x_ref, g_ref, sc_ref, sh_ref, w_ref, o_ref, h_ref):
    @pl.when(pl.program_id(1) == 0)
    def _():
        def body(r, carry):
            rows = pl.ds(pl.multiple_of(r * NORM_ROWS, NORM_ROWS), NORM_ROWS)
            h_ref[rows, :] = _modulated_rmsnorm(x_ref[rows, :], g_ref[...], sc_ref[0], sh_ref[0]).astype(bf16)
            return carry

        lax.fori_loop(0, x_ref.shape[0] // NORM_ROWS, body, 0, unroll=8)

    o_ref[...] = jnp.dot(h_ref[...], w_ref[...], preferred_element_type=f32).astype(o_ref.dtype)


def _norm_mm(x, g, sc, sh, w, seq):
    n, d = x.shape
    nout = w.shape[1]
    tm = min(TILE_MM, seq)
    tn = nout // IN_PROJ_COL_STEPS
    tiles_per_seq = seq // tm
    return pl.pallas_call(
        _norm_mm_kernel,
        grid=(n // tm, nout // tn),
        in_specs=[
            pl.BlockSpec((tm, d), lambda i, j: (i, 0)),
            pl.BlockSpec((1, d), lambda i, j: (0, 0)),
            pl.BlockSpec((1, 1, d), lambda i, j: (i // tiles_per_seq, 0, 0)),
            pl.BlockSpec((1, 1, d), lambda i, j: (i // tiles_per_seq, 0, 0)),
            pl.BlockSpec((d, tn), lambda i, j: (0, j)),
        ],
        out_specs=pl.BlockSpec((tm, tn), lambda i, j: (i, j)),
        out_shape=jax.ShapeDtypeStruct((n, nout), bf16),
        scratch_shapes=[pltpu.VMEM((tm, d), bf16)],
        compiler_params=_cparams(("parallel", "arbitrary")),
        name="norm_in_proj",
    )(x, g.reshape(1, d), sc, sh, w)


def _ab_mix_kernel(x_ref, p_ref, g1_ref, lng_ref, lnb_ref, sw_ref, sbt_ref, pw_ref, ps_ref, wo_ref,
                   o_ref, ext_ref, cat_ref):
    s_idx = pl.program_id(1)
    ts = x_ref.shape[0]
    wa = lng_ref.shape[1]
    hd = wa // A_HEADS
    gb = pw_ref.shape[1]
    blk = SGU_BLOCK

    @pl.when(s_idx == 0)
    def _():
        ext_ref[0:blk, :] = jnp.zeros((blk, ext_ref.shape[1]), bf16)

    @pl.when(s_idx > 0)
    def _():
        ext_ref[0:blk, :] = ext_ref[ts:ts + blk, :]

    ext_ref[blk:, :] = p_ref[:, 2 * wa:]

    ci = lax.broadcasted_iota(i32, (blk, blk), 0) // CHUNK
    cj = lax.broadcasted_iota(i32, (blk, blk), 1) // CHUNK
    causal = ci >= cj
    trow = lax.broadcasted_iota(i32, (blk, 2 * blk), 0) + blk
    tcol = lax.broadcasted_iota(i32, (blk, 2 * blk), 1)
    tpos = lax.broadcasted_iota(i32, (blk, 1), 0)

    def block_body(bi, carry):
        r0 = pl.multiple_of(bi * blk, blk)
        rows = pl.ds(r0, blk)
        v = p_ref[rows, wa:2 * wa].astype(f32)
        mu = jnp.mean(v, axis=-1, keepdims=True)
        vc = v - mu
        var = jnp.mean(vc * vc, axis=-1, keepdims=True)
        vn = (vc * lax.rsqrt(var + EPS) * lng_ref[...] + lnb_ref[...]).astype(bf16)
        for h in range(A_HEADS):
            cols = slice(h * hd, (h + 1) * hd)
            wm = jnp.where(causal, sw_ref[h], 0.0).astype(bf16)
            s = jnp.dot(wm, vn[:, cols], preferred_element_type=f32) + sbt_ref[:, h:h + 1]
            u = p_ref[rows, cols].astype(f32)
            cat_ref[rows, cols] = (u * s).astype(bf16)
        win_rows = pl.ds(r0, 2 * blk)
        pos = (s_idx * ts + r0 + tpos + 1).astype(f32)
        for gi, win in enumerate(POOL_WINDOWS):
            cols = slice(gi * gb, (gi + 1) * gb)
            band = jnp.where((tcol <= trow) & (tcol > trow - win), 1.0, 0.0).astype(bf16)
            window = ext_ref[win_rows, cols]
            sums = jnp.dot(band, window, preferred_element_type=f32)
            xb = window[blk:, :].astype(f32)
            pooled = sums / jnp.minimum(pos, float(win)) - xb
            yb = jnp.dot(pooled.astype(bf16), pw_ref[gi], preferred_element_type=f32) * ps_ref[:, cols]
            cat_ref[rows, wa + gi * gb:wa + (gi + 1) * gb] = yb.astype(bf16)
        return carry

    lax.fori_loop(0, ts // blk, block_body, 0)
    mix = jnp.dot(cat_ref[...], wo_ref[...], preferred_element_type=f32)
    o_ref[...] = x_ref[...] + g1_ref[0] * mix


def _ab_mix(x, p, g1, ln_g, ln_b, sgu_w, sgu_b, pool_w, pool_scale, w_out, bsz, seq):
    n, d = x.shape
    wa = ln_g.shape[0]
    wb = pool_scale.shape[0]
    ts = min(TILE_MIX, seq)
    nst = seq // ts
    const2 = lambda b, s: (0, 0)
    const3 = lambda b, s: (0, 0, 0)
    return pl.pallas_call(
        _ab_mix_kernel,
        grid=(bsz, nst),
        in_specs=[
            pl.BlockSpec((ts, d), lambda b, s: (b * nst + s, 0)),
            pl.BlockSpec((ts, p.shape[1]), lambda b, s: (b * nst + s, 0)),
            pl.BlockSpec((1, 1, d), lambda b, s: (b, 0, 0)),
            pl.BlockSpec((1, wa), const2),
            pl.BlockSpec((1, wa), const2),
            pl.BlockSpec(sgu_w.shape, const3),
            pl.BlockSpec((SGU_BLOCK, A_HEADS), const2),
            pl.BlockSpec(pool_w.shape, const3),
            pl.BlockSpec((1, wb), const2),
            pl.BlockSpec(w_out.shape, const2),
        ],
        out_specs=pl.BlockSpec((ts, d), lambda b, s: (b * nst + s, 0)),
        out_shape=jax.ShapeDtypeStruct((n, d), f32),
        scratch_shapes=[pltpu.VMEM((ts + SGU_BLOCK, wb), bf16), pltpu.VMEM((ts, wa + wb), bf16)],
        compiler_params=_cparams(("parallel", "arbitrary")),
        name="mixer_ab",
    )(x, p, g1, ln_g.reshape(1, wa), ln_b.reshape(1, wa), sgu_w, sgu_b.T, pool_w.astype(bf16),
      pool_scale.reshape(1, wb), w_out)


def _cd_mix_kernel(x_ref, p_ref, g1_ref, c3w_ref, c3b_ref, c31w_ref, c31b_ref, lng_ref, lnb_ref, wo_ref,
                   o_ref, e3_ref, e31_ref, d_ref, cat_ref):
    s_idx = pl.program_id(1)
    ts = x_ref.shape[0]
    wc = c3b_ref.shape[1]
    wd = c31b_ref.shape[1]
    h3 = e3_ref.shape[0] - ts
    h31 = e31_ref.shape[0] - ts

    @pl.when(s_idx == 0)
    def _():
        e3_ref[0:h3, :] = jnp.zeros((h3, wc), f32)
        e31_ref[0:h31, :] = jnp.zeros((h31, wd), f32)

    @pl.when(s_idx > 0)
    def _():
        e3_ref[0:h3, :] = e3_ref[ts:ts + h3, :]
        e31_ref[0:h31, :] = e31_ref[ts:ts + h31, :]

    cg = p_ref[:, wc:2 * wc].astype(f32)
    xc = p_ref[:, 2 * wc:3 * wc].astype(f32)
    e3_ref[h3:, :] = cg * xc
    da = p_ref[:, 3 * wc:3 * wc + wd].astype(f32)
    dg = p_ref[:, 3 * wc + wd:].astype(f32)
    e31_ref[h31:, :] = da * jax.nn.sigmoid(dg)

    cw = CONV_COLS
    rr = CONV_ROWS
    sub = V7X_SUBLANES

    def causal_conv(w_ref, b_ref, e_ref, halo, r0, cols):
        ntaps = w_ref.shape[0]
        acc = jnp.broadcast_to(b_ref[:, cols], (rr, cw))
        for b in range(sub):
            taps = [k for k in range(ntaps) if (halo - (ntaps - 1) + k) % sub == b]
            if not taps:
                continue
            nload = rr + sub if b else rr
            z = None
            for k in taps:
                a8 = (halo - (ntaps - 1) + k) // sub * sub
                t = w_ref[k:k + 1, cols] * e_ref[pl.ds(r0 + a8, nload), cols]
                z = t if z is None else z + t
            if b:
                z = pltpu.roll(z, nload - b, 0)[:rr]
            acc = acc + z
        return acc

    def conv_rows(ri):
        r0 = pl.multiple_of(ri * rr, rr)
        rows = pl.ds(r0, rr)
        for c in range(wc // cw):
            cols = slice(c * cw, (c + 1) * cw)
            acc = causal_conv(c3w_ref, c3b_ref, e3_ref, h3, r0, cols)
            cat_ref[rows, cols] = (p_ref[rows, cols].astype(f32) * acc).astype(bf16)
        for c in range(wd // cw):
            cols = slice(c * cw, (c + 1) * cw)
            d_ref[rows, cols] = causal_conv(c31w_ref, c31b_ref, e31_ref, h31, r0, cols)
        for q in range(rr // LN_ROWS):
            qrows = pl.ds(r0 + q * LN_ROWS, LN_ROWS)
            dv = d_ref[qrows, :]
            mu = jnp.mean(dv, axis=-1, keepdims=True)
            dc = dv - mu
            var = jnp.mean(dc * dc, axis=-1, keepdims=True)
            dn = dc * lax.rsqrt(var + EPS) * lng_ref[...] + lnb_ref[...]
            cat_ref[qrows, wc:] = jax.nn.silu(dn).astype(bf16)

    def rows_body(ri, carry):
        conv_rows(ri)
        return carry

    lax.fori_loop(0, ts // rr, rows_body, 0)
    mix = jnp.dot(cat_ref[...], wo_ref[...], preferred_element_type=f32)
    o_ref[...] = x_ref[...] + g1_ref[0] * mix


def _cd_mix(x, p, g1, c3w, c3b, c31w, c31b, ln_g, ln_b, w_out, bsz, seq):
    n, d = x.shape
    wc = c3b.shape[0]
    wd = c31b.shape[0]
    ts = min(TILE_MIX, seq)
    nst = seq // ts
    h3 = V7X_SUBLANES
    h31 = 4 * V7X_SUBLANES
    assert h3 >= SHORT_CONV - 1 and h31 >= CONF_CONV - 1
    const2 = lambda b, s: (0, 0)
    return pl.pallas_call(
        _cd_mix_kernel,
        grid=(bsz, nst),
        in_specs=[
            pl.BlockSpec((ts, d), lambda b, s: (b * nst + s, 0)),
            pl.BlockSpec((ts, p.shape[1]), lambda b, s: (b * nst + s, 0)),
            pl.BlockSpec((1, 1, d), lambda b, s: (b, 0, 0)),
            pl.BlockSpec(c3w.shape, const2),
            pl.BlockSpec((1, wc), const2),
            pl.BlockSpec(c31w.shape, const2),
            pl.BlockSpec((1, wd), const2),
            pl.BlockSpec((1, wd), const2),
            pl.BlockSpec((1, wd), const2),
            pl.BlockSpec(w_out.shape, const2),
        ],
        out_specs=pl.BlockSpec((ts, d), lambda b, s: (b * nst + s, 0)),
        out_shape=jax.ShapeDtypeStruct((n, d), f32),
        scratch_shapes=[pltpu.VMEM((ts + h3, wc), f32), pltpu.VMEM((ts + h31, wd), f32),
                        pltpu.VMEM((ts, wd), f32), pltpu.VMEM((ts, wc + wd), bf16)],
        compiler_params=_cparams(("parallel", "arbitrary")),
        name="mixer_cd",
    )(x, p, g1, c3w, c3b.reshape(1, wc), c31w, c31b.reshape(1, wd), ln_g.reshape(1, wd), ln_b.reshape(1, wd),
      w_out)


def _route_kernel(x_ref, g_ref, sc_ref, sh_ref, rwh_ref, rwl_ref, rb_ref,
                  hp_ref, eidx_ref, wts_ref, rank_ref, cnt_ref, carry_ref):
    tm = x_ref.shape[0]
    ne = rwh_ref.shape[0]
    gsz = ne // N_GROUPS
    neg = -jnp.inf

    @pl.when(pl.program_id(0) == 0)
    def _():
        carry_ref[...] = jnp.zeros_like(carry_ref)

    h = _modulated_rmsnorm(x_ref[...], g_ref[...], sc_ref[0], sh_ref[0])
    hh = h.astype(bf16)
    hf = hh.astype(f32)
    hl = (h - hf).astype(bf16)
    hp_ref[...] = _rows_to_tiles(_pack_halves(hf))

    nt = (((1,), (1,)), ((), ()))
    logits = (lax.dot_general(rwh_ref[...], hh, nt, preferred_element_type=f32)
              + lax.dot_general(rwh_ref[...], hl, nt, preferred_element_type=f32)
              + lax.dot_general(rwl_ref[...], hh, nt, preferred_element_type=f32))
    scores = jax.nn.sigmoid(logits)
    choice = scores + rb_ref[...]

    c3 = choice.reshape(N_GROUPS, gsz, tm)
    sub = lax.broadcasted_iota(i32, c3.shape, 1)
    m1 = jnp.max(c3, axis=1, keepdims=True)
    first = jnp.min(jnp.where(c3 == m1, sub, gsz), axis=1, keepdims=True)
    m2 = jnp.max(jnp.where(sub == first, neg, c3), axis=1, keepdims=True)
    gs = m1 + m2
    keep = []
    for gi in range(N_GROUPS):
        beaten = jnp.zeros((1, tm), i32)
        for gj in range(N_GROUPS):
            if gj == gi:
                continue
            wins = (gs[gj] > gs[gi]) | ((gs[gj] == gs[gi]) & (gj < gi))
            beaten = beaten + wins.astype(i32)
        keep.append(beaten < TOPK_GROUPS)
    masked = jnp.concatenate(
        [jnp.where(keep[gi], c3[gi], neg) for gi in range(N_GROUPS)], axis=0)

    eio = lax.broadcasted_iota(i32, (ne, tm), 0)
    cur = masked
    onehot = jnp.zeros((ne, tm), f32)
    idxs, sels, svals = [], [], []
    for _ in range(TOP_K):
        m = jnp.max(cur, axis=0, keepdims=True)
        idx = jnp.min(jnp.where(cur == m, eio, ne), axis=0, keepdims=True)
        sel = eio == idx
        svals.append(jnp.sum(jnp.where(sel, scores, 0.0), axis=0, keepdims=True))
        cur = jnp.where(sel, neg, cur)
        onehot = onehot + sel.astype(f32)
        idxs.append(idx)
        sels.append(sel)
    ssum = svals[0]
    for sv in svals[1:]:
        ssum = ssum + sv

    tr = lax.broadcasted_iota(i32, (tm, tm), 0)
    tc = lax.broadcasted_iota(i32, (tm, tm), 1)
    upper = jnp.where(tr < tc, 1.0, 0.0).astype(bf16)
    before = jnp.dot(onehot.astype(bf16), upper, preferred_element_type=f32) + carry_ref[...]
    new_carry = carry_ref[...] + jnp.sum(onehot, axis=1, keepdims=True)
    carry_ref[...] = new_carry
    cnt_ref[...] = jnp.broadcast_to(new_carry, cnt_ref.shape).astype(i32)

    zrow_i = jnp.zeros((1, tm), i32)
    zrow_f = jnp.zeros((1, tm), f32)
    ranks = [jnp.sum(jnp.where(sel, before, 0.0), axis=0, keepdims=True).astype(i32) for sel in sels]
    pad = V7X_SUBLANES - TOP_K
    eidx_ref[...] = jnp.concatenate(idxs + [zrow_i] * pad, axis=0)
    wts_ref[...] = jnp.concatenate([sv / ssum * ROUTED_SCALE for sv in svals] + [zrow_f] * pad, axis=0)
    rank_ref[...] = jnp.concatenate(ranks + [zrow_i] * pad, axis=0)


def _route(x, g, sc, sh, router_w, router_b, seq):
    n, d = x.shape
    ne = router_w.shape[1]
    tm = min(TILE_ROUTE, seq)
    tiles_per_seq = seq // tm
    rwt = router_w.T
    rwh = rwt.astype(bf16)
    rwl = (rwt - rwh.astype(f32)).astype(bf16)
    slots = V7X_SUBLANES
    return pl.pallas_call(
        _route_kernel,
        grid=(n // tm,),
        in_specs=[
            pl.BlockSpec((tm, d), lambda i: (i, 0)),
            pl.BlockSpec((1, d), lambda i: (0, 0)),
            pl.BlockSpec((1, 1, d), lambda i: (i // tiles_per_seq, 0, 0)),
            pl.BlockSpec((1, 1, d), lambda i: (i // tiles_per_seq, 0, 0)),
            pl.BlockSpec((ne, d), lambda i: (0, 0)),
            pl.BlockSpec((ne, d), lambda i: (0, 0)),
            pl.BlockSpec((ne, 1), lambda i: (0, 0)),
        ],
        out_specs=[
            pl.BlockSpec((tm, d // 2 // V7X_LANES, V7X_LANES), lambda i: (i, 0, 0)),
            pl.BlockSpec((slots, tm), lambda i: (0, i)),
            pl.BlockSpec((slots, tm), lambda i: (0, i)),
            pl.BlockSpec((slots, tm), lambda i: (0, i)),
            pl.BlockSpec((ne, V7X_LANES), lambda i: (0, 0)),
        ],
        out_shape=[
            jax.ShapeDtypeStruct((n, d // 2 // V7X_LANES, V7X_LANES), u32),
            jax.ShapeDtypeStruct((slots, n), i32),
            jax.ShapeDtypeStruct((slots, n), f32),
            jax.ShapeDtypeStruct((slots, n), i32),
            jax.ShapeDtypeStruct((ne, V7X_LANES), i32),
        ],
        scratch_shapes=[pltpu.VMEM((ne, 1), f32)],
        compiler_params=_cparams(("arbitrary",)),
        name="route",
    )(x, g.reshape(1, d), sc, sh, rwh, rwl, router_b.reshape(ne, 1))


def _dest_kernel(poffs_ref, eidx_ref, rank_ref, dest_ref):
    eidx = eidx_ref[...]
    base = jnp.zeros(eidx.shape, i32)
    for e in range(N_EXPERTS):
        base = jnp.where(eidx == e, poffs_ref[e], base)
    dest_ref[...] = base + rank_ref[...]


def _dest(poffs, eidx_t, rank_t):
    slots, n = eidx_t.shape
    td = min(TILE_DEST, n)
    grid_spec = pltpu.PrefetchScalarGridSpec(
        num_scalar_prefetch=1,
        grid=(n // td,),
        in_specs=[pl.BlockSpec((slots, td), lambda i, po: (0, i)),
                  pl.BlockSpec((slots, td), lambda i, po: (0, i))],
        out_specs=pl.BlockSpec((slots, td), lambda i, po: (0, i)),
    )
    return pl.pallas_call(
        _dest_kernel,
        grid_spec=grid_spec,
        out_shape=jax.ShapeDtypeStruct((slots, n), i32),
        compiler_params=_cparams(("parallel",)),
        name="dest_rows",
    )(poffs, eidx_t, rank_t)


def _dispatch_kernel(nu_ref, zb_ref, dest_ref, hp_ref, xb_hbm, zero_ref, zsem, sem):
    i = pl.program_id(0)
    tm = hp_ref.shape[0]
    rows = zero_ref.shape[0]
    nz = zb_ref.shape[0]

    @pl.when(i == 0)
    def _():
        zero_ref[...] = jnp.zeros_like(zero_ref)

        def zcopy(j):
            return pltpu.make_async_copy(zero_ref, xb_hbm.at[pl.ds(zb_ref[j] * rows, rows)], zsem)

        def zstart(j, carry):
            @pl.when(zb_ref[j] >= 0)
            def _():
                zcopy(j).start()
            return carry

        def zwait(j, carry):
            @pl.when(zb_ref[j] >= 0)
            def _():
                zcopy(j).wait()
            return carry

        lax.fori_loop(0, nz, zstart, 0)
        lax.fori_loop(0, nz, zwait, 0)

    for t in range(tm):
        for k in range(TOP_K):
            pltpu.make_async_copy(hp_ref.at[t], xb_hbm.at[dest_ref[k, t]], sem).start(priority=k % 2)
    for k in range(TOP_K):
        pltpu.make_async_copy(hp_ref, xb_hbm.at[pl.ds(0, tm)], sem).wait()


def _dispatch(hp, dest_t, nused, zero_blocks, nb):
    n, s, lanes = hp.shape
    tm = min(TILE_DISPATCH, n)
    grid_spec = pltpu.PrefetchScalarGridSpec(
        num_scalar_prefetch=2,
        grid=(n // tm,),
        in_specs=[
            pl.BlockSpec((dest_t.shape[0], tm), lambda i, nu, zb: (0, i), memory_space=pltpu.SMEM),
            pl.BlockSpec((tm, s, lanes), lambda i, nu, zb: (i, 0, 0)),
        ],
        out_specs=pl.BlockSpec(memory_space=pl.ANY),
        scratch_shapes=[pltpu.VMEM((ROW_BLOCK, s, lanes), u32), pltpu.SemaphoreType.DMA(()),
                        pltpu.SemaphoreType.DMA(())],
    )
    return pl.pallas_call(
        _dispatch_kernel,
        grid_spec=grid_spec,
        out_shape=jax.ShapeDtypeStruct((nb * ROW_BLOCK, s, lanes), u32),
        compiler_params=_cparams(("arbitrary",)),
        name="dispatch",
    )(nused, zero_blocks, dest_t, hp)


def _expert_kernel(be_ref, nu_ref, eord_ref, elist_ref, nexp_ref, nv_ref, x_ref, w1_hbm, w3_hbm, w2_hbm, y_ref,
                   w1_st, w3_st, w2_st, w13_s, w2_s, wsem, *, layer):
    b = pl.program_id(0)
    nused = nu_ref[0]
    nvalid = nv_ref[b]
    ff = w2_s.shape[0]
    rows = x_ref.shape[0]
    cr = rows // EXPERT_CHUNKS

    def weight_copies(j, slot):
        e = elist_ref[j]
        return [pltpu.make_async_copy(src.at[layer, e], dst.at[slot], wsem.at[slot])
                for src, dst in ((w1_hbm, w1_st), (w3_hbm, w3_st), (w2_hbm, w2_st))]

    @pl.when(b == 0)
    def _():
        for cp in weight_copies(0, 0):
            cp.start()

    j = eord_ref[b]
    first = (b == 0) | (be_ref[b] != be_ref[jnp.maximum(b - 1, 0)])

    @pl.when(first & (b < nused))
    def _():
        slot = j % 2
        for cp in weight_copies(j, slot):
            cp.wait()

        @pl.when(j + 1 < nexp_ref[0])
        def _():
            for cp in weight_copies(j + 1, 1 - slot):
                cp.start()

        w13_s[:, :ff] = w1_st[slot].astype(bf16)
        w13_s[:, ff:] = w3_st[slot].astype(bf16)
        w2_s[...] = w2_st[slot].astype(bf16)

    def ffn_chunk(c):
        rs = slice(c * cr, (c + 1) * cr)
        lo, hi = _unpack_halves(_tiles_to_rows(x_ref[rs]))
        x = jnp.concatenate([lo, hi], axis=1).astype(bf16)
        hcat = jnp.dot(x, w13_s[...], preferred_element_type=f32)
        a = jax.nn.silu(hcat[:, :ff]) * hcat[:, ff:]
        y = jnp.dot(a.astype(bf16), w2_s[...], preferred_element_type=f32)
        y_ref[rs] = _rows_to_tiles(_pack_halves(y.astype(bf16).astype(f32)))

    @pl.when((b < nused) & (nvalid > rows - cr))
    def _():
        for c in range(EXPERT_CHUNKS):
            ffn_chunk(c)

    for c in range(EXPERT_CHUNKS):
        @pl.when((b < nused) & (nvalid <= rows - cr) & (nvalid > c * cr))
        def _():
            ffn_chunk(c)

        @pl.when((b >= nused) | (nvalid <= c * cr))
        def _():
            y_ref[c * cr:(c + 1) * cr] = jnp.zeros((cr,) + y_ref.shape[1:], y_ref.dtype)


def _experts(xbuf, block_e, nused, expert_ord, expert_list, nexp, nvalid, w1, w3, w2, layer):
    nrows, s, lanes = xbuf.shape
    nb = block_e.shape[0]
    _, ne, d, ff = w1.shape
    rows = ROW_BLOCK
    grid_spec = pltpu.PrefetchScalarGridSpec(
        num_scalar_prefetch=6,
        grid=(nb,),
        in_specs=[
            pl.BlockSpec((rows, s, lanes), lambda b, be, nu, eo, el, nx, nv: (jnp.minimum(b, nu[0] - 1), 0, 0)),
            pl.BlockSpec(memory_space=pl.ANY),
            pl.BlockSpec(memory_space=pl.ANY),
            pl.BlockSpec(memory_space=pl.ANY),
        ],
        out_specs=pl.BlockSpec((rows, s, lanes), lambda b, be, nu, eo, el, nx, nv: (b, 0, 0)),
        scratch_shapes=[pltpu.VMEM((2, d, ff), f32), pltpu.VMEM((2, d, ff), f32), pltpu.VMEM((2, ff, d), f32),
                        pltpu.VMEM((d, 2 * ff), bf16), pltpu.VMEM((ff, d), bf16),
                        pltpu.SemaphoreType.DMA((2,))],
    )
    return pl.pallas_call(
        functools.partial(_expert_kernel, layer=layer),
        grid_spec=grid_spec,
        out_shape=jax.ShapeDtypeStruct((nrows, s, lanes), u32),
        compiler_params=_cparams(("arbitrary",)),
        name="experts",
    )(block_e, nused, expert_ord, expert_list, nexp, nvalid, xbuf, w1, w3, w2)


def _combine_kernel(dc_ref, dn_ref, x_ref, hp_ref, w_ref, g2_ref, sw13_ref, sw2_ref, fg_ref, y_hbm,
                    o_ref, buf0_ref, buf1_ref, sem, *, final):
    i = pl.program_id(0)
    nsteps = pl.num_programs(0)
    slot = i % 2
    tm = x_ref.shape[0]
    half = hp_ref.shape[1] * hp_ref.shape[2]
    sff = sw2_ref.shape[0]
    bufs = (buf0_ref, buf1_ref)

    def row_copy(d_ref, s, t, k):
        return pltpu.make_async_copy(y_hbm.at[d_ref[k, t]], bufs[s].at[k * tm + t], sem.at[s])

    def drain(s):
        for k in range(TOP_K):
            pltpu.make_async_copy(y_hbm.at[pl.ds(0, tm)], bufs[s].at[pl.ds(k * tm, tm)], sem.at[s]).wait()

    @pl.when(i == 0)
    def _():
        def body(t, carry):
            for k in range(TOP_K):
                row_copy(dc_ref, 0, t, k).start(priority=k % 2)
            return carry

        lax.fori_loop(0, tm, body, 0, unroll=2)

    def step(s):
        drain(s)
        for t in range(tm):
            for k in range(TOP_K):
                row_copy(dn_ref, 1 - s, t, k).start(priority=k % 2)

        lo, hi = _unpack_halves(_tiles_to_rows(hp_ref[...]))
        hx = jnp.concatenate([lo, hi], axis=1).astype(bf16)
        hcat = jnp.dot(hx, sw13_ref[...], preferred_element_type=f32)
        a = jax.nn.silu(hcat[:, :sff]) * hcat[:, sff:]
        shared = jnp.dot(a.astype(bf16), sw2_ref[...], preferred_element_type=f32)

        w = w_ref[...]
        acc_lo = shared[:, :half]
        acc_hi = shared[:, half:]
        for k in range(TOP_K):
            ylo, yhi = _unpack_halves(_tiles_to_rows(bufs[s][k * tm:(k + 1) * tm]))
            acc_lo = acc_lo + w[:, k:k + 1] * ylo
            acc_hi = acc_hi + w[:, k:k + 1] * yhi
        g2 = g2_ref[0]
        out_lo = x_ref[:, :half] + g2[:, :half] * acc_lo
        out_hi = x_ref[:, half:] + g2[:, half:] * acc_hi
        if final:
            ms = (jnp.sum(out_lo * out_lo, axis=-1, keepdims=True)
                  + jnp.sum(out_hi * out_hi, axis=-1, keepdims=True)) / (2 * half)
            inv = lax.rsqrt(ms + EPS)
            out_lo = out_lo * inv * fg_ref[:, :half]
            out_hi = out_hi * inv * fg_ref[:, half:]
        o_ref[:, :half] = out_lo
        o_ref[:, half:] = out_hi

    for s in range(2):
        @pl.when(slot == s)
        def _():
            step(s)

        @pl.when((slot == s) & (i + 1 == nsteps))
        def _():
            drain(1 - s)


def _combine(x, hp, wts, dest_t, g2, sw13, sw2, final_g, ybuf, seq, final):
    n, d = x.shape
    _, s, lanes = hp.shape
    tm = min(TILE_COMB, seq)
    nt = n // tm
    tiles_per_seq = seq // tm
    slots = dest_t.shape[0]
    return pl.pallas_call(
        functools.partial(_combine_kernel, final=final),
        grid=(nt,),
        in_specs=[
            pl.BlockSpec((slots, tm), lambda i: (0, i), memory_space=pltpu.SMEM),
            pl.BlockSpec((slots, tm), lambda i: (0, jnp.minimum(i + 1, nt - 1)), memory_space=pltpu.SMEM),
            pl.BlockSpec((tm, d), lambda i: (i, 0)),
            pl.BlockSpec((tm, s, lanes), lambda i: (i, 0, 0)),
            pl.BlockSpec((tm, V7X_SUBLANES), lambda i: (i, 0)),
            pl.BlockSpec((1, 1, d), lambda i: (i // tiles_per_seq, 0, 0)),
            pl.BlockSpec(sw13.shape, lambda i: (0, 0)),
            pl.BlockSpec(sw2.shape, lambda i: (0, 0)),
            pl.BlockSpec((1, d), lambda i: (0, 0)),
            pl.BlockSpec(memory_space=pl.ANY),
        ],
        out_specs=pl.BlockSpec((tm, d), lambda i: (i, 0)),
        out_shape=jax.ShapeDtypeStruct((n, d), f32),
        scratch_shapes=[pltpu.VMEM((tm * TOP_K, s, lanes), u32), pltpu.VMEM((tm * TOP_K, s, lanes), u32),
                        pltpu.SemaphoreType.DMA((2,))],
        compiler_params=_cparams(("arbitrary",)),
        name="combine",
    )(dest_t, dest_t, x, hp, wts.T, g2, sw13, sw2, final_g.reshape(1, d), ybuf)


def _moe(x, g, sc, sh, g2, router_w, router_b, w1, w3, w2, layer, sw1, sw3, sw2, final_g, seq, final):
    n, d = x.shape
    ne = router_w.shape[1]
    hp, eidx_t, wts_t, rank_t, cnt = _route(x, g, sc, sh, router_w, router_b, seq)
    counts = cnt[:, 0]
    pcounts = (counts + ROW_BLOCK - 1) // ROW_BLOCK * ROW_BLOCK
    pends = jnp.cumsum(pcounts)
    poffs = pends - pcounts
    nb = -(-(n * TOP_K) // ROW_BLOCK) + ne
    starts = jnp.arange(nb, dtype=i32) * ROW_BLOCK
    before = pends[None, :] <= starts[:, None]
    block_e = jnp.minimum(jnp.sum(before.astype(i32), axis=1), ne - 1)
    nused = (pends[-1:] // ROW_BLOCK).astype(i32)
    real_end = jnp.min(jnp.where(before, jnp.iinfo(jnp.int32).max, (poffs + counts)[None, :]), axis=1)
    nvalid = jnp.where(starts < pends[-1], jnp.clip(real_end - starts, 0, ROW_BLOCK), 0).astype(i32)
    has_rows = pcounts > 0
    last_blocks = jnp.where(has_rows, pends // ROW_BLOCK - 1, -1)
    tail = nused[0] + jnp.arange(nb - (n * TOP_K) // ROW_BLOCK, dtype=i32)
    zero_blocks = jnp.concatenate([last_blocks, jnp.where(tail < nb, tail, -1)]).astype(i32)
    dest_t = _dest(poffs.astype(i32), eidx_t, rank_t)
    xbuf = _dispatch(hp, dest_t, nused, zero_blocks, nb)
    expert_list = jnp.nonzero(has_rows, size=ne, fill_value=ne - 1)[0].astype(i32)
    nexp = jnp.sum(has_rows.astype(i32)).reshape(1)
    expert_ord = jnp.minimum(jnp.sum((before & has_rows[None, :]).astype(i32), axis=1), nexp - 1)
    ybuf = _experts(xbuf, block_e, nused, expert_ord, expert_list, nexp, nvalid, w1, w3, w2, layer)
    sw13 = jnp.concatenate([sw1, sw3], axis=-1).astype(bf16)
    return _combine(x, hp, wts_t, dest_t, g2, sw13, sw2.astype(bf16), final_g, ybuf, seq, final)


def kernel(x, c, ada_w, ada_b, norm_mix_g, norm_ffn_g, ab_w_in, sgu_ln_g, sgu_ln_b, sgu_w, sgu_b, pool_w,
           pool_scale, ab_w_out, cd_w_in, conv3_w, conv3_b, conv31_w, conv31_b, cd_ln_g, cd_ln_b, cd_w_out,
           router_w, router_b, exp_w1, exp_w3, exp_w2, sh_w1, sh_w3, sh_w2, final_g):
    bsz, seq, d = x.shape
    depth = ada_w.shape[0]
    n = bsz * seq
    xf = x.reshape(n, d)
    mod = _ada(c, ada_w, ada_b)
    for l in range(depth):
        sh1, sc1, g1, sh2, sc2, g2 = [mod[l, :, i * d:(i + 1) * d].reshape(bsz, 1, d) for i in range(6)]
        if l % 2 == 0:
            e = l // 2
            p = _norm_mm(xf, norm_mix_g[l], sc1, sh1, ab_w_in[e].astype(bf16), seq)
            xf = _ab_mix(xf, p, g1, sgu_ln_g[e], sgu_ln_b[e], sgu_w[e], sgu_b[e], pool_w[e], pool_scale[e],
                         ab_w_out[e].astype(bf16), bsz, seq)
        else:
            o = l // 2
            p = _norm_mm(xf, norm_mix_g[l], sc1, sh1, cd_w_in[o].astype(bf16), seq)
            xf = _cd_mix(xf, p, g1, conv3_w[o], conv3_b[o], conv31_w[o], conv31_b[o], cd_ln_g[o], cd_ln_b[o],
                         cd_w_out[o].astype(bf16), bsz, seq)
        xf = _moe(xf, norm_ffn_g[l], sc2, sh2, g2, router_w[l], router_b[l], exp_w1, exp_w3, exp_w2, l,
                  sh_w1[l], sh_w3[l], sh_w2[l], final_g, seq, final=(l == depth - 1))
    return xf.reshape(bsz, seq, d)
```
